```python
import math
import jax
import jax.numpy as jnp
from jax import lax
import numpy as np

D_MODEL = 1024
BATCH = 8
SEQ = 2048
DEPTH = 2
DEC_BATCH = 128
DEC_SEQ = 8
PAST_LEN = 16384
PAGE_SIZE = 128

N_EVEN = (DEPTH + 1) // 2
N_ODD = DEPTH // 2

RET_HEADS = 4
RET_DK = 64
RET_DV = 128
RET_CHUNK = 64
RET_ROPE_BASE = 10000.0
RET_Q = RET_HEADS * RET_DK
RET_V = RET_HEADS * RET_DV

RWKV_HEADS = 8
RWKV_N = 64
RWKV_W = RWKV_HEADS * RWKV_N
RWKV_LORA_W = 64
RWKV_LORA_A = 64
RWKV_GN_EPS = 64e-5
SHIFT_W = 3 * RWKV_W + RWKV_LORA_W + RWKV_LORA_A

GDN_HEADS = 8
GDN_DK = 128
GDN_DV = 128
GDN_CONV = 4
GDN_CHUNK = 64
GDN_QK = GDN_HEADS * GDN_DK
GDN_VW = GDN_HEADS * GDN_DV
GDN_QKV = 2 * GDN_QK + GDN_VW

E_IN = 2 * RET_Q + 2 * RET_V + SHIFT_W + RWKV_W
E_MIX = RET_V + RWKV_W
O_IN = GDN_QKV + GDN_VW + 2 * GDN_HEADS
O_MIX = GDN_VW

NORM_EPS = 1e-6
L2_EPS = 1e-12

kernel_name = 'hybrid_retention_rwkv7_gdn_step'


def rmsnorm(x, w):
    xf = x.astype(jnp.float32)
    return xf * lax.rsqrt(jnp.mean(xf * xf, -1, keepdims=True) + NORM_EPS) * w.astype(jnp.float32)


def head_rmsnorm(y):
    return y * lax.rsqrt(jnp.mean(y * y, -1, keepdims=True) + NORM_EPS)


def head_groupnorm(y, eps):
    mu = jnp.mean(y, -1, keepdims=True)
    var = jnp.mean(jnp.square(y - mu), -1, keepdims=True)
    return (y - mu) * lax.rsqrt(var + eps)


def l2norm(x):
    return x * lax.rsqrt(jnp.sum(x * x, -1, keepdims=True) + L2_EPS)


def rotary(x, pos):
    half = x.shape[-1] // 2
    inv = 1.0 / (RET_ROPE_BASE ** jnp.linspace(0.0, 1.0, half, dtype=jnp.float32))
    ang = pos.astype(jnp.float32)[:, None] * inv[None, :]
    cos = jnp.cos(ang)[None, :, None, :]
    sin = jnp.sin(ang)[None, :, None, :]
    x1, x2 = x[..., 0::2], x[..., 1::2]
    return jnp.stack([x1 * cos - x2 * sin, x1 * sin + x2 * cos], -1).reshape(x.shape)


def to_chunks(t, C):
    B, L = t.shape[:2]
    t = t.reshape(B, L // C, C, *t.shape[2:])
    return jnp.moveaxis(t, (1, 3), (0, 2))


def from_chunks(t):
    t = jnp.moveaxis(t, (0, 2), (1, 3))
    return t.reshape(t.shape[0], -1, *t.shape[3:])


def retention_chunked(q, k, v, S0):
    B, L, H, _ = q.shape
    C = math.gcd(L, RET_CHUNK)
    lg = jnp.log1p(-jnp.exp2(-jnp.linspace(5.0, 12.0, H, dtype=jnp.float32)))
    ci = jnp.arange(C, dtype=jnp.float32)
    diff = ci[:, None] - ci[None, :]
    causal = diff >= 0
    decay = jnp.where(causal, jnp.exp(jnp.where(causal, diff, 0.0) * lg[:, None, None]), 0.0)
    q_scale = jnp.exp((ci + 1.0) * lg[:, None])[..., None]
    k_scale = jnp.exp((C - 1.0 - ci) * lg[:, None])[..., None]
    s_scale = jnp.exp(C * lg)[:, None, None]

    def step(S, inp):
        qc, kc, vc = inp
        intra = jnp.einsum('bhij,bhjv->bhiv', jnp.einsum('bhik,bhjk->bhij', qc, kc) * decay, vc)
        o = jnp.einsum('bhck,bhkv->bhcv', qc * q_scale, S) + intra
        S = S * s_scale + jnp.einsum('bhck,bhcv->bhkv', kc * k_scale, vc)
        return S, o

    S, o = lax.scan(step, S0.astype(jnp.float32), (to_chunks(q, C), to_chunks(k, C), to_chunks(v, C)))
    return from_chunks(o), S


def rwkv7_recurrence(r, log_decay, k, v, kk, a, S0):
    def step(S, inp):
        r_t, lw_t, k_t, v_t, kk_t, a_t = inp
        s_kk = jnp.einsum('bhvk,bhk->bhv', S, kk_t)
        S = (S * jnp.exp(lw_t)[:, :, None, :] - s_kk[..., None] * (kk_t * a_t)[:, :, None, :]
             + v_t[..., None] * k_t[:, :, None, :])
        return S, jnp.einsum('bhvk,bhk->bhv', S, r_t)

    xs = tuple(t.transpose(1, 0, 2, 3) for t in (r, log_decay, k, v, kk, a))
    S, y = lax.scan(step, S0.astype(jnp.float32), xs)
    return y.transpose(1, 0, 2, 3), S


def gated_delta_chunked(q, k, v, beta, g, S0):
    B, L, H, _ = q.shape
    dv = v.shape[-1]
    C = math.gcd(L, GDN_CHUNK)
    ci = jnp.arange(C)
    causal = ci[:, None] >= ci[None, :]
    strict = ci[:, None] > ci[None, :]
    eye = jnp.eye(C, dtype=jnp.float32)

    def step(S, inp):
        qc, kc, vc, bc, gc = inp
        gcum = jnp.cumsum(gc, -1)
        diff = gcum[..., :, None] - gcum[..., None, :]
        decay = jnp.where(causal, jnp.exp(jnp.where(causal, diff, 0.0)), 0.0)
        kb = kc * bc[..., None]
        lower = jnp.where(strict, jnp.einsum('bhik,bhjk->bhij', kb, kc) * decay, 0.0)
        rhs = jnp.concatenate([vc * bc[..., None], kb * jnp.exp(gcum)[..., None]], -1)
        sol = lax.linalg.triangular_solve(eye + lower, rhs, left_side=True, lower=True, unit_diagonal=True)
        u, w = sol[..., :dv], sol[..., dv:]
        v_new = u - jnp.einsum('bhck,bhkv->bhcv', w, S)
        attn = jnp.einsum('bhik,bhjk->bhij', qc, kc) * decay
        o = jnp.einsum('bhck,bhkv->bhcv', qc * jnp.exp(gcum)[..., None], S) + jnp.einsum('bhij,bhjv->bhiv', attn, v_new)
        g_last = gcum[..., -1:]
        S = S * jnp.exp(g_last)[..., None] + jnp.einsum('bhck,bhcv->bhkv', kc * jnp.exp(g_last - gcum)[..., None], v_new)
        return S, o

    xs = (to_chunks(q, C), to_chunks(k, C), to_chunks(v, C), to_chunks(beta, C), to_chunks(g, C))
    S, o = lax.scan(step, S0.astype(jnp.float32), xs)
    return from_chunks(o), S


def even_layer(h, pos, s_ret, s_rwkv, s_shift, norm_w, w_in, mu, w0, w2, a0, a2, k_k, k_a, r_k, ln_w, ln_b, ret_w, w_out):
    B, L, _ = h.shape
    p = jnp.einsum('bld,de->ble', rmsnorm(h, norm_w), w_in).astype(jnp.float32)
    cuts = np.cumsum([RET_Q, RET_Q, RET_V, RET_V, SHIFT_W]).tolist()
    rq, rk, rv, rg, sh, gb = jnp.split(p, cuts, axis=-1)
    q = rotary(rq.reshape(B, L, RET_HEADS, RET_DK), pos)
    k = rotary(rk.reshape(B, L, RET_HEADS, RET_DK), pos) * (RET_DK ** -0.5)
    v = rv.reshape(B, L, RET_HEADS, RET_DV)
    o, s_ret_new = retention_chunked(q, k, v, s_ret)
    o_ret = (head_rmsnorm(o) * ret_w.reshape(RET_HEADS, RET_DV)).reshape(B, L, RET_V) * jax.nn.silu(rg)
    prev = jnp.concatenate([s_shift[:, None, :].astype(jnp.float32), sh[:, :-1]], axis=1)
    xs = sh + (prev - sh) * mu
    r, kr, vr, wd, ad = jnp.split(xs, [RWKV_W, 2 * RWKV_W, 3 * RWKV_W, 3 * RWKV_W + RWKV_LORA_W], axis=-1)
    w = -jax.nn.softplus(-(w0 + jnp.tanh(wd) @ w2)) - 0.5
    log_decay = -jnp.exp(w)
    a = jax.nn.sigmoid(a0 + ad @ a2)
    hd = lambda t: t.reshape(B, L, RWKV_HEADS, RWKV_N)
    kk = l2norm(hd(kr * k_k))
    kr = kr * (1.0 + (a - 1.0) * k_a)
    r, kr, vr, a, log_decay = hd(r), hd(kr), hd(vr), hd(a), hd(log_decay)
    y, s_rwkv_new = rwkv7_recurrence(r, log_decay, kr, vr, kk, a, s_rwkv)
    y = head_groupnorm(y, RWKV_GN_EPS) * ln_w.reshape(RWKV_HEADS, RWKV_N) + ln_b.reshape(RWKV_HEADS, RWKV_N)
    y = y + jnp.sum(r * kr * r_k, -1, keepdims=True) * vr
    o_rwkv = y.reshape(B, L, RWKV_W) * jax.nn.silu(gb)
    out = jnp.concatenate([o_ret, o_rwkv], axis=-1) @ w_out
    return h + out.astype(h.dtype), s_ret_new, s_rwkv_new, sh[:, -1]


def odd_layer(h, s_gdn, s_conv, norm_w, w_in, conv_w, a_log, dt_bias, gn_w, w_out):
    B, L, _ = h.shape
    p = jnp.einsum('bld,de->ble', rmsnorm(h, norm_w), w_in).astype(jnp.float32)
    qkv, z, b_raw, a_raw = jnp.split(p, [GDN_QKV, GDN_QKV + GDN_VW, GDN_QKV + GDN_VW + GDN_HEADS], axis=-1)
    xp = jnp.concatenate([s_conv.astype(jnp.float32), qkv], axis=1)
    conv = sum(xp[:, j:j + L] * conv_w[j] for j in range(GDN_CONV))
    qkv = jax.nn.silu(conv)
    q, k, v = jnp.split(qkv, [GDN_QK, 2 * GDN_QK], axis=-1)
    q = l2norm(q.reshape(B, L, GDN_HEADS, GDN_DK)) * (GDN_DK ** -0.5)
    k = l2norm(k.reshape(B, L, GDN_HEADS, GDN_DK))
    v = v.reshape(B, L, GDN_HEADS, GDN_DV)
    beta = jax.nn.sigmoid(b_raw)
    g = -jnp.exp(a_log.astype(jnp.float32)) * jax.nn.softplus(a_raw + dt_bias)
    o, s_gdn_new = gated_delta_chunked(q, k, v, beta, g, s_gdn)
    o = (head_rmsnorm(o) * gn_w).reshape(B, L, O_MIX) * jax.nn.silu(z)
    out = o @ w_out
    return h + out.astype(h.dtype), s_gdn_new, xp[:, -(GDN_CONV - 1):]


def setup_inputs(seed: int = 0) -> dict:
    key = jax.random.key(seed)
    keys = list(jax.random.split(key, 40))

    def nrm(shape, scale):
        return jax.random.normal(keys.pop(), shape, jnp.float32) * scale

    def uni(shape, lo, hi):
        return jax.random.uniform(keys.pop(), shape, jnp.float32, lo, hi)

    dt = jnp.exp(uni((N_ODD, GDN_HEADS), math.log(1e-3), math.log(1e-1)))
    return {
        'x_prompt': nrm((BATCH, SEQ, D_MODEL), 1.0),
        'x_sample': nrm((DEC_BATCH, DEC_SEQ, D_MODEL), 1.0),
        'state_ret': nrm((N_EVEN, DEC_BATCH, RET_HEADS, RET_DK, RET_DV), 0.1),
        'state_rwkv': nrm((N_EVEN, DEC_BATCH, RWKV_HEADS, RWKV_N, RWKV_N), 0.1),
        'state_shift': nrm((N_EVEN, DEC_BATCH, SHIFT_W), 1.0),
        'state_gdn': nrm((N_ODD, DEC_BATCH, GDN_HEADS, GDN_DK, GDN_DV), 0.1),
        'state_conv': nrm((N_ODD, DEC_BATCH, GDN_CONV - 1, GDN_QKV), 1.0),
        'norm_e': 1.0 + nrm((N_EVEN, D_MODEL), 0.02),
        'w_in_e': nrm((N_EVEN, D_MODEL, E_IN), D_MODEL ** -0.5),
        'rwkv_mu': uni((N_EVEN, SHIFT_W), 0.0, 1.0),
        'rwkv_w0': uni((N_EVEN, RWKV_W), -6.0, -1.0),
        'rwkv_w2': nrm((N_EVEN, RWKV_LORA_W, RWKV_W), 0.1 * RWKV_LORA_W ** -0.5),
        'rwkv_a0': nrm((N_EVEN, RWKV_W), 0.1),
        'rwkv_a2': nrm((N_EVEN, RWKV_LORA_A, RWKV_W), 0.1 * RWKV_LORA_A ** -0.5),
        'rwkv_kk': 0.85 + nrm((N_EVEN, RWKV_W), 0.02),
        'rwkv_ka': 1.0 + nrm((N_EVEN, RWKV_W), 0.02),
        'rwkv_rk': nrm((N_EVEN, RWKV_HEADS, RWKV_N), 0.1),
        'rwkv_ln_w': 1.0 + nrm((N_EVEN, RWKV_W), 0.02),
        'rwkv_ln_b': nrm((N_EVEN, RWKV_W), 0.02),
        'ret_norm_w': 1.0 + nrm((N_EVEN, RET_V), 0.02),
        'w_out_e': nrm((N_EVEN, E_MIX, D_MODEL), E_MIX ** -0.5),
        'norm_o': 1.0 + nrm((N_ODD, D_MODEL), 0.02),
        'w_in_o': nrm((N_ODD, D_MODEL, O_IN), D_MODEL ** -0.5),
        'gdn_conv_w': nrm((N_ODD, GDN_CONV, GDN_QKV), GDN_CONV ** -0.5),
        'gdn_a_log': jnp.log(uni((N_ODD, GDN_HEADS), 1.0, 16.0)),
        'gdn_dt_bias': dt + jnp.log(-jnp.expm1(-dt)),
        'gdn_norm_w': 1.0 + nrm((N_ODD, GDN_DV), 0.02),
        'w_out_o': nrm((N_ODD, O_MIX, D_MODEL), O_MIX ** -0.5),
        'final_norm': 1.0 + nrm((D_MODEL,), 0.02),
    }


def reference(x_prompt, x_sample, state_ret, state_rwkv, state_shift, state_gdn, state_conv,
              norm_e, w_in_e, rwkv_mu, rwkv_w0, rwkv_w2, rwkv_a0, rwkv_a2, rwkv_kk, rwkv_ka, rwkv_rk,
              rwkv_ln_w, rwkv_ln_b, ret_norm_w, w_out_e,
              norm_o, w_in_o, gdn_conv_w, gdn_a_log, gdn_dt_bias, gdn_norm_w, w_out_o, final_norm):
    def run(h, pos, s_ret, s_rwkv, s_shift, s_gdn, s_conv):
        ret_l, rwkv_l, shift_l, gdn_l, conv_l = [], [], [], [], []
        for layer in range(DEPTH):
            i = layer // 2
            if layer % 2 == 0:
                h, n_ret, n_rwkv, n_shift = even_layer(
                    h, pos, s_ret[i], s_rwkv[i], s_shift[i], norm_e[i], w_in_e[i], rwkv_mu[i],
                    rwkv_w0[i], rwkv_w2[i], rwkv_a0[i], rwkv_a2[i], rwkv_kk[i], rwkv_ka[i], rwkv_rk[i],
                    rwkv_ln_w[i], rwkv_ln_b[i], ret_norm_w[i], w_out_e[i])
                ret_l.append(n_ret)
                rwkv_l.append(n_rwkv)
                shift_l.append(n_shift)
            else:
                h, n_gdn, n_conv = odd_layer(
                    h, s_gdn[i], s_conv[i], norm_o[i], w_in_o[i], gdn_conv_w[i], gdn_a_log[i],
                    gdn_dt_bias[i], gdn_norm_w[i], w_out_o[i])
                gdn_l.append(n_gdn)
                conv_l.append(n_conv)
        y = rmsnorm(h, final_norm).astype(h.dtype)
        return y, jnp.stack(ret_l), jnp.stack(rwkv_l), jnp.stack(shift_l), jnp.stack(gdn_l), jnp.stack(conv_l)

    f32 = jnp.float32
    y_prompt, ret_p, rwkv_p, shift_p, gdn_p, conv_p = run(
        x_prompt, jnp.arange(SEQ),
        jnp.zeros((N_EVEN, BATCH, RET_HEADS, RET_DK, RET_DV), f32),
        jnp.zeros((N_EVEN, BATCH, RWKV_HEADS, RWKV_N, RWKV_N), f32),
        jnp.zeros((N_EVEN, BATCH, SHIFT_W), f32),
        jnp.zeros((N_ODD, BATCH, GDN_HEADS, GDN_DK, GDN_DV), f32),
        jnp.zeros((N_ODD, BATCH, GDN_CONV - 1, GDN_QKV), f32))
    y_sample, ret_s, rwkv_s, shift_s, gdn_s, conv_s = run(
        x_sample, PAST_LEN + jnp.arange(DEC_SEQ),
        state_ret, state_rwkv, state_shift, state_gdn, state_conv)
    return (y_prompt, y_sample, ret_p, rwkv_p, shift_p, gdn_p, conv_p, ret_s, rwkv_s, shift_s, gdn_s, conv_s)
```

```python
import functools
import math

import jax
import jax.numpy as jnp
from jax import lax
from jax.experimental import pallas as pl
from jax.experimental.pallas import tpu as pltpu

F32 = jnp.float32
BF16 = jnp.bfloat16
HIGHEST = lax.Precision.HIGHEST

D_MODEL = 1024
RET_HEADS, RET_DK, RET_DV = 4, 64, 128
RET_ROPE_BASE = 10000.0
RET_Q = RET_HEADS * RET_DK
RET_V = RET_HEADS * RET_DV
RET_W = 2 * RET_Q + 2 * RET_V
RWKV_HEADS, RWKV_N = 8, 64
RWKV_W = RWKV_HEADS * RWKV_N
RWKV_LORA = 64
RWKV_GN_EPS = 64e-5
SHIFT_W = 3 * RWKV_W + 2 * RWKV_LORA
RWKV_IN = SHIFT_W + RWKV_W
GDN_HEADS, GDN_DK, GDN_DV, GDN_CONV = 8, 128, 128, 4
GDN_QK = GDN_HEADS * GDN_DK
GDN_VW = GDN_HEADS * GDN_DV
GDN_QKV = 2 * GDN_QK + GDN_VW
NORM_EPS = 1e-6
L2_EPS = 1e-12

SUBLANES = 8
INV_BLOCK = 16
VMEM_LIMIT = 56 * 1024 * 1024
PROMPT_CHUNK = 64
TOKEN_TILE = 256


def _mm(a, b, precision=HIGHEST):
    return lax.dot_general(a, b, (((1,), (0,)), ((), ())), precision=precision, preferred_element_type=F32)


def _mm_nt(a, b, precision=HIGHEST):
    return lax.dot_general(a, b, (((1,), (1,)), ((), ())), precision=precision, preferred_element_type=F32)


def _mm_tn(a, b, precision=HIGHEST):
    return lax.dot_general(a, b, (((0,), (0,)), ((), ())), precision=precision, preferred_element_type=F32)


def _silu(x):
    return x * jax.nn.sigmoid(x)


def _softplus(x):
    return jnp.maximum(x, 0.0) + jnp.log1p(jnp.exp(-jnp.abs(x)))


def _square_masks(n):
    ri = lax.broadcasted_iota(jnp.int32, (n, n), 0)
    ci = lax.broadcasted_iota(jnp.int32, (n, n), 1)
    return ri, ci


def _neumann_inverse(low, eye, n):
    inv = eye - low
    power = low
    k = 2
    while k < n:
        power = _mm(power, power)
        inv = inv + _mm(inv, power)
        k *= 2
    return inv


def _unit_lower_inverse(low, n):
    ri, ci = _square_masks(n)
    eye = (ri == ci).astype(F32)
    if n <= INV_BLOCK:
        return _neumann_inverse(low, eye, n)
    shift = int(math.log2(INV_BLOCK))
    same_block = (ri >> shift) == (ci >> shift)
    diag_inv = _neumann_inverse(jnp.where(same_block, low, 0.0), eye, INV_BLOCK)
    off = _mm(diag_inv, jnp.where(same_block, 0.0, low))
    return _mm(_neumann_inverse(off, eye, n // INV_BLOCK), diag_inv)


def _rmsnorm_rows(x, w):
    return x * lax.rsqrt(jnp.mean(x * x, -1, keepdims=True) + NORM_EPS) * w


def _even_in_kernel(x_ref, nw_ref, w_ref, ret_ref, rwkv_ref):
    xn = _rmsnorm_rows(x_ref[...], nw_ref[...]).astype(BF16)
    ret_ref[...] = jnp.dot(xn, w_ref[:, :RET_W], preferred_element_type=F32)
    rwkv_ref[...] = jnp.dot(xn, w_ref[:, RET_W:], preferred_element_type=F32)


def _even_out_odd_in_kernel(h_ref, oret_ref, orwkv_ref, wout_ref, nw_ref, win_ref, wba_ref,
                            h1_ref, qkv_ref, z_ref, ba_ref):
    mix = jnp.dot(oret_ref[...].astype(BF16), wout_ref[:RET_V, :], preferred_element_type=F32)
    mix = mix + jnp.dot(orwkv_ref[...].astype(BF16), wout_ref[RET_V:, :], preferred_element_type=F32)
    h1 = h_ref[...] + mix
    h1_ref[...] = h1
    xn = _rmsnorm_rows(h1, nw_ref[...]).astype(BF16)
    qkv_ref[...] = jnp.dot(xn, win_ref[:, :GDN_QKV], preferred_element_type=F32)
    z_ref[...] = jnp.dot(xn, win_ref[:, GDN_QKV:], preferred_element_type=F32)
    ba_ref[...] = jnp.dot(xn, wba_ref[...], preferred_element_type=F32)


def _odd_out_final_kernel(h_ref, o_ref, wout_ref, nw_ref, y_ref):
    h2 = h_ref[...] + jnp.dot(o_ref[...].astype(BF16), wout_ref[...], preferred_element_type=F32)
    y_ref[...] = _rmsnorm_rows(h2, nw_ref[...])


def _row_spec(tile, width):
    return pl.BlockSpec((tile, width), lambda i: (i, 0))


def _full_spec(shape):
    return pl.BlockSpec(shape, lambda i: (0,) * len(shape))


def _token_call(kernel, n_tokens, row_inputs, full_inputs, out_widths, order, name):
    tile = TOKEN_TILE
    assert n_tokens % tile == 0
    specs = {**{k: _row_spec(tile, v.shape[1]) for k, v in row_inputs.items()},
             **{k: _full_spec(v.shape) for k, v in full_inputs.items()}}
    arrays = {**row_inputs, **full_inputs}
    return pl.pallas_call(
        kernel,
        grid=(n_tokens // tile,),
        in_specs=[specs[k] for k in order],
        out_specs=[_row_spec(tile, w) for w in out_widths],
        out_shape=[jax.ShapeDtypeStruct((n_tokens, w), F32) for w in out_widths],
        compiler_params=pltpu.CompilerParams(dimension_semantics=("parallel",), vmem_limit_bytes=VMEM_LIMIT),
        name=name,
    )(*[arrays[k] for k in order])


def _ret_kernel(p_ref, cos_ref, sin_ref, dec_ref, qs_ref, ks_ref, ss_ref, s0_ref, nw_ref, o_ref, s_ref, *, chunk):
    @pl.when(pl.program_id(1) == 0)
    def _():
        s_ref[...] = s0_ref[...]

    p = p_ref[...]
    cos, sin = cos_ref[...], sin_ref[...]
    lane = lax.broadcasted_iota(jnp.int32, (chunk, RET_Q), 1)
    even = (lane & 1) == 0

    def rotary(x):
        partner = jnp.where(even, pltpu.roll(x, RET_Q - 1, 1), pltpu.roll(x, 1, 1))
        return x * cos + partner * sin

    q = rotary(p[:, :RET_Q])
    k = rotary(p[:, RET_Q:2 * RET_Q]) * (RET_DK ** -0.5)
    q_state = q * qs_ref[...]
    k_state = k * ks_ref[...]
    v = p[:, 2 * RET_Q:2 * RET_Q + RET_V]
    gate = p[:, 2 * RET_Q + RET_V:]
    for h in range(RET_HEADS):
        ks_ = slice(h * RET_DK, (h + 1) * RET_DK)
        vs_ = slice(h * RET_DV, (h + 1) * RET_DV)
        state = s_ref[h]
        v_h = v[:, vs_]
        intra = _mm(_mm_nt(q[:, ks_], k[:, ks_]) * dec_ref[h], v_h)
        o = _mm(q_state[:, ks_], state) + intra
        s_ref[h] = state * ss_ref[h] + _mm_tn(k_state[:, ks_], v_h)
        o = o * lax.rsqrt(jnp.mean(o * o, -1, keepdims=True) + NORM_EPS) * nw_ref[:, vs_]
        o_ref[:, vs_] = o * _silu(gate[:, vs_])


def _ret_tables(pos, chunk):
    half = RET_DK // 2
    inv = 1.0 / (RET_ROPE_BASE ** jnp.linspace(0.0, 1.0, half, dtype=F32))
    ang = pos.astype(F32)[:, None] * inv[None, :]
    cos = jnp.repeat(jnp.cos(ang), 2, axis=-1)
    sin = jnp.stack([-jnp.sin(ang), jnp.sin(ang)], -1).reshape(ang.shape[0], RET_DK)
    cos, sin = jnp.tile(cos, (1, RET_HEADS)), jnp.tile(sin, (1, RET_HEADS))
    lg = jnp.log1p(-jnp.exp2(-jnp.linspace(5.0, 12.0, RET_HEADS, dtype=F32)))
    ci = jnp.arange(chunk, dtype=F32)
    diff = ci[:, None] - ci[None, :]
    causal = diff >= 0
    decay = jnp.where(causal, jnp.exp(jnp.where(causal, diff, 0.0) * lg[:, None, None]), 0.0)
    q_scale = jnp.exp((ci + 1.0) * lg[:, None])
    k_scale = jnp.exp((chunk - 1.0 - ci) * lg[:, None])
    s_scale = jnp.exp(chunk * lg)
    widen = lambda t: jnp.repeat(t.T, RET_DK, axis=1)
    s_scale = jnp.broadcast_to(s_scale[:, None, None], (RET_HEADS, 1, RET_DV))
    return cos, sin, decay, widen(q_scale), widen(k_scale), s_scale


def _retention(p_flat, pos, s0, norm_w, batch, seq, chunk):
    n_chunks = seq // chunk
    cos, sin, decay, q_scale, k_scale, s_scale = _ret_tables(pos, chunk)
    const = lambda shape: pl.BlockSpec(shape, lambda b, c: (0,) * len(shape))
    state_spec = pl.BlockSpec((None, RET_HEADS, RET_DK, RET_DV), lambda b, c: (b, 0, 0, 0))
    return pl.pallas_call(
        functools.partial(_ret_kernel, chunk=chunk),
        grid=(batch, n_chunks),
        in_specs=[
            pl.BlockSpec((chunk, RET_W), lambda b, c: (b * n_chunks + c, 0)),
            pl.BlockSpec((chunk, RET_Q), lambda b, c: (c, 0)),
            pl.BlockSpec((chunk, RET_Q), lambda b, c: (c, 0)),
            const(decay.shape), const(q_scale.shape), const(k_scale.shape), const(s_scale.shape),
            state_spec, const(norm_w.shape),
        ],
        out_specs=[pl.BlockSpec((chunk, RET_V), lambda b, c: (b * n_chunks + c, 0)), state_spec],
        out_shape=[jax.ShapeDtypeStruct((batch * seq, RET_V), F32), jax.ShapeDtypeStruct(s0.shape, F32)],
        compiler_params=pltpu.CompilerParams(dimension_semantics=("parallel", "arbitrary"),
                                             vmem_limit_bytes=VMEM_LIMIT),
        name="retention",
    )(p_flat, cos, sin, decay, q_scale, k_scale, s_scale, s0, norm_w)


def _rwkv_kernel(p_ref, s0_ref, sh0_ref, mu_ref, w0_ref, w2_ref, a0_ref, a2_ref, kk_ref, ka_ref, rk_ref,
                 lnw_ref, lnb_ref, o_ref, s_ref, shout_ref, *, chunk):
    @pl.when(pl.program_id(1) == 0)
    def _():
        s_ref[...] = s0_ref[...]
        shout_ref[...] = sh0_ref[...]

    p = p_ref[...]
    sh = p[:, :SHIFT_W]
    gate = p[:, SHIFT_W:]
    row = lax.broadcasted_iota(jnp.int32, sh.shape, 0)
    prev = jnp.where(row == 0, shout_ref[...], pltpu.roll(sh, 1, 0))
    shout_ref[...] = sh[chunk - 1:chunk, :]
    xs = sh + (prev - sh) * mu_ref[...]
    r = xs[:, :RWKV_W]
    k_in = xs[:, RWKV_W:2 * RWKV_W]
    v = xs[:, 2 * RWKV_W:3 * RWKV_W]
    wd = xs[:, 3 * RWKV_W:3 * RWKV_W + RWKV_LORA]
    ad = xs[:, 3 * RWKV_W + RWKV_LORA:]
    w = -_softplus(-(w0_ref[...] + _mm(jnp.tanh(wd), w2_ref[...]))) - 0.5
    log_decay = -jnp.exp(w)
    a = jax.nn.sigmoid(a0_ref[...] + _mm(ad, a2_ref[...]))
    kk_raw = k_in * kk_ref[...]
    k = k_in * (1.0 + (a - 1.0) * ka_ref[...])
    bonus_rk = r * k * rk_ref[...]

    ri, ci = _square_masks(chunk)
    tril = ri >= ci
    strict = ri > ci
    cum = _mm(tril.astype(F32), log_decay)
    for h in range(RWKV_HEADS):
        hs = slice(h * RWKV_N, (h + 1) * RWKV_N)
        r_h, k_h, v_h, a_h = r[:, hs], k[:, hs], v[:, hs], a[:, hs]
        kk_h = kk_raw[:, hs]
        kk_h = kk_h * lax.rsqrt(jnp.sum(kk_h * kk_h, -1, keepdims=True) + L2_EPS)
        b_h = kk_h * a_h
        cum_h = cum[:, hs]
        cum_last = cum_h[chunk - 1:chunk, :]
        grow = jnp.exp(-cum_h)
        tail = jnp.exp(cum_last - cum_h)
        lhs = jnp.concatenate([kk_h * jnp.exp(cum_h - log_decay[:, hs]), r_h * jnp.exp(cum_h)], axis=0)
        state = s_ref[h]
        from_state = _mm_nt(lhs, state)
        g_k = _mm_nt(lhs, k_h * grow)
        g_b = _mm_nt(lhs, b_h * grow)
        l_ck = jnp.where(strict, g_k[:chunk], 0.0)
        l_cb = jnp.where(strict, g_b[:chunk], 0.0)
        m_rk = jnp.where(tril, g_k[chunk:], 0.0)
        m_rb = jnp.where(tril, g_b[chunk:], 0.0)
        u = _mm(_unit_lower_inverse(l_cb, chunk), from_state[:chunk] + _mm(l_ck, v_h))
        y = from_state[chunk:] + _mm(m_rk, v_h) - _mm(m_rb, u)
        s_ref[h] = state * jnp.exp(cum_last) + _mm_tn(jnp.concatenate([v_h, -u], axis=0),
                                                      jnp.concatenate([k_h * tail, b_h * tail], axis=0))
        mean = jnp.mean(y, -1, keepdims=True)
        var = jnp.mean(jnp.square(y - mean), -1, keepdims=True)
        y = (y - mean) * lax.rsqrt(var + RWKV_GN_EPS) * lnw_ref[:, hs] + lnb_ref[:, hs]
        y = y + jnp.sum(bonus_rk[:, hs], -1, keepdims=True) * v_h
        o_ref[:, hs] = y * _silu(gate[:, hs])


def _rwkv(p_flat, s0, shift0, params, batch, seq, chunk):
    n_chunks = seq // chunk
    const = lambda a: pl.BlockSpec(a.shape, lambda b, c: (0,) * a.ndim)
    state_spec = pl.BlockSpec((None, RWKV_HEADS, RWKV_N, RWKV_N), lambda b, c: (b, 0, 0, 0))
    shift_spec = pl.BlockSpec((None, 1, SHIFT_W), lambda b, c: (b, 0, 0))
    shift0 = shift0.reshape(batch, 1, SHIFT_W)
    o, s, shift = pl.pallas_call(
        functools.partial(_rwkv_kernel, chunk=chunk),
        grid=(batch, n_chunks),
        in_specs=[pl.BlockSpec((chunk, RWKV_IN), lambda b, c: (b * n_chunks + c, 0)), state_spec, shift_spec]
        + [const(a) for a in params],
        out_specs=[pl.BlockSpec((chunk, RWKV_W), lambda b, c: (b * n_chunks + c, 0)), state_spec, shift_spec],
        out_shape=[jax.ShapeDtypeStruct((batch * seq, RWKV_W), F32), jax.ShapeDtypeStruct(s0.shape, F32),
                   jax.ShapeDtypeStruct(shift0.shape, F32)],
        compiler_params=pltpu.CompilerParams(dimension_semantics=("parallel", "arbitrary"),
                                             vmem_limit_bytes=VMEM_LIMIT),
        name="rwkv7",
    )(p_flat, s0, shift0, *params)
    return o, s, shift.reshape(batch, SHIFT_W)


def _gdn_kernel(qkv_ref, z_ref, ba_ref, s0_ref, c0_ref, cw_ref, alog_ref, dtb_ref, gnw_ref,
                o_ref, s_ref, cout_ref, xbuf_ref, *, chunk):
    taps = GDN_CONV - 1

    @pl.when(pl.program_id(1) == 0)
    def _():
        s_ref[...] = s0_ref[...]
        xbuf_ref[SUBLANES - taps:SUBLANES, :] = c0_ref[...]

    x = qkv_ref[...]
    xbuf_ref[SUBLANES:SUBLANES + chunk, :] = x
    conv = x * cw_ref[taps:taps + 1, :]
    for j in range(taps):
        conv = conv + xbuf_ref[SUBLANES - taps + j:SUBLANES - taps + j + chunk, :] * cw_ref[j:j + 1, :]
    carry = xbuf_ref[SUBLANES + chunk - taps:SUBLANES + chunk, :]
    cout_ref[...] = carry
    xbuf_ref[SUBLANES - taps:SUBLANES, :] = carry

    act = _silu(conv)
    z = z_ref[...]
    ba = ba_ref[...]
    beta = jax.nn.sigmoid(ba[:, :GDN_HEADS])
    g = -jnp.exp(alog_ref[...]) * _softplus(ba[:, GDN_HEADS:] + dtb_ref[...])
    ri, ci = _square_masks(chunk)
    tril = ri >= ci
    strict = ri > ci
    eye = ri == ci
    gcum = _mm(tril.astype(F32), g)
    for h in range(GDN_HEADS):
        qs_ = slice(h * GDN_DK, (h + 1) * GDN_DK)
        ks_ = slice(GDN_QK + h * GDN_DK, GDN_QK + (h + 1) * GDN_DK)
        vs_ = slice(2 * GDN_QK + h * GDN_DV, 2 * GDN_QK + (h + 1) * GDN_DV)
        os_ = slice(h * GDN_DV, (h + 1) * GDN_DV)
        q_h, k_h, v_h = act[:, qs_], act[:, ks_], act[:, vs_]
        q_h = q_h * lax.rsqrt(jnp.sum(q_h * q_h, -1, keepdims=True) + L2_EPS) * (GDN_DK ** -0.5)
        k_h = k_h * lax.rsqrt(jnp.sum(k_h * k_h, -1, keepdims=True) + L2_EPS)
        beta_h = beta[:, h:h + 1]
        gc = gcum[:, h:h + 1]
        gc_row = jnp.sum(jnp.where(eye, gc, 0.0), axis=0, keepdims=True)
        decay = jnp.where(tril, jnp.exp(jnp.where(tril, gc - gc_row, 0.0)), 0.0)
        kb = k_h * beta_h
        exp_gc = jnp.exp(gc)
        lower = jnp.where(strict, _mm_nt(kb, k_h) * decay, 0.0)
        sol = _mm(_unit_lower_inverse(lower, chunk), jnp.concatenate([v_h * beta_h, kb * exp_gc], axis=1))
        state = s_ref[h]
        v_new = sol[:, :GDN_DV] - _mm(sol[:, GDN_DV:], state)
        attn = _mm_nt(q_h, k_h) * decay
        o = _mm(q_h * exp_gc, state) + _mm(attn, v_new)
        g_last = gc[chunk - 1:chunk, :]
        s_ref[h] = state * jnp.exp(g_last) + _mm_tn(k_h * jnp.exp(g_last - gc), v_new)
        o = o * lax.rsqrt(jnp.mean(o * o, -1, keepdims=True) + NORM_EPS) * gnw_ref[...]
        o_ref[:, os_] = o * _silu(z[:, os_])


def _gdn(qkv, z, ba, s0, conv0, params, batch, seq, chunk):
    n_chunks = seq // chunk
    const = lambda a: pl.BlockSpec(a.shape, lambda b, c: (0,) * a.ndim)
    rows = lambda w: pl.BlockSpec((chunk, w), lambda b, c: (b * n_chunks + c, 0))
    state_spec = pl.BlockSpec((None, GDN_HEADS, GDN_DK, GDN_DV), lambda b, c: (b, 0, 0, 0))
    conv_spec = pl.BlockSpec((None, GDN_CONV - 1, GDN_QKV), lambda b, c: (b, 0, 0))
    return pl.pallas_call(
        functools.partial(_gdn_kernel, chunk=chunk),
        grid=(batch, n_chunks),
        in_specs=[rows(GDN_QKV), rows(GDN_VW), rows(2 * GDN_HEADS), state_spec, conv_spec]
        + [const(a) for a in params],
        out_specs=[rows(GDN_VW), state_spec, conv_spec],
        out_shape=[jax.ShapeDtypeStruct((batch * seq, GDN_VW), F32), jax.ShapeDtypeStruct(s0.shape, F32),
                   jax.ShapeDtypeStruct(conv0.shape, F32)],
        scratch_shapes=[pltpu.VMEM((chunk + SUBLANES, GDN_QKV), F32)],
        compiler_params=pltpu.CompilerParams(dimension_semantics=("parallel", "arbitrary"),
                                             vmem_limit_bytes=VMEM_LIMIT),
        name="gated_delta",
    )(qkv, z, ba, s0, conv0, *params)


def _run_group(x, pos, s_ret, s_rwkv, s_shift, s_gdn, s_conv, chunk, w):
    batch, seq, _ = x.shape
    n_tokens = batch * seq
    h0 = x.reshape(n_tokens, D_MODEL)
    p_ret, p_rwkv = _token_call(
        _even_in_kernel, n_tokens, {"x": h0}, {"nw": w["norm_e"], "w": w["w_in_e"]},
        [RET_W, RWKV_IN], ["x", "nw", "w"], "even_in")
    o_ret, ret_new = _retention(p_ret, pos, s_ret, w["ret_norm_w"], batch, seq, chunk)
    o_rwkv, rwkv_new, shift_new = _rwkv(p_rwkv, s_rwkv, s_shift, w["rwkv_params"], batch, seq, chunk)
    h1, qkv, z, ba = _token_call(
        _even_out_odd_in_kernel, n_tokens, {"h": h0, "oret": o_ret, "orwkv": o_rwkv},
        {"wout": w["w_out_e"], "nw": w["norm_o"], "win": w["w_in_o"], "wba": w["w_ba_o"]},
        [D_MODEL, GDN_QKV, GDN_VW, 2 * GDN_HEADS], ["h", "oret", "orwkv", "wout", "nw", "win", "wba"],
        "even_out_odd_in")
    o_gdn, gdn_new, conv_new = _gdn(qkv, z, ba, s_gdn, s_conv, w["gdn_params"], batch, seq, chunk)
    (y,) = _token_call(
        _odd_out_final_kernel, n_tokens, {"h": h1, "o": o_gdn}, {"wout": w["w_out_o"], "nw": w["final_norm"]},
        [D_MODEL], ["h", "o", "wout", "nw"], "odd_out_final")
    return (y.reshape(batch, seq, D_MODEL), ret_new[None], rwkv_new[None], shift_new[None], gdn_new[None],
            conv_new[None])


def kernel(x_prompt, x_sample, state_ret, state_rwkv, state_shift, state_gdn, state_conv, norm_e, w_in_e, rwkv_mu, rwkv_w0, rwkv_w2, rwkv_a0, rwkv_a2, rwkv_kk, rwkv_ka, rwkv_rk, rwkv_ln_w, rwkv_ln_b, ret_norm_w, w_out_e, norm_o, w_in_o, gdn_conv_w, gdn_a_log, gdn_dt_bias, gdn_norm_w, w_out_o, final_norm):
    assert state_ret.shape[0] == 1 and state_gdn.shape[0] == 1, "one even and one odd layer"
    row = lambda a: a.reshape(1, -1)
    n_qkvz = GDN_QKV + GDN_VW
    w = {
        "norm_e": row(norm_e[0]), "w_in_e": w_in_e[0].astype(BF16), "ret_norm_w": row(ret_norm_w[0]),
        "rwkv_params": (row(rwkv_mu[0]), row(rwkv_w0[0]), rwkv_w2[0], row(rwkv_a0[0]), rwkv_a2[0], row(rwkv_kk[0]),
                        row(rwkv_ka[0]), row(rwkv_rk[0]), row(rwkv_ln_w[0]), row(rwkv_ln_b[0])),
        "w_out_e": w_out_e[0].astype(BF16), "norm_o": row(norm_o[0]),
        "w_in_o": w_in_o[0][:, :n_qkvz].astype(BF16), "w_ba_o": w_in_o[0][:, n_qkvz:].astype(BF16),
        "gdn_params": (gdn_conv_w[0], row(gdn_a_log[0]), row(gdn_dt_bias[0]), row(gdn_norm_w[0])),
        "w_out_o": w_out_o[0].astype(BF16), "final_norm": row(final_norm),
    }
    batch, seq, _ = x_prompt.shape
    dec_batch, dec_seq, _ = x_sample.shape
    past_len = 16384
    zeros = lambda s: jnp.zeros((batch,) + s.shape[2:], F32)
    prompt = _run_group(x_prompt, jnp.arange(seq), zeros(state_ret), zeros(state_rwkv), zeros(state_shift),
                        zeros(state_gdn), zeros(state_conv), math.gcd(seq, PROMPT_CHUNK), w)
    sample = _run_group(x_sample, past_len + jnp.arange(dec_seq), state_ret[0], state_rwkv[0], state_shift[0],
                        state_gdn[0], state_conv[0], dec_seq, w)
    return (prompt[0], sample[0]) + prompt[1:] + sample[1:]
```

```python
import functools
import math

import jax
import jax.numpy as jnp
from jax import lax
from jax.experimental import pallas as pl
from jax.experimental.pallas import tpu as pltpu

F32 = jnp.float32
BF16 = jnp.bfloat16
HIGHEST = lax.Precision.HIGHEST

D_MODEL = 1024
PAST_LEN = 16384
RET_HEADS, RET_DK, RET_DV = 4, 64, 128
RET_ROPE_BASE = 10000.0
RET_Q = RET_HEADS * RET_DK
RET_V = RET_HEADS * RET_DV
RET_W = 2 * RET_Q + 2 * RET_V
RWKV_HEADS, RWKV_N = 8, 64
RWKV_W = RWKV_HEADS * RWKV_N
RWKV_LORA = 64
RWKV_GN_EPS = 64e-5
SHIFT_W = 3 * RWKV_W + 2 * RWKV_LORA
RWKV_IN = SHIFT_W + RWKV_W
GDN_HEADS, GDN_DK, GDN_DV, GDN_CONV = 8, 128, 128, 4
GDN_QK = GDN_HEADS * GDN_DK
GDN_VW = GDN_HEADS * GDN_DV
GDN_QKV = 2 * GDN_QK + GDN_VW
NORM_EPS = 1e-6
L2_EPS = 1e-12

SUBLANES = 8
INV_BLOCK = 16
VMEM_LIMIT = 56 * 1024 * 1024
PROMPT_CHUNK = 64
TOKEN_TILE = 256


def _mm(a, b, precision=HIGHEST):
    return lax.dot_general(a, b, (((1,), (0,)), ((), ())), precision=precision, preferred_element_type=F32)


def _mm_nt(a, b, precision=HIGHEST):
    return lax.dot_general(a, b, (((1,), (1,)), ((), ())), precision=precision, preferred_element_type=F32)


def _mm_tn(a, b, precision=HIGHEST):
    return lax.dot_general(a, b, (((0,), (0,)), ((), ())), precision=precision, preferred_element_type=F32)


def _silu(x):
    return x * jax.nn.sigmoid(x)


def _softplus(x):
    return jnp.maximum(x, 0.0) + jnp.log1p(jnp.exp(-jnp.abs(x)))


def _square_masks(n):
    ri = lax.broadcasted_iota(jnp.int32, (n, n), 0)
    ci = lax.broadcasted_iota(jnp.int32, (n, n), 1)
    return ri, ci


def _each(fn, *seqs):
    return [fn(*args) for args in zip(*seqs)]


def _neumann_inverses(lows, eye, n):
    invs = [eye - low for low in lows]
    powers = list(lows)
    k = 2
    while k < n:
        powers = _each(_mm, powers, powers)
        invs = _each(lambda inv, p: inv + _mm(inv, p), invs, powers)
        k *= 2
    return invs


def _unit_lower_inverses(lows, n):
    ri, ci = _square_masks(n)
    eye = (ri == ci).astype(F32)
    if n <= INV_BLOCK:
        return _neumann_inverses(lows, eye, n)
    shift = int(math.log2(INV_BLOCK))
    same_block = (ri >> shift) == (ci >> shift)
    diag_invs = _neumann_inverses([jnp.where(same_block, low, 0.0) for low in lows], eye, INV_BLOCK)
    offs = _each(lambda d, low: _mm(d, jnp.where(same_block, 0.0, low)), diag_invs, lows)
    return _each(_mm, _neumann_inverses(offs, eye, n // INV_BLOCK), diag_invs)


def _rmsnorm_rows(x, w):
    return x * lax.rsqrt(jnp.mean(x * x, -1, keepdims=True) + NORM_EPS) * w


def _even_in_kernel(x_ref, nw_ref, w_ref, ret_ref, rwkv_ref):
    xn = _rmsnorm_rows(x_ref[...], nw_ref[...]).astype(BF16)
    ret_ref[...] = jnp.dot(xn, w_ref[:, :RET_W], preferred_element_type=F32)
    rwkv_ref[...] = jnp.dot(xn, w_ref[:, RET_W:], preferred_element_type=F32)


def _even_out_odd_in_kernel(h_ref, oret_ref, orwkv_ref, wout_ref, nw_ref, win_ref, wba_ref,
                            h1_ref, qkv_ref, z_ref, ba_ref):
    mix = jnp.dot(oret_ref[...].astype(BF16), wout_ref[:RET_V, :], preferred_element_type=F32)
    mix = mix + jnp.dot(orwkv_ref[...].astype(BF16), wout_ref[RET_V:, :], preferred_element_type=F32)
    h1 = h_ref[...] + mix
    h1_ref[...] = h1
    xn = _rmsnorm_rows(h1, nw_ref[...]).astype(BF16)
    qkv_ref[...] = jnp.dot(xn, win_ref[:, :GDN_QKV], preferred_element_type=F32)
    z_ref[...] = jnp.dot(xn, win_ref[:, GDN_QKV:], preferred_element_type=F32)
    ba_ref[...] = jnp.dot(xn, wba_ref[...], preferred_element_type=F32)


def _odd_out_final_kernel(h_ref, o_ref, wout_ref, nw_ref, y_ref):
    h2 = h_ref[...] + jnp.dot(o_ref[...].astype(BF16), wout_ref[...], preferred_element_type=F32)
    y_ref[...] = _rmsnorm_rows(h2, nw_ref[...])


def _row_spec(tile, width):
    return pl.BlockSpec((tile, width), lambda i: (i, 0))


def _full_spec(shape):
    return pl.BlockSpec(shape, lambda i: (0,) * len(shape))


def _token_call(kernel, n_tokens, row_inputs, full_inputs, out_widths, order, name):
    tile = TOKEN_TILE
    assert n_tokens % tile == 0
    specs = {**{k: _row_spec(tile, v.shape[1]) for k, v in row_inputs.items()},
             **{k: _full_spec(v.shape) for k, v in full_inputs.items()}}
    arrays = {**row_inputs, **full_inputs}
    return pl.pallas_call(
        kernel,
        grid=(n_tokens // tile,),
        in_specs=[specs[k] for k in order],
        out_specs=[_row_spec(tile, w) for w in out_widths],
        out_shape=[jax.ShapeDtypeStruct((n_tokens, w), F32) for w in out_widths],
        compiler_params=pltpu.CompilerParams(dimension_semantics=("parallel",), vmem_limit_bytes=VMEM_LIMIT),
        name=name,
    )(*[arrays[k] for k in order])


def _ret_kernel(p_ref, cos_ref, sin_ref, dec_ref, qs_ref, ks_ref, ss_ref, s0_ref, nw_ref, o_ref, s_ref, *, chunk):
    @pl.when(pl.program_id(1) == 0)
    def _():
        s_ref[...] = s0_ref[...]

    p = p_ref[...]
    cos, sin = cos_ref[...], sin_ref[...]
    lane = lax.broadcasted_iota(jnp.int32, (chunk, RET_Q), 1)
    even = (lane & 1) == 0

    def rotary(x):
        partner = jnp.where(even, pltpu.roll(x, RET_Q - 1, 1), pltpu.roll(x, 1, 1))
        return x * cos + partner * sin

    q = rotary(p[:, :RET_Q])
    k = rotary(p[:, RET_Q:2 * RET_Q]) * (RET_DK ** -0.5)
    q_state = q * qs_ref[...]
    k_state = k * ks_ref[...]
    v = p[:, 2 * RET_Q:2 * RET_Q + RET_V]
    gate = p[:, 2 * RET_Q + RET_V:]
    heads = range(RET_HEADS)
    k_slices = [slice(h * RET_DK, (h + 1) * RET_DK) for h in heads]
    v_slices = [slice(h * RET_DV, (h + 1) * RET_DV) for h in heads]
    states = [s_ref[h] for h in heads]
    v_heads = [v[:, s] for s in v_slices]
    scores = [_mm_nt(q[:, s], k[:, s]) * dec_ref[h] for h, s in zip(heads, k_slices)]
    from_state = [_mm(q_state[:, s], state) for s, state in zip(k_slices, states)]
    intra = _each(_mm, scores, v_heads)
    outer = [_mm_tn(k_state[:, s], v_h) for s, v_h in zip(k_slices, v_heads)]
    for h in heads:
        s_ref[h] = states[h] * ss_ref[h] + outer[h]
        o = from_state[h] + intra[h]
        o = o * lax.rsqrt(jnp.mean(o * o, -1, keepdims=True) + NORM_EPS) * nw_ref[:, v_slices[h]]
        o_ref[:, v_slices[h]] = o * _silu(gate[:, v_slices[h]])


def _ret_tables(pos, chunk):
    half = RET_DK // 2
    inv = 1.0 / (RET_ROPE_BASE ** jnp.linspace(0.0, 1.0, half, dtype=F32))
    ang = pos.astype(F32)[:, None] * inv[None, :]
    cos = jnp.repeat(jnp.cos(ang), 2, axis=-1)
    sin = jnp.stack([-jnp.sin(ang), jnp.sin(ang)], -1).reshape(ang.shape[0], RET_DK)
    cos, sin = jnp.tile(cos, (1, RET_HEADS)), jnp.tile(sin, (1, RET_HEADS))
    lg = jnp.log1p(-jnp.exp2(-jnp.linspace(5.0, 12.0, RET_HEADS, dtype=F32)))
    ci = jnp.arange(chunk, dtype=F32)
    diff = ci[:, None] - ci[None, :]
    causal = diff >= 0
    decay = jnp.where(causal, jnp.exp(jnp.where(causal, diff, 0.0) * lg[:, None, None]), 0.0)
    q_scale = jnp.exp((ci + 1.0) * lg[:, None])
    k_scale = jnp.exp((chunk - 1.0 - ci) * lg[:, None])
    s_scale = jnp.exp(chunk * lg)
    widen = lambda t: jnp.repeat(t.T, RET_DK, axis=1)
    s_scale = jnp.broadcast_to(s_scale[:, None, None], (RET_HEADS, 1, RET_DV))
    return cos, sin, decay, widen(q_scale), widen(k_scale), s_scale


def _retention(p_flat, pos, s0, norm_w, batch, seq, chunk):
    n_chunks = seq // chunk
    cos, sin, decay, q_scale, k_scale, s_scale = _ret_tables(pos, chunk)
    const = lambda shape: pl.BlockSpec(shape, lambda b, c: (0,) * len(shape))
    state_spec = pl.BlockSpec((None, RET_HEADS, RET_DK, RET_DV), lambda b, c: (b, 0, 0, 0))
    return pl.pallas_call(
        functools.partial(_ret_kernel, chunk=chunk),
        grid=(batch, n_chunks),
        in_specs=[
            pl.BlockSpec((chunk, RET_W), lambda b, c: (b * n_chunks + c, 0)),
            pl.BlockSpec((chunk, RET_Q), lambda b, c: (c, 0)),
            pl.BlockSpec((chunk, RET_Q), lambda b, c: (c, 0)),
            const(decay.shape), const(q_scale.shape), const(k_scale.shape), const(s_scale.shape),
            state_spec, const(norm_w.shape),
        ],
        out_specs=[pl.BlockSpec((chunk, RET_V), lambda b, c: (b * n_chunks + c, 0)), state_spec],
        out_shape=[jax.ShapeDtypeStruct((batch * seq, RET_V), F32), jax.ShapeDtypeStruct(s0.shape, F32)],
        compiler_params=pltpu.CompilerParams(dimension_semantics=("parallel", "arbitrary"),
                                             vmem_limit_bytes=VMEM_LIMIT),
        name="retention",
    )(p_flat, cos, sin, decay, q_scale, k_scale, s_scale, s0, norm_w)


def _rwkv_kernel(p_ref, s0_ref, sh0_ref, mu_ref, w0_ref, w2_ref, a0_ref, a2_ref, kk_ref, ka_ref, rk_ref,
                 lnw_ref, lnb_ref, o_ref, s_ref, shout_ref, *, chunk):
    @pl.when(pl.program_id(1) == 0)
    def _():
        s_ref[...] = s0_ref[...]
        shout_ref[...] = sh0_ref[...]

    p = p_ref[...]
    sh = p[:, :SHIFT_W]
    gate = p[:, SHIFT_W:]
    row = lax.broadcasted_iota(jnp.int32, sh.shape, 0)
    prev = jnp.where(row == 0, shout_ref[...], pltpu.roll(sh, 1, 0))
    shout_ref[...] = sh[chunk - 1:chunk, :]
    xs = sh + (prev - sh) * mu_ref[...]
    r = xs[:, :RWKV_W]
    k_in = xs[:, RWKV_W:2 * RWKV_W]
    v = xs[:, 2 * RWKV_W:3 * RWKV_W]
    wd = xs[:, 3 * RWKV_W:3 * RWKV_W + RWKV_LORA]
    ad = xs[:, 3 * RWKV_W + RWKV_LORA:]
    w = -_softplus(-(w0_ref[...] + _mm(jnp.tanh(wd), w2_ref[...]))) - 0.5
    log_decay = -jnp.exp(w)
    a = jax.nn.sigmoid(a0_ref[...] + _mm(ad, a2_ref[...]))
    kk_raw = k_in * kk_ref[...]
    k = k_in * (1.0 + (a - 1.0) * ka_ref[...])
    bonus_rk = r * k * rk_ref[...]

    ri, ci = _square_masks(chunk)
    tril = ri >= ci
    strict = ri > ci
    cum = _mm(tril.astype(F32), log_decay)
    heads = range(RWKV_HEADS)
    slices = [slice(h * RWKV_N, (h + 1) * RWKV_N) for h in heads]
    per_head = lambda t: [t[:, s] for s in slices]
    r_h, k_h, v_h, cum_h = per_head(r), per_head(k), per_head(v), per_head(cum)
    kk_h = [x * lax.rsqrt(jnp.sum(x * x, -1, keepdims=True) + L2_EPS) for x in per_head(kk_raw)]
    b_h = _each(jnp.multiply, kk_h, per_head(a))
    cum_last = [c[chunk - 1:chunk, :] for c in cum_h]
    grow = [jnp.exp(-c) for c in cum_h]
    tail = _each(lambda cl, c: jnp.exp(cl - c), cum_last, cum_h)
    lhs = _each(lambda kk, r_, c, ld: jnp.concatenate([kk * jnp.exp(c - ld), r_ * jnp.exp(c)], axis=0),
                kk_h, r_h, cum_h, per_head(log_decay))
    states = [s_ref[h] for h in heads]
    g_b = _each(lambda x, b, g: _mm_nt(x, b * g), lhs, b_h, grow)
    g_k = _each(lambda x, k_, g: _mm_nt(x, k_ * g), lhs, k_h, grow)
    from_state = _each(_mm_nt, lhs, states)
    inverses = _unit_lower_inverses([jnp.where(strict, g[:chunk], 0.0) for g in g_b], chunk)
    rhs = _each(lambda fs, g, v_: fs[:chunk] + _mm(jnp.where(strict, g[:chunk], 0.0), v_), from_state, g_k, v_h)
    u = _each(_mm, inverses, rhs)
    y_v = _each(lambda g, v_: _mm(jnp.where(tril, g[chunk:], 0.0), v_), g_k, v_h)
    y_u = _each(lambda g, u_: _mm(jnp.where(tril, g[chunk:], 0.0), u_), g_b, u)
    outer = _each(lambda v_, u_, k_, b, t: _mm_tn(jnp.concatenate([v_, -u_], axis=0),
                                                  jnp.concatenate([k_ * t, b * t], axis=0)),
                  v_h, u, k_h, b_h, tail)
    for h in heads:
        hs = slices[h]
        s_ref[h] = states[h] * jnp.exp(cum_last[h]) + outer[h]
        y = from_state[h][chunk:] + y_v[h] - y_u[h]
        mean = jnp.mean(y, -1, keepdims=True)
        var = jnp.mean(jnp.square(y - mean), -1, keepdims=True)
        y = (y - mean) * lax.rsqrt(var + RWKV_GN_EPS) * lnw_ref[:, hs] + lnb_ref[:, hs]
        y = y + jnp.sum(bonus_rk[:, hs], -1, keepdims=True) * v_h[h]
        o_ref[:, hs] = y * _silu(gate[:, hs])


def _rwkv(p_flat, s0, shift0, params, batch, seq, chunk):
    n_chunks = seq // chunk
    const = lambda a: pl.BlockSpec(a.shape, lambda b, c: (0,) * a.ndim)
    state_spec = pl.BlockSpec((None, RWKV_HEADS, RWKV_N, RWKV_N), lambda b, c: (b, 0, 0, 0))
    shift_spec = pl.BlockSpec((None, 1, SHIFT_W), lambda b, c: (b, 0, 0))
    shift0 = shift0.reshape(batch, 1, SHIFT_W)
    o, s, shift = pl.pallas_call(
        functools.partial(_rwkv_kernel, chunk=chunk),
        grid=(batch, n_chunks),
        in_specs=[pl.BlockSpec((chunk, RWKV_IN), lambda b, c: (b * n_chunks + c, 0)), state_spec, shift_spec]
        + [const(a) for a in params],
        out_specs=[pl.BlockSpec((chunk, RWKV_W), lambda b, c: (b * n_chunks + c, 0)), state_spec, shift_spec],
        out_shape=[jax.ShapeDtypeStruct((batch * seq, RWKV_W), F32), jax.ShapeDtypeStruct(s0.shape, F32),
                   jax.ShapeDtypeStruct(shift0.shape, F32)],
        compiler_params=pltpu.CompilerParams(dimension_semantics=("parallel", "arbitrary"),
                                             vmem_limit_bytes=VMEM_LIMIT),
        name="rwkv7",
    )(p_flat, s0, shift0, *params)
    return o, s, shift.reshape(batch, SHIFT_W)


def _gdn_kernel(qkv_ref, z_ref, ba_ref, s0_ref, c0_ref, cw_ref, alog_ref, dtb_ref, gnw_ref,
                o_ref, s_ref, cout_ref, xbuf_ref, *, chunk):
    taps = GDN_CONV - 1

    @pl.when(pl.program_id(1) == 0)
    def _():
        s_ref[...] = s0_ref[...]
        xbuf_ref[SUBLANES - taps:SUBLANES, :] = c0_ref[...]

    x = qkv_ref[...]
    xbuf_ref[SUBLANES:SUBLANES + chunk, :] = x
    conv = x * cw_ref[taps:taps + 1, :]
    for j in range(taps):
        conv = conv + xbuf_ref[SUBLANES - taps + j:SUBLANES - taps + j + chunk, :] * cw_ref[j:j + 1, :]
    carry = xbuf_ref[SUBLANES + chunk - taps:SUBLANES + chunk, :]
    cout_ref[...] = carry
    xbuf_ref[SUBLANES - taps:SUBLANES, :] = carry

    act = _silu(conv)
    z = z_ref[...]
    ba = ba_ref[...]
    beta = jax.nn.sigmoid(ba[:, :GDN_HEADS])
    g = -jnp.exp(alog_ref[...]) * _softplus(ba[:, GDN_HEADS:] + dtb_ref[...])
    ri, ci = _square_masks(chunk)
    tril = ri >= ci
    strict = ri > ci
    eye = ri == ci
    gcum = _mm(tril.astype(F32), g)
    heads = range(GDN_HEADS)
    head_slice = lambda base, h: slice(base + h * GDN_DK, base + (h + 1) * GDN_DK)
    l2 = lambda x: x * lax.rsqrt(jnp.sum(x * x, -1, keepdims=True) + L2_EPS)
    q_h = [l2(act[:, head_slice(0, h)]) * (GDN_DK ** -0.5) for h in heads]
    k_h = [l2(act[:, head_slice(GDN_QK, h)]) for h in heads]
    v_h = [act[:, head_slice(2 * GDN_QK, h)] for h in heads]
    beta_h = [beta[:, h:h + 1] for h in heads]
    gc = [gcum[:, h:h + 1] for h in heads]
    gc_row = [jnp.sum(jnp.where(eye, g_, 0.0), axis=0, keepdims=True) for g_ in gc]
    decay = _each(lambda c, r: jnp.where(tril, jnp.exp(jnp.where(tril, c - r, 0.0)), 0.0), gc, gc_row)
    kb = _each(jnp.multiply, k_h, beta_h)
    exp_gc = [jnp.exp(g_) for g_ in gc]
    g_last = [g_[chunk - 1:chunk, :] for g_ in gc]
    states = [s_ref[h] for h in heads]
    lower = _each(lambda kb_, k_, d: jnp.where(strict, _mm_nt(kb_, k_) * d, 0.0), kb, k_h, decay)
    attn = _each(lambda q_, k_, d: _mm_nt(q_, k_) * d, q_h, k_h, decay)
    o_state = _each(lambda q_, e, s: _mm(q_ * e, s), q_h, exp_gc, states)
    inverses = _unit_lower_inverses(lower, chunk)
    sol = _each(lambda inv, v_, b, kb_, e: _mm(inv, jnp.concatenate([v_ * b, kb_ * e], axis=1)),
                inverses, v_h, beta_h, kb, exp_gc)
    v_new = _each(lambda s_, state: s_[:, :GDN_DV] - _mm(s_[:, GDN_DV:], state), sol, states)
    o_intra = _each(_mm, attn, v_new)
    outer = _each(lambda k_, gl, g_, vn: _mm_tn(k_ * jnp.exp(gl - g_), vn), k_h, g_last, gc, v_new)
    for h in heads:
        os_ = head_slice(0, h)
        s_ref[h] = states[h] * jnp.exp(g_last[h]) + outer[h]
        o = o_state[h] + o_intra[h]
        o = o * lax.rsqrt(jnp.mean(o * o, -1, keepdims=True) + NORM_EPS) * gnw_ref[...]
        o_ref[:, os_] = o * _silu(z[:, os_])


def _gdn(qkv, z, ba, s0, conv0, params, batch, seq, chunk):
    n_chunks = seq // chunk
    const = lambda a: pl.BlockSpec(a.shape, lambda b, c: (0,) * a.ndim)
    rows = lambda w: pl.BlockSpec((chunk, w), lambda b, c: (b * n_chunks + c, 0))
    state_spec = pl.BlockSpec((None, GDN_HEADS, GDN_DK, GDN_DV), lambda b, c: (b, 0, 0, 0))
    conv_spec = pl.BlockSpec((None, GDN_CONV - 1, GDN_QKV), lambda b, c: (b, 0, 0))
    return pl.pallas_call(
        functools.partial(_gdn_kernel, chunk=chunk),
        grid=(batch, n_chunks),
        in_specs=[rows(GDN_QKV), rows(GDN_VW), rows(2 * GDN_HEADS), state_spec, conv_spec]
        + [const(a) for a in params],
        out_specs=[rows(GDN_VW), state_spec, conv_spec],
        out_shape=[jax.ShapeDtypeStruct((batch * seq, GDN_VW), F32), jax.ShapeDtypeStruct(s0.shape, F32),
                   jax.ShapeDtypeStruct(conv0.shape, F32)],
        scratch_shapes=[pltpu.VMEM((chunk + SUBLANES, GDN_QKV), F32)],
        compiler_params=pltpu.CompilerParams(dimension_semantics=("parallel", "arbitrary"),
                                             vmem_limit_bytes=VMEM_LIMIT),
        name="gated_delta",
    )(qkv, z, ba, s0, conv0, *params)


def _run_group(x, pos, s_ret, s_rwkv, s_shift, s_gdn, s_conv, chunk, w):
    batch, seq, _ = x.shape
    n_tokens = batch * seq
    h0 = x.reshape(n_tokens, D_MODEL)
    p_ret, p_rwkv = _token_call(
        _even_in_kernel, n_tokens, {"x": h0}, {"nw": w["norm_e"], "w": w["w_in_e"]},
        [RET_W, RWKV_IN], ["x", "nw", "w"], "even_in")
    o_ret, ret_new = _retention(p_ret, pos, s_ret, w["ret_norm_w"], batch, seq, chunk)
    o_rwkv, rwkv_new, shift_new = _rwkv(p_rwkv, s_rwkv, s_shift, w["rwkv_params"], batch, seq, chunk)
    h1, qkv, z, ba = _token_call(
        _even_out_odd_in_kernel, n_tokens, {"h": h0, "oret": o_ret, "orwkv": o_rwkv},
        {"wout": w["w_out_e"], "nw": w["norm_o"], "win": w["w_in_o"], "wba": w["w_ba_o"]},
        [D_MODEL, GDN_QKV, GDN_VW, 2 * GDN_HEADS], ["h", "oret", "orwkv", "wout", "nw", "win", "wba"],
        "even_out_odd_in")
    o_gdn, gdn_new, conv_new = _gdn(qkv, z, ba, s_gdn, s_conv, w["gdn_params"], batch, seq, chunk)
    (y,) = _token_call(
        _odd_out_final_kernel, n_tokens, {"h": h1, "o": o_gdn}, {"wout": w["w_out_o"], "nw": w["final_norm"]},
        [D_MODEL], ["h", "o", "wout", "nw"], "odd_out_final")
    return (y.reshape(batch, seq, D_MODEL), ret_new[None], rwkv_new[None], shift_new[None], gdn_new[None],
            conv_new[None])


def kernel(x_prompt, x_sample, state_ret, state_rwkv, state_shift, state_gdn, state_conv, norm_e, w_in_e, rwkv_mu, rwkv_w0, rwkv_w2, rwkv_a0, rwkv_a2, rwkv_kk, rwkv_ka, rwkv_rk, rwkv_ln_w, rwkv_ln_b, ret_norm_w, w_out_e, norm_o, w_in_o, gdn_conv_w, gdn_a_log, gdn_dt_bias, gdn_norm_w, w_out_o, final_norm):
    assert state_ret.shape[0] == 1 and state_gdn.shape[0] == 1, "one even and one odd layer"
    row = lambda a: a.reshape(1, -1)
    n_qkvz = GDN_QKV + GDN_VW
    w = {
        "norm_e": row(norm_e[0]), "w_in_e": w_in_e[0].astype(BF16), "ret_norm_w": row(ret_norm_w[0]),
        "rwkv_params": (row(rwkv_mu[0]), row(rwkv_w0[0]), rwkv_w2[0], row(rwkv_a0[0]), rwkv_a2[0], row(rwkv_kk[0]),
                        row(rwkv_ka[0]), row(rwkv_rk[0]), row(rwkv_ln_w[0]), row(rwkv_ln_b[0])),
        "w_out_e": w_out_e[0].astype(BF16), "norm_o": row(norm_o[0]),
        "w_in_o": w_in_o[0][:, :n_qkvz].astype(BF16), "w_ba_o": w_in_o[0][:, n_qkvz:].astype(BF16),
        "gdn_params": (gdn_conv_w[0], row(gdn_a_log[0]), row(gdn_dt_bias[0]), row(gdn_norm_w[0])),
        "w_out_o": w_out_o[0].astype(BF16), "final_norm": row(final_norm),
    }
    batch, seq, _ = x_prompt.shape
    dec_batch, dec_seq, _ = x_sample.shape
    zeros = lambda s: jnp.zeros((batch,) + s.shape[2:], F32)
    prompt = _run_group(x_prompt, jnp.arange(seq), zeros(state_ret), zeros(state_rwkv), zeros(state_shift),
                        zeros(state_gdn), zeros(state_conv), math.gcd(seq, PROMPT_CHUNK), w)
    sample = _run_group(x_sample, PAST_LEN + jnp.arange(dec_seq), state_ret[0], state_rwkv[0], state_shift[0],
                        state_gdn[0], state_conv[0], dec_seq, w)
    return (prompt[0], sample[0]) + prompt[1:] + sample[1:]
```

```python
import functools
import math

import jax
import jax.numpy as jnp
from jax import lax
from jax.experimental import pallas as pl
from jax.experimental.pallas import tpu as pltpu

F32 = jnp.float32
BF16 = jnp.bfloat16

D_MODEL = 1024
PAST_LEN = 16384
RET_HEADS, RET_DK, RET_DV = 4, 64, 128
RET_ROPE_BASE = 10000.0
RET_Q = RET_HEADS * RET_DK
RET_V = RET_HEADS * RET_DV
RET_W = 2 * RET_Q + 2 * RET_V
RWKV_HEADS, RWKV_N = 8, 64
RWKV_W = RWKV_HEADS * RWKV_N
RWKV_LORA = 64
RWKV_GN_EPS = 64e-5
SHIFT_W = 3 * RWKV_W + 2 * RWKV_LORA
RWKV_IN = SHIFT_W + RWKV_W
GDN_HEADS, GDN_DK, GDN_DV, GDN_CONV = 8, 128, 128, 4
GDN_QK = GDN_HEADS * GDN_DK
GDN_VW = GDN_HEADS * GDN_DV
GDN_QKV = 2 * GDN_QK + GDN_VW
NORM_EPS = 1e-6
L2_EPS = 1e-12

SUBLANES = 8
INV_BLOCK = 16
VMEM_LIMIT = 56 * 1024 * 1024
PROMPT_CHUNK = 64
TOKEN_TILE = 256


def _split_bf16(x):
    hi = x.astype(BF16)
    return hi, (x - hi.astype(F32)).astype(BF16)


def _dot(a, b, dims, mode):
    dot = lambda x, y: lax.dot_general(x, y, (dims, ((), ())), preferred_element_type=F32)
    if mode == "bf16":
        return dot(a.astype(BF16), b.astype(BF16))
    assert mode == "bf16x3", mode
    a_hi, a_lo = _split_bf16(a)
    b_hi, b_lo = _split_bf16(b)
    return (dot(a_lo, b_hi) + dot(a_hi, b_lo)) + dot(a_hi, b_hi)


def _mm(a, b, mode="bf16"):
    return _dot(a, b, ((1,), (0,)), mode)


def _mm_nt(a, b, mode="bf16"):
    return _dot(a, b, ((1,), (1,)), mode)


def _mm_tn(a, b, mode="bf16"):
    return _dot(a, b, ((0,), (0,)), mode)


def _mm3(a, b):
    return _mm(a, b, "bf16x3")


def _chunk_cumsum(tril, x):
    ones = tril.astype(BF16)
    hi = x.astype(BF16)
    rest = x - hi.astype(F32)
    mid = rest.astype(BF16)
    lo = (rest - mid.astype(F32)).astype(BF16)
    dot = lambda p: jnp.dot(ones, p, preferred_element_type=F32)
    return (dot(lo) + dot(mid)) + dot(hi)


def _silu(x):
    return x * jax.nn.sigmoid(x)


def _softplus(x):
    return jnp.maximum(x, 0.0) + jnp.log1p(jnp.exp(-jnp.abs(x)))


def _square_masks(n):
    ri = lax.broadcasted_iota(jnp.int32, (n, n), 0)
    ci = lax.broadcasted_iota(jnp.int32, (n, n), 1)
    return ri, ci


def _each(fn, *seqs):
    return [fn(*args) for args in zip(*seqs)]


def _neumann_inverses(lows, eye, n):
    invs = [eye - low for low in lows]
    powers = list(lows)
    k = 2
    while k < n:
        powers = _each(_mm3, powers, powers)
        invs = _each(lambda inv, p: inv + _mm3(inv, p), invs, powers)
        k *= 2
    return invs


def _unit_lower_inverses(lows, n):
    ri, ci = _square_masks(n)
    eye = (ri == ci).astype(F32)
    if n <= INV_BLOCK:
        return _neumann_inverses(lows, eye, n)
    shift = int(math.log2(INV_BLOCK))
    same_block = (ri >> shift) == (ci >> shift)
    diag_invs = _neumann_inverses([jnp.where(same_block, low, 0.0) for low in lows], eye, INV_BLOCK)
    offs = _each(lambda d, low: _mm3(d, jnp.where(same_block, 0.0, low)), diag_invs, lows)
    return _each(_mm3, _neumann_inverses(offs, eye, n // INV_BLOCK), diag_invs)


def _rmsnorm_rows(x, w):
    return x * lax.rsqrt(jnp.mean(x * x, -1, keepdims=True) + NORM_EPS) * w


def _even_in_kernel(x_ref, nw_ref, w_ref, ret_ref, rwkv_ref):
    xn = _rmsnorm_rows(x_ref[...], nw_ref[...]).astype(BF16)
    ret_ref[...] = jnp.dot(xn, w_ref[:, :RET_W], preferred_element_type=F32)
    rwkv_ref[...] = jnp.dot(xn, w_ref[:, RET_W:], preferred_element_type=F32)


def _even_out_odd_in_kernel(h_ref, oret_ref, orwkv_ref, wout_ref, nw_ref, win_ref, wba_ref,
                            h1_ref, qkv_ref, z_ref, ba_ref):
    mix = jnp.dot(oret_ref[...].astype(BF16), wout_ref[:RET_V, :], preferred_element_type=F32)
    mix = mix + jnp.dot(orwkv_ref[...].astype(BF16), wout_ref[RET_V:, :], preferred_element_type=F32)
    h1 = h_ref[...] + mix
    h1_ref[...] = h1
    xn = _rmsnorm_rows(h1, nw_ref[...]).astype(BF16)
    qkv_ref[...] = jnp.dot(xn, win_ref[:, :GDN_QKV], preferred_element_type=F32)
    z_ref[...] = jnp.dot(xn, win_ref[:, GDN_QKV:], preferred_element_type=F32)
    ba_ref[...] = jnp.dot(xn, wba_ref[...], preferred_element_type=F32)


def _odd_out_final_kernel(h_ref, o_ref, wout_ref, nw_ref, y_ref):
    h2 = h_ref[...] + jnp.dot(o_ref[...].astype(BF16), wout_ref[...], preferred_element_type=F32)
    y_ref[...] = _rmsnorm_rows(h2, nw_ref[...])


def _row_spec(tile, width):
    return pl.BlockSpec((tile, width), lambda i: (i, 0))


def _full_spec(shape):
    return pl.BlockSpec(shape, lambda i: (0,) * len(shape))


def _token_call(kernel, n_tokens, row_inputs, full_inputs, out_widths, order, name):
    tile = TOKEN_TILE
    assert n_tokens % tile == 0
    specs = {**{k: _row_spec(tile, v.shape[1]) for k, v in row_inputs.items()},
             **{k: _full_spec(v.shape) for k, v in full_inputs.items()}}
    arrays = {**row_inputs, **full_inputs}
    return pl.pallas_call(
        kernel,
        grid=(n_tokens // tile,),
        in_specs=[specs[k] for k in order],
        out_specs=[_row_spec(tile, w) for w in out_widths],
        out_shape=[jax.ShapeDtypeStruct((n_tokens, w), F32) for w in out_widths],
        compiler_params=pltpu.CompilerParams(dimension_semantics=("parallel",), vmem_limit_bytes=VMEM_LIMIT),
        name=name,
    )(*[arrays[k] for k in order])


def _ret_kernel(p_ref, cos_ref, sin_ref, dec_ref, qs_ref, ks_ref, ss_ref, s0_ref, nw_ref, o_ref, s_ref, *, chunk):
    @pl.when(pl.program_id(1) == 0)
    def _():
        s_ref[...] = s0_ref[...]

    p = p_ref[...]
    cos, sin = cos_ref[...], sin_ref[...]
    lane = lax.broadcasted_iota(jnp.int32, (chunk, RET_Q), 1)
    even = (lane & 1) == 0

    def rotary(x):
        partner = jnp.where(even, pltpu.roll(x, RET_Q - 1, 1), pltpu.roll(x, 1, 1))
        return x * cos + partner * sin

    q = rotary(p[:, :RET_Q])
    k = rotary(p[:, RET_Q:2 * RET_Q]) * (RET_DK ** -0.5)
    q_state = q * qs_ref[...]
    k_state = k * ks_ref[...]
    v = p[:, 2 * RET_Q:2 * RET_Q + RET_V]
    gate = p[:, 2 * RET_Q + RET_V:]
    heads = range(RET_HEADS)
    k_slices = [slice(h * RET_DK, (h + 1) * RET_DK) for h in heads]
    v_slices = [slice(h * RET_DV, (h + 1) * RET_DV) for h in heads]
    states = [s_ref[h] for h in heads]
    v_heads = [v[:, s] for s in v_slices]
    scores = [_mm_nt(q[:, s], k[:, s]) * dec_ref[h] for h, s in zip(heads, k_slices)]
    from_state = [_mm(q_state[:, s], state) for s, state in zip(k_slices, states)]
    intra = _each(_mm, scores, v_heads)
    outer = [_mm_tn(k_state[:, s], v_h) for s, v_h in zip(k_slices, v_heads)]
    for h in heads:
        s_ref[h] = states[h] * ss_ref[h] + outer[h]
        o = from_state[h] + intra[h]
        o = o * lax.rsqrt(jnp.mean(o * o, -1, keepdims=True) + NORM_EPS) * nw_ref[:, v_slices[h]]
        o_ref[:, v_slices[h]] = o * _silu(gate[:, v_slices[h]])


def _ret_tables(pos, chunk):
    half = RET_DK // 2
    inv = 1.0 / (RET_ROPE_BASE ** jnp.linspace(0.0, 1.0, half, dtype=F32))
    ang = pos.astype(F32)[:, None] * inv[None, :]
    cos = jnp.repeat(jnp.cos(ang), 2, axis=-1)
    sin = jnp.stack([-jnp.sin(ang), jnp.sin(ang)], -1).reshape(ang.shape[0], RET_DK)
    cos, sin = jnp.tile(cos, (1, RET_HEADS)), jnp.tile(sin, (1, RET_HEADS))
    lg = jnp.log1p(-jnp.exp2(-jnp.linspace(5.0, 12.0, RET_HEADS, dtype=F32)))
    ci = jnp.arange(chunk, dtype=F32)
    diff = ci[:, None] - ci[None, :]
    causal = diff >= 0
    decay = jnp.where(causal, jnp.exp(jnp.where(causal, diff, 0.0) * lg[:, None, None]), 0.0)
    q_scale = jnp.exp((ci + 1.0) * lg[:, None])
    k_scale = jnp.exp((chunk - 1.0 - ci) * lg[:, None])
    s_scale = jnp.exp(chunk * lg)
    widen = lambda t: jnp.repeat(t.T, RET_DK, axis=1)
    s_scale = jnp.broadcast_to(s_scale[:, None, None], (RET_HEADS, 1, RET_DV))
    return cos, sin, decay, widen(q_scale), widen(k_scale), s_scale


def _retention(p_flat, pos, s0, norm_w, batch, seq, chunk):
    n_chunks = seq // chunk
    cos, sin, decay, q_scale, k_scale, s_scale = _ret_tables(pos, chunk)
    const = lambda shape: pl.BlockSpec(shape, lambda b, c: (0,) * len(shape))
    state_spec = pl.BlockSpec((None, RET_HEADS, RET_DK, RET_DV), lambda b, c: (b, 0, 0, 0))
    return pl.pallas_call(
        functools.partial(_ret_kernel, chunk=chunk),
        grid=(batch, n_chunks),
        in_specs=[
            pl.BlockSpec((chunk, RET_W), lambda b, c: (b * n_chunks + c, 0)),
            pl.BlockSpec((chunk, RET_Q), lambda b, c: (c, 0)),
            pl.BlockSpec((chunk, RET_Q), lambda b, c: (c, 0)),
            const(decay.shape), const(q_scale.shape), const(k_scale.shape), const(s_scale.shape),
            state_spec, const(norm_w.shape),
        ],
        out_specs=[pl.BlockSpec((chunk, RET_V), lambda b, c: (b * n_chunks + c, 0)), state_spec],
        out_shape=[jax.ShapeDtypeStruct((batch * seq, RET_V), F32), jax.ShapeDtypeStruct(s0.shape, F32)],
        compiler_params=pltpu.CompilerParams(dimension_semantics=("parallel", "arbitrary"),
                                             vmem_limit_bytes=VMEM_LIMIT),
        name="retention",
    )(p_flat, cos, sin, decay, q_scale, k_scale, s_scale, s0, norm_w)


def _rwkv_kernel(p_ref, s0_ref, sh0_ref, mu_ref, w0_ref, w2_ref, a0_ref, a2_ref, kk_ref, ka_ref, rk_ref,
                 lnw_ref, lnb_ref, o_ref, s_ref, shout_ref, *, chunk):
    @pl.when(pl.program_id(1) == 0)
    def _():
        s_ref[...] = s0_ref[...]
        shout_ref[...] = sh0_ref[...]

    p = p_ref[...]
    sh = p[:, :SHIFT_W]
    gate = p[:, SHIFT_W:]
    row = lax.broadcasted_iota(jnp.int32, sh.shape, 0)
    prev = jnp.where(row == 0, shout_ref[...], pltpu.roll(sh, 1, 0))
    shout_ref[...] = sh[chunk - 1:chunk, :]
    xs = sh + (prev - sh) * mu_ref[...]
    r = xs[:, :RWKV_W]
    k_in = xs[:, RWKV_W:2 * RWKV_W]
    v = xs[:, 2 * RWKV_W:3 * RWKV_W]
    wd = xs[:, 3 * RWKV_W:3 * RWKV_W + RWKV_LORA]
    ad = xs[:, 3 * RWKV_W + RWKV_LORA:]
    w = -_softplus(-(w0_ref[...] + _mm3(jnp.tanh(wd), w2_ref[...]))) - 0.5
    log_decay = -jnp.exp(w)
    a = jax.nn.sigmoid(a0_ref[...] + _mm3(ad, a2_ref[...]))
    kk_raw = k_in * kk_ref[...]
    k = k_in * (1.0 + (a - 1.0) * ka_ref[...])
    bonus_rk = r * k * rk_ref[...]

    ri, ci = _square_masks(chunk)
    tril = ri >= ci
    strict = ri > ci
    cum = _chunk_cumsum(tril, log_decay)
    heads = range(RWKV_HEADS)
    slices = [slice(h * RWKV_N, (h + 1) * RWKV_N) for h in heads]
    per_head = lambda t: [t[:, s] for s in slices]
    r_h, k_h, v_h, cum_h = per_head(r), per_head(k), per_head(v), per_head(cum)
    kk_h = [x * lax.rsqrt(jnp.sum(x * x, -1, keepdims=True) + L2_EPS) for x in per_head(kk_raw)]
    b_h = _each(jnp.multiply, kk_h, per_head(a))
    cum_last = [c[chunk - 1:chunk, :] for c in cum_h]
    grow = [jnp.exp(-c) for c in cum_h]
    tail = _each(lambda cl, c: jnp.exp(cl - c), cum_last, cum_h)
    lhs = _each(lambda kk, r_, c, ld: jnp.concatenate([kk * jnp.exp(c - ld), r_ * jnp.exp(c)], axis=0),
                kk_h, r_h, cum_h, per_head(log_decay))
    states = [s_ref[h] for h in heads]
    g_b = _each(lambda x, b, g: _mm_nt(x, b * g), lhs, b_h, grow)
    g_k = _each(lambda x, k_, g: _mm_nt(x, k_ * g), lhs, k_h, grow)
    from_state = _each(_mm_nt, lhs, states)
    inverses = _unit_lower_inverses([jnp.where(strict, g[:chunk], 0.0) for g in g_b], chunk)
    rhs = _each(lambda fs, g, v_: fs[:chunk] + _mm(jnp.where(strict, g[:chunk], 0.0), v_), from_state, g_k, v_h)
    u = _each(_mm, inverses, rhs)
    y_v = _each(lambda g, v_: _mm(jnp.where(tril, g[chunk:], 0.0), v_), g_k, v_h)
    y_u = _each(lambda g, u_: _mm(jnp.where(tril, g[chunk:], 0.0), u_), g_b, u)
    outer = _each(lambda v_, u_, k_, b, t: _mm_tn(jnp.concatenate([v_, -u_], axis=0),
                                                  jnp.concatenate([k_ * t, b * t], axis=0)),
                  v_h, u, k_h, b_h, tail)
    for h in heads:
        hs = slices[h]
        s_ref[h] = states[h] * jnp.exp(cum_last[h]) + outer[h]
        y = from_state[h][chunk:] + y_v[h] - y_u[h]
        mean = jnp.mean(y, -1, keepdims=True)
        var = jnp.mean(jnp.square(y - mean), -1, keepdims=True)
        y = (y - mean) * lax.rsqrt(var + RWKV_GN_EPS) * lnw_ref[:, hs] + lnb_ref[:, hs]
        y = y + jnp.sum(bonus_rk[:, hs], -1, keepdims=True) * v_h[h]
        o_ref[:, hs] = y * _silu(gate[:, hs])


def _rwkv(p_flat, s0, shift0, params, batch, seq, chunk):
    n_chunks = seq // chunk
    const = lambda a: pl.BlockSpec(a.shape, lambda b, c: (0,) * a.ndim)
    state_spec = pl.BlockSpec((None, RWKV_HEADS, RWKV_N, RWKV_N), lambda b, c: (b, 0, 0, 0))
    shift_spec = pl.BlockSpec((None, 1, SHIFT_W), lambda b, c: (b, 0, 0))
    shift0 = shift0.reshape(batch, 1, SHIFT_W)
    o, s, shift = pl.pallas_call(
        functools.partial(_rwkv_kernel, chunk=chunk),
        grid=(batch, n_chunks),
        in_specs=[pl.BlockSpec((chunk, RWKV_IN), lambda b, c: (b * n_chunks + c, 0)), state_spec, shift_spec]
        + [const(a) for a in params],
        out_specs=[pl.BlockSpec((chunk, RWKV_W), lambda b, c: (b * n_chunks + c, 0)), state_spec, shift_spec],
        out_shape=[jax.ShapeDtypeStruct((batch * seq, RWKV_W), F32), jax.ShapeDtypeStruct(s0.shape, F32),
                   jax.ShapeDtypeStruct(shift0.shape, F32)],
        compiler_params=pltpu.CompilerParams(dimension_semantics=("parallel", "arbitrary"),
                                             vmem_limit_bytes=VMEM_LIMIT),
        name="rwkv7",
    )(p_flat, s0, shift0, *params)
    return o, s, shift.reshape(batch, SHIFT_W)


def _gdn_kernel(qkv_ref, z_ref, ba_ref, s0_ref, c0_ref, cw_ref, alog_ref, dtb_ref, gnw_ref,
                o_ref, s_ref, cout_ref, xbuf_ref, *, chunk):
    taps = GDN_CONV - 1

    @pl.when(pl.program_id(1) == 0)
    def _():
        s_ref[...] = s0_ref[...]
        xbuf_ref[SUBLANES - taps:SUBLANES, :] = c0_ref[...]

    x = qkv_ref[...]
    xbuf_ref[SUBLANES:SUBLANES + chunk, :] = x
    conv = x * cw_ref[taps:taps + 1, :]
    for j in range(taps):
        conv = conv + xbuf_ref[SUBLANES - taps + j:SUBLANES - taps + j + chunk, :] * cw_ref[j:j + 1, :]
    carry = xbuf_ref[SUBLANES + chunk - taps:SUBLANES + chunk, :]
    cout_ref[...] = carry
    xbuf_ref[SUBLANES - taps:SUBLANES, :] = carry

    act = _silu(conv)
    z = z_ref[...]
    ba = ba_ref[...]
    beta = jax.nn.sigmoid(ba[:, :GDN_HEADS])
    g = -jnp.exp(alog_ref[...]) * _softplus(ba[:, GDN_HEADS:] + dtb_ref[...])
    ri, ci = _square_masks(chunk)
    tril = ri >= ci
    strict = ri > ci
    eye = ri == ci
    gcum = _chunk_cumsum(tril, g)
    heads = range(GDN_HEADS)
    head_slice = lambda base, h: slice(base + h * GDN_DK, base + (h + 1) * GDN_DK)
    l2 = lambda x: x * lax.rsqrt(jnp.sum(x * x, -1, keepdims=True) + L2_EPS)
    q_h = [l2(act[:, head_slice(0, h)]) * (GDN_DK ** -0.5) for h in heads]
    k_h = [l2(act[:, head_slice(GDN_QK, h)]) for h in heads]
    v_h = [act[:, head_slice(2 * GDN_QK, h)] for h in heads]
    beta_h = [beta[:, h:h + 1] for h in heads]
    gc = [gcum[:, h:h + 1] for h in heads]
    gc_row = [jnp.sum(jnp.where(eye, g_, 0.0), axis=0, keepdims=True) for g_ in gc]
    decay = _each(lambda c, r: jnp.where(tril, jnp.exp(jnp.where(tril, c - r, 0.0)), 0.0), gc, gc_row)
    kb = _each(jnp.multiply, k_h, beta_h)
    exp_gc = [jnp.exp(g_) for g_ in gc]
    g_last = [g_[chunk - 1:chunk, :] for g_ in gc]
    states = [s_ref[h] for h in heads]
    lower = _each(lambda kb_, k_, d: jnp.where(strict, _mm_nt(kb_, k_) * d, 0.0), kb, k_h, decay)
    attn = _each(lambda q_, k_, d: _mm_nt(q_, k_) * d, q_h, k_h, decay)
    o_state = _each(lambda q_, e, s: _mm(q_ * e, s), q_h, exp_gc, states)
    inverses = _unit_lower_inverses(lower, chunk)
    sol = _each(lambda inv, v_, b, kb_, e: _mm(inv, jnp.concatenate([v_ * b, kb_ * e], axis=1)),
                inverses, v_h, beta_h, kb, exp_gc)
    v_new = _each(lambda s_, state: s_[:, :GDN_DV] - _mm(s_[:, GDN_DV:], state), sol, states)
    o_intra = _each(_mm, attn, v_new)
    outer = _each(lambda k_, gl, g_, vn: _mm_tn(k_ * jnp.exp(gl - g_), vn), k_h, g_last, gc, v_new)
    for h in heads:
        os_ = head_slice(0, h)
        s_ref[h] = states[h] * jnp.exp(g_last[h]) + outer[h]
        o = o_state[h] + o_intra[h]
        o = o * lax.rsqrt(jnp.mean(o * o, -1, keepdims=True) + NORM_EPS) * gnw_ref[...]
        o_ref[:, os_] = o * _silu(z[:, os_])


def _gdn(qkv, z, ba, s0, conv0, params, batch, seq, chunk):
    n_chunks = seq // chunk
    const = lambda a: pl.BlockSpec(a.shape, lambda b, c: (0,) * a.ndim)
    rows = lambda w: pl.BlockSpec((chunk, w), lambda b, c: (b * n_chunks + c, 0))
    state_spec = pl.BlockSpec((None, GDN_HEADS, GDN_DK, GDN_DV), lambda b, c: (b, 0, 0, 0))
    conv_spec = pl.BlockSpec((None, GDN_CONV - 1, GDN_QKV), lambda b, c: (b, 0, 0))
    return pl.pallas_call(
        functools.partial(_gdn_kernel, chunk=chunk),
        grid=(batch, n_chunks),
        in_specs=[rows(GDN_QKV), rows(GDN_VW), rows(2 * GDN_HEADS), state_spec, conv_spec]
        + [const(a) for a in params],
        out_specs=[rows(GDN_VW), state_spec, conv_spec],
        out_shape=[jax.ShapeDtypeStruct((batch * seq, GDN_VW), F32), jax.ShapeDtypeStruct(s0.shape, F32),
                   jax.ShapeDtypeStruct(conv0.shape, F32)],
        scratch_shapes=[pltpu.VMEM((chunk + SUBLANES, GDN_QKV), F32)],
        compiler_params=pltpu.CompilerParams(dimension_semantics=("parallel", "arbitrary"),
                                             vmem_limit_bytes=VMEM_LIMIT),
        name="gated_delta",
    )(qkv, z, ba, s0, conv0, *params)


def _run_group(x, pos, s_ret, s_rwkv, s_shift, s_gdn, s_conv, chunk, w):
    batch, seq, _ = x.shape
    n_tokens = batch * seq
    h0 = x.reshape(n_tokens, D_MODEL)
    p_ret, p_rwkv = _token_call(
        _even_in_kernel, n_tokens, {"x": h0}, {"nw": w["norm_e"], "w": w["w_in_e"]},
        [RET_W, RWKV_IN], ["x", "nw", "w"], "even_in")
    o_ret, ret_new = _retention(p_ret, pos, s_ret, w["ret_norm_w"], batch, seq, chunk)
    o_rwkv, rwkv_new, shift_new = _rwkv(p_rwkv, s_rwkv, s_shift, w["rwkv_params"], batch, seq, chunk)
    h1, qkv, z, ba = _token_call(
        _even_out_odd_in_kernel, n_tokens, {"h": h0, "oret": o_ret, "orwkv": o_rwkv},
        {"wout": w["w_out_e"], "nw": w["norm_o"], "win": w["w_in_o"], "wba": w["w_ba_o"]},
        [D_MODEL, GDN_QKV, GDN_VW, 2 * GDN_HEADS], ["h", "oret", "orwkv", "wout", "nw", "win", "wba"],
        "even_out_odd_in")
    o_gdn, gdn_new, conv_new = _gdn(qkv, z, ba, s_gdn, s_conv, w["gdn_params"], batch, seq, chunk)
    (y,) = _token_call(
        _odd_out_final_kernel, n_tokens, {"h": h1, "o": o_gdn}, {"wout": w["w_out_o"], "nw": w["final_norm"]},
        [D_MODEL], ["h", "o", "wout", "nw"], "odd_out_final")
    return (y.reshape(batch, seq, D_MODEL), ret_new[None], rwkv_new[None], shift_new[None], gdn_new[None],
            conv_new[None])


def kernel(x_prompt, x_sample, state_ret, state_rwkv, state_shift, state_gdn, state_conv, norm_e, w_in_e, rwkv_mu, rwkv_w0, rwkv_w2, rwkv_a0, rwkv_a2, rwkv_kk, rwkv_ka, rwkv_rk, rwkv_ln_w, rwkv_ln_b, ret_norm_w, w_out_e, norm_o, w_in_o, gdn_conv_w, gdn_a_log, gdn_dt_bias, gdn_norm_w, w_out_o, final_norm):
    assert state_ret.shape[0] == 1 and state_gdn.shape[0] == 1, "one even and one odd layer"
    row = lambda a: a.reshape(1, -1)
    n_qkvz = GDN_QKV + GDN_VW
    w = {
        "norm_e": row(norm_e[0]), "w_in_e": w_in_e[0].astype(BF16), "ret_norm_w": row(ret_norm_w[0]),
        "rwkv_params": (row(rwkv_mu[0]), row(rwkv_w0[0]), rwkv_w2[0], row(rwkv_a0[0]), rwkv_a2[0], row(rwkv_kk[0]),
                        row(rwkv_ka[0]), row(rwkv_rk[0]), row(rwkv_ln_w[0]), row(rwkv_ln_b[0])),
        "w_out_e": w_out_e[0].astype(BF16), "norm_o": row(norm_o[0]),
        "w_in_o": w_in_o[0][:, :n_qkvz].astype(BF16), "w_ba_o": w_in_o[0][:, n_qkvz:].astype(BF16),
        "gdn_params": (gdn_conv_w[0], row(gdn_a_log[0]), row(gdn_dt_bias[0]), row(gdn_norm_w[0])),
        "w_out_o": w_out_o[0].astype(BF16), "final_norm": row(final_norm),
    }
    batch, seq, _ = x_prompt.shape
    dec_batch, dec_seq, _ = x_sample.shape
    zeros = lambda s: jnp.zeros((batch,) + s.shape[2:], F32)
    prompt = _run_group(x_prompt, jnp.arange(seq), zeros(state_ret), zeros(state_rwkv), zeros(state_shift),
                        zeros(state_gdn), zeros(state_conv), math.gcd(seq, PROMPT_CHUNK), w)
    sample = _run_group(x_sample, PAST_LEN + jnp.arange(dec_seq), state_ret[0], state_rwkv[0], state_shift[0],
                        state_gdn[0], state_conv[0], dec_seq, w)
    return (prompt[0], sample[0]) + prompt[1:] + sample[1:]
```

```python
import functools
import math
from typing import NamedTuple

import jax
import jax.numpy as jnp
from jax import lax
from jax.experimental import pallas as pl
from jax.experimental.pallas import tpu as pltpu

F32 = jnp.float32
BF16 = jnp.bfloat16

D_MODEL = 1024
PAST_LEN = 16384
RET_HEADS, RET_DK, RET_DV = 4, 64, 128
RET_ROPE_BASE = 10000.0
RET_Q = RET_HEADS * RET_DK
RET_V = RET_HEADS * RET_DV
RET_W = 2 * RET_Q + 2 * RET_V
RWKV_HEADS, RWKV_N = 8, 64
RWKV_W = RWKV_HEADS * RWKV_N
RWKV_LORA = 64
RWKV_GN_EPS = 64e-5
SHIFT_W = 3 * RWKV_W + 2 * RWKV_LORA
RWKV_IN = SHIFT_W + RWKV_W
GDN_HEADS, GDN_DK, GDN_DV, GDN_CONV = 8, 128, 128, 4
GDN_QK = GDN_HEADS * GDN_DK
GDN_VW = GDN_HEADS * GDN_DV
GDN_QKV = 2 * GDN_QK + GDN_VW
NORM_EPS = 1e-6
L2_EPS = 1e-12

SUBLANES = 8
INV_BLOCK = 16
VMEM_LIMIT = 56 * 1024 * 1024
PROMPT_CHUNK = 64
PROMPT_CHUNKS_PER_STEP = 4
SAMPLE_SEQS_PER_STEP = 8
SAMPLE_SEQS_JOINT = 2
PREPARE_STAGES_PER_FINISH_STAGE = 3
TOKEN_TILE = 256


def _split_bf16(x):
    hi = x.astype(BF16)
    return hi, (x - hi.astype(F32)).astype(BF16)


def _dot(a, b, dims, mode):
    dot = lambda x, y: lax.dot_general(x, y, (dims, ((), ())), preferred_element_type=F32)
    if mode == "bf16":
        return dot(a.astype(BF16), b.astype(BF16))
    assert mode == "bf16x3", mode
    a_hi, a_lo = _split_bf16(a)
    b_hi, b_lo = _split_bf16(b)
    return (dot(a_lo, b_hi) + dot(a_hi, b_lo)) + dot(a_hi, b_hi)


def _mm(a, b, mode="bf16"):
    return _dot(a, b, ((1,), (0,)), mode)


def _mm_nt(a, b, mode="bf16"):
    return _dot(a, b, ((1,), (1,)), mode)


def _mm_tn(a, b, mode="bf16"):
    return _dot(a, b, ((0,), (0,)), mode)


def _mm3(a, b):
    return _mm(a, b, "bf16x3")


def _chunk_cumsum(tril, x):
    ones = tril.astype(BF16)
    hi = x.astype(BF16)
    rest = x - hi.astype(F32)
    mid = rest.astype(BF16)
    lo = (rest - mid.astype(F32)).astype(BF16)
    dot = lambda p: jnp.dot(ones, p, preferred_element_type=F32)
    return (dot(lo) + dot(mid)) + dot(hi)


def _silu(x):
    return x * jax.nn.sigmoid(x)


def _softplus(x):
    return jnp.maximum(x, 0.0) + jnp.log1p(jnp.exp(-jnp.abs(x)))


def _square_masks(n):
    ri = lax.broadcasted_iota(jnp.int32, (n, n), 0)
    ci = lax.broadcasted_iota(jnp.int32, (n, n), 1)
    return ri, ci


def _each(fn, *seqs):
    return [fn(*args) for args in zip(*seqs)]


def _neumann_inverses(lows, eye, n):
    invs = [eye - low for low in lows]
    powers = list(lows)
    k = 2
    while k < n:
        powers = _each(_mm, powers, powers)
        yield
        invs = _each(lambda inv, p: inv + _mm(inv, p), invs, powers)
        yield
        k *= 2
    return invs


def _unit_lower_inverses(lows, n):
    ri, ci = _square_masks(n)
    eye = (ri == ci).astype(F32)
    if n <= INV_BLOCK:
        invs = yield from _neumann_inverses(lows, eye, n)
    else:
        shift = int(math.log2(INV_BLOCK))
        same_block = (ri >> shift) == (ci >> shift)
        diag_invs = yield from _neumann_inverses([jnp.where(same_block, low, 0.0) for low in lows], eye, INV_BLOCK)
        offs = _each(lambda d, low: _mm(d, jnp.where(same_block, 0.0, low)), diag_invs, lows)
        yield
        off_invs = yield from _neumann_inverses(offs, eye, n // INV_BLOCK)
        invs = _each(_mm, off_invs, diag_invs)
        yield
    residuals = _each(lambda low, inv: (eye - inv) - _mm3(low, inv), lows, invs)
    yield
    refined = _each(lambda inv, res: inv + _mm(inv, res), invs, residuals)
    yield
    return refined


def _rmsnorm_rows(x, w):
    return x * lax.rsqrt(jnp.mean(x * x, -1, keepdims=True) + NORM_EPS) * w


def _even_in_kernel(x_ref, nw_ref, w_ref, ret_ref, rwkv_ref):
    xn = _rmsnorm_rows(x_ref[...], nw_ref[...]).astype(BF16)
    ret_ref[...] = jnp.dot(xn, w_ref[:, :RET_W], preferred_element_type=F32)
    rwkv_ref[...] = jnp.dot(xn, w_ref[:, RET_W:], preferred_element_type=F32)


def _even_out_odd_in_kernel(h_ref, oret_ref, orwkv_ref, wout_ref, nw_ref, win_ref, wba_ref,
                            h1_ref, qkv_ref, z_ref, ba_ref):
    mix = jnp.dot(oret_ref[...].astype(BF16), wout_ref[:RET_V, :], preferred_element_type=F32)
    mix = mix + jnp.dot(orwkv_ref[...].astype(BF16), wout_ref[RET_V:, :], preferred_element_type=F32)
    h1 = h_ref[...] + mix
    h1_ref[...] = h1
    xn = _rmsnorm_rows(h1, nw_ref[...]).astype(BF16)
    qkv_ref[...] = jnp.dot(xn, win_ref[:, :GDN_QKV], preferred_element_type=F32)
    z_ref[...] = jnp.dot(xn, win_ref[:, GDN_QKV:], preferred_element_type=F32)
    ba_ref[...] = jnp.dot(xn, wba_ref[...], preferred_element_type=F32)


def _odd_out_final_kernel(h_ref, o_ref, wout_ref, nw_ref, y_ref):
    h2 = h_ref[...] + jnp.dot(o_ref[...].astype(BF16), wout_ref[...], preferred_element_type=F32)
    y_ref[...] = _rmsnorm_rows(h2, nw_ref[...])


def _row_spec(tile, width):
    return pl.BlockSpec((tile, width), lambda i: (i, 0))


def _full_spec(shape):
    return pl.BlockSpec(shape, lambda i: (0,) * len(shape))


def _token_call(kernel, n_tokens, row_inputs, full_inputs, out_widths, order, name):
    tile = TOKEN_TILE
    assert n_tokens % tile == 0
    specs = {**{k: _row_spec(tile, v.shape[1]) for k, v in row_inputs.items()},
             **{k: _full_spec(v.shape) for k, v in full_inputs.items()}}
    arrays = {**row_inputs, **full_inputs}
    return pl.pallas_call(
        kernel,
        grid=(n_tokens // tile,),
        in_specs=[specs[k] for k in order],
        out_specs=[_row_spec(tile, w) for w in out_widths],
        out_shape=[jax.ShapeDtypeStruct((n_tokens, w), F32) for w in out_widths],
        compiler_params=pltpu.CompilerParams(dimension_semantics=("parallel",), vmem_limit_bytes=VMEM_LIMIT),
        name=name,
    )(*[arrays[k] for k in order])


class _Geometry(NamedTuple):
    chunk: int
    regions: int
    chunks: int
    joint: int

    @property
    def rows(self):
        return self.regions * self.chunks * self.chunk

    def row_slice(self, region, chunk_index):
        start = (region * self.chunks + chunk_index) * self.chunk
        return slice(start, start + self.chunk)


def _recurrent_call(body, geo, batch, seq, rows, tables, states, consts, out_widths, scratch_shapes, name):
    steps = seq // (geo.chunks * geo.chunk)
    assert batch % geo.regions == 0 and steps * geo.chunks * geo.chunk == seq
    row_spec = lambda w: pl.BlockSpec((geo.rows, w), lambda b, c: (b * steps + c, 0))
    table_spec = lambda a: pl.BlockSpec((geo.rows, a.shape[1]), lambda b, c: (c, 0))
    state_spec = lambda a: pl.BlockSpec((geo.regions,) + a.shape[1:], lambda b, c: (b,) + (0,) * (a.ndim - 1))
    const_spec = lambda a: pl.BlockSpec(a.shape, lambda b, c: (0,) * a.ndim)
    return pl.pallas_call(
        functools.partial(body, geo=geo),
        grid=(batch // geo.regions, steps),
        in_specs=[row_spec(a.shape[1]) for a in rows] + [table_spec(a) for a in tables]
        + [state_spec(a) for a in states] + [const_spec(a) for a in consts],
        out_specs=[row_spec(w) for w in out_widths] + [state_spec(a) for a in states],
        out_shape=[jax.ShapeDtypeStruct((batch * seq, w), F32) for w in out_widths]
        + [jax.ShapeDtypeStruct(a.shape, F32) for a in states],
        scratch_shapes=scratch_shapes,
        compiler_params=pltpu.CompilerParams(dimension_semantics=("parallel", "arbitrary"),
                                             vmem_limit_bytes=VMEM_LIMIT),
        name=name,
    )(*rows, *tables, *states, *consts)


def _run_step(geo, prepare, finish):
    groups = [list(range(g, g + geo.joint)) for g in range(0, geo.regions, geo.joint)]
    items = [(group, ci) for group in groups for ci in range(geo.chunks)]

    def advance(gen, stages):
        for _ in range(stages):
            try:
                next(gen)
            except StopIteration as stop:
                return True, stop.value
        return False, None

    _, prepared = advance(prepare(*items[0]), 10 ** 6)
    for n, item in enumerate(items):
        finishing = finish(*item, prepared)
        preparing = prepare(*items[n + 1]) if n + 1 < len(items) else None
        finished, prepared_next = False, preparing is None
        prepared = None
        while not (finished and prepared_next):
            if not finished:
                finished, _ = advance(finishing, 1)
            if not prepared_next:
                prepared_next, prepared = advance(preparing, PREPARE_STAGES_PER_FINISH_STAGE)


def _ret_kernel(p_ref, cos_ref, sin_ref, s0_ref, dec_ref, qs_ref, ks_ref, ss_ref, nw_ref, o_ref, s_ref, *, geo):
    @pl.when(pl.program_id(1) == 0)
    def _():
        s_ref[...] = s0_ref[...]

    lane = lax.broadcasted_iota(jnp.int32, (geo.chunk, RET_Q), 1)
    even = (lane & 1) == 0
    heads = range(RET_HEADS)
    k_slices = [slice(h * RET_DK, (h + 1) * RET_DK) for h in heads]
    v_slices = [slice(h * RET_DV, (h + 1) * RET_DV) for h in heads]

    def rotary(x, cos, sin):
        partner = jnp.where(even, pltpu.roll(x, RET_Q - 1, 1), pltpu.roll(x, 1, 1))
        return x * cos + partner * sin

    def prepare(group, ci):
        q_l, k_l, qs_l, ks_l, v_l, dec_l, ss_l, post = [], [], [], [], [], [], [], []
        for reg in group:
            rows = geo.row_slice(reg, ci)
            p = p_ref[rows, :]
            cos, sin = cos_ref[rows, :], sin_ref[rows, :]
            q = rotary(p[:, :RET_Q], cos, sin)
            k = rotary(p[:, RET_Q:2 * RET_Q], cos, sin) * (RET_DK ** -0.5)
            q_state = q * qs_ref[...]
            k_state = k * ks_ref[...]
            for h in heads:
                q_l.append(q[:, k_slices[h]])
                k_l.append(k[:, k_slices[h]])
                qs_l.append(q_state[:, k_slices[h]])
                ks_l.append(k_state[:, k_slices[h]])
                v_l.append(p[:, 2 * RET_Q + h * RET_DV:2 * RET_Q + (h + 1) * RET_DV])
                dec_l.append(dec_ref[h])
                ss_l.append(ss_ref[h])
                post.append((rows, v_slices[h], p[:, 2 * RET_Q + RET_V + h * RET_DV:
                                                  2 * RET_Q + RET_V + (h + 1) * RET_DV]))
        scores = _each(lambda q_, k_, d: _mm_nt(q_, k_) * d, q_l, k_l, dec_l)
        yield
        intra = _each(_mm, scores, v_l)
        yield
        outer = _each(_mm_tn, ks_l, v_l)
        yield
        return qs_l, ss_l, post, intra, outer

    states = {}

    def finish(group, ci, prepared):
        qs_l, ss_l, post, intra, outer = prepared
        key = group[0]
        if ci == 0:
            states[key] = [s_ref[reg, h] for reg in group for h in heads]
        from_state = _each(_mm, qs_l, states[key])
        yield
        states[key] = _each(lambda s, ss, o: s * ss + o, states[key], ss_l, outer)
        o_l = _each(jnp.add, from_state, intra)
        ms_l = [jnp.mean(o * o, -1, keepdims=True) for o in o_l]
        for (rows, vs_, gate), o, ms in zip(post, o_l, ms_l):
            o_ref[rows, vs_] = o * lax.rsqrt(ms + NORM_EPS) * nw_ref[:, vs_] * _silu(gate)
        if ci == geo.chunks - 1:
            for i, reg in enumerate(group):
                for h in heads:
                    s_ref[reg, h] = states[key][i * RET_HEADS + h]

    _run_step(geo, prepare, finish)


def _ret_tables(pos, geo):
    chunk = geo.chunk
    half = RET_DK // 2
    inv = 1.0 / (RET_ROPE_BASE ** jnp.linspace(0.0, 1.0, half, dtype=F32))
    ang = pos.astype(F32)[:, None] * inv[None, :]
    cos = jnp.repeat(jnp.cos(ang), 2, axis=-1)
    sin = jnp.stack([-jnp.sin(ang), jnp.sin(ang)], -1).reshape(ang.shape[0], RET_DK)
    cos, sin = jnp.tile(cos, (geo.regions, RET_HEADS)), jnp.tile(sin, (geo.regions, RET_HEADS))
    lg = jnp.log1p(-jnp.exp2(-jnp.linspace(5.0, 12.0, RET_HEADS, dtype=F32)))
    ci = jnp.arange(chunk, dtype=F32)
    diff = ci[:, None] - ci[None, :]
    causal = diff >= 0
    decay = jnp.where(causal, jnp.exp(jnp.where(causal, diff, 0.0) * lg[:, None, None]), 0.0)
    q_scale = jnp.exp((ci + 1.0) * lg[:, None])
    k_scale = jnp.exp((chunk - 1.0 - ci) * lg[:, None])
    s_scale = jnp.exp(chunk * lg)
    widen = lambda t: jnp.repeat(t.T, RET_DK, axis=1)
    s_scale = jnp.broadcast_to(s_scale[:, None, None], (RET_HEADS, 1, RET_DV))
    return cos, sin, decay, widen(q_scale), widen(k_scale), s_scale


def _retention(p_flat, pos, s0, norm_w, batch, seq, geo):
    cos, sin, decay, q_scale, k_scale, s_scale = _ret_tables(pos, geo)
    return _recurrent_call(_ret_kernel, geo, batch, seq, [p_flat], [cos, sin], [s0],
                           [decay, q_scale, k_scale, s_scale, norm_w], [RET_V], [], "retention")


def _rwkv_kernel(p_ref, s0_ref, sh0_ref, mu_ref, w0_ref, w2_ref, a0_ref, a2_ref, kk_ref, ka_ref, rk_ref,
                 lnw_ref, lnb_ref, o_ref, s_ref, shout_ref, *, geo):
    chunk = geo.chunk

    @pl.when(pl.program_id(1) == 0)
    def _():
        s_ref[...] = s0_ref[...]
        shout_ref[...] = sh0_ref[...]

    ri, ci_ = _square_masks(chunk)
    tril = ri >= ci_
    strict = ri > ci_
    first_row = lax.broadcasted_iota(jnp.int32, (chunk, SHIFT_W), 0) == 0
    heads = range(RWKV_HEADS)
    slices = [slice(h * RWKV_N, (h + 1) * RWKV_N) for h in heads]
    per_head = lambda t: [t[:, s] for s in slices]
    kr, kc = _square_masks(RWKV_N)
    key_eye = kr == kc

    def prepare(group, ci):
        r_h, k_h, v_h, kk_h, b_h, cum_h, ld_h, bonus_h, post = [], [], [], [], [], [], [], [], []
        for reg in group:
            rows = geo.row_slice(reg, ci)
            p = p_ref[rows, :]
            sh = p[:, :SHIFT_W]
            before = shout_ref[reg] if ci == 0 else p_ref[rows.start - 1:rows.start, :SHIFT_W]
            prev = jnp.where(first_row, before, pltpu.roll(sh, 1, 0))
            if ci == geo.chunks - 1:
                shout_ref[reg] = sh[chunk - 1:chunk, :]
            xs = sh + (prev - sh) * mu_ref[...]
            r = xs[:, :RWKV_W]
            k_in = xs[:, RWKV_W:2 * RWKV_W]
            v = xs[:, 2 * RWKV_W:3 * RWKV_W]
            wd = xs[:, 3 * RWKV_W:3 * RWKV_W + RWKV_LORA]
            ad = xs[:, 3 * RWKV_W + RWKV_LORA:]
            w = -_softplus(-(w0_ref[...] + _mm3(jnp.tanh(wd), w2_ref[...]))) - 0.5
            log_decay = -jnp.exp(w)
            a = jax.nn.sigmoid(a0_ref[...] + _mm3(ad, a2_ref[...]))
            k = k_in * (1.0 + (a - 1.0) * ka_ref[...])
            kk = [x * lax.rsqrt(jnp.sum(x * x, -1, keepdims=True) + L2_EPS) for x in per_head(k_in * kk_ref[...])]
            r_h += per_head(r)
            k_h += per_head(k)
            v_h += per_head(v)
            kk_h += kk
            b_h += _each(jnp.multiply, kk, per_head(a))
            cum_h += per_head(_chunk_cumsum(tril, log_decay))
            ld_h += per_head(log_decay)
            bonus_h += [jnp.sum(x, -1, keepdims=True) for x in per_head(r * k * rk_ref[...])]
            post += [(rows, slices[h], p[:, SHIFT_W + h * RWKV_N:SHIFT_W + (h + 1) * RWKV_N]) for h in heads]
        yield
        cum_last = [c[chunk - 1:chunk, :] for c in cum_h]
        grow = [jnp.exp(-c) for c in cum_h]
        tail = _each(lambda cl, c: jnp.exp(cl - c), cum_last, cum_h)
        lhs = _each(lambda kk, r_, c, ld: jnp.concatenate([kk * jnp.exp(c - ld), r_ * jnp.exp(c)], axis=0),
                    kk_h, r_h, cum_h, ld_h)
        g_b = _each(lambda x, b, g: _mm_nt(x, b * g), lhs, b_h, grow)
        yield
        inverting = _unit_lower_inverses([jnp.where(strict, g[:chunk], 0.0) for g in g_b], chunk)
        g_k = _each(lambda x, k_, g: _mm_nt(x, k_ * g), lhs, k_h, grow)
        yield
        from_v = _each(lambda g, v_: _mm(jnp.where(strict, g[:chunk], 0.0), v_), g_k, v_h)
        yield
        y_v = _each(lambda g, v_: _mm(jnp.where(tril, g[chunk:], 0.0), v_), g_k, v_h)
        yield
        m_rb = [jnp.where(tril, g[chunk:], 0.0) for g in g_b]
        k_tail_t = _each(lambda k_, b, t: jnp.concatenate([k_ * t, b * t], axis=0).T, k_h, b_h, tail)
        state_decay = [jnp.sum(jnp.where(key_eye, jnp.exp(cl), 0.0), axis=1, keepdims=True) for cl in cum_last]
        inverses = yield from inverting
        return lhs, inverses, from_v, y_v, m_rb, k_tail_t, v_h, state_decay, bonus_h, post

    states = {}

    def finish(group, ci, prepared):
        lhs, inverses, from_v, y_v, m_rb, k_tail_t, v_h, state_decay, bonus_h, post = prepared
        key = group[0]
        if ci == 0:
            states[key] = [s_ref[reg, h].T for reg in group for h in heads]
        from_state = _each(_mm, lhs, states[key])
        yield
        u = _each(lambda inv, fs, fv: _mm(inv, fs[:chunk] + fv), inverses, from_state, from_v)
        yield
        outer = _each(lambda kt, v_, u_: _mm(kt, jnp.concatenate([v_, -u_], axis=0)), k_tail_t, v_h, u)
        yield
        y_u = _each(_mm, m_rb, u)
        yield
        states[key] = _each(lambda s, d, o: s * d + o, states[key], state_decay, outer)
        y_l = _each(lambda fs, yv, yu: fs[chunk:] + yv - yu, from_state, y_v, y_u)
        mean_l = [jnp.mean(y, -1, keepdims=True) for y in y_l]
        cen_l = _each(jnp.subtract, y_l, mean_l)
        var_l = [jnp.mean(jnp.square(c), -1, keepdims=True) for c in cen_l]
        for (rows, hs, gate), cen, var, bonus, v_ in zip(post, cen_l, var_l, bonus_h, v_h):
            y = cen * lax.rsqrt(var + RWKV_GN_EPS) * lnw_ref[:, hs] + lnb_ref[:, hs]
            o_ref[rows, hs] = (y + bonus * v_) * _silu(gate)
        if ci == geo.chunks - 1:
            for i, reg in enumerate(group):
                for h in heads:
                    s_ref[reg, h] = states[key][i * RWKV_HEADS + h].T

    _run_step(geo, prepare, finish)


def _rwkv(p_flat, s0, shift0, params, batch, seq, geo):
    o, s, shift = _recurrent_call(_rwkv_kernel, geo, batch, seq, [p_flat], [],
                                  [s0, shift0.reshape(batch, 1, SHIFT_W)], list(params), [RWKV_W], [], "rwkv7")
    return o, s, shift.reshape(batch, SHIFT_W)


def _gdn_kernel(qkv_ref, z_ref, ba_ref, s0_ref, c0_ref, cw_ref, alog_ref, dtb_ref, gnw_ref,
                o_ref, s_ref, cout_ref, xbuf_ref, *, geo):
    chunk = geo.chunk
    taps = GDN_CONV - 1
    region_rows = geo.chunks * chunk
    stride = region_rows + SUBLANES

    @pl.when(pl.program_id(1) == 0)
    def _():
        s_ref[...] = s0_ref[...]
        cout_ref[...] = c0_ref[...]

    ri, ci_ = _square_masks(chunk)
    tril = ri >= ci_
    strict = ri > ci_
    eye = ri == ci_
    heads = range(GDN_HEADS)
    head_slice = lambda base, h: slice(base + h * GDN_DK, base + (h + 1) * GDN_DK)
    l2 = lambda x: x * lax.rsqrt(jnp.sum(x * x, -1, keepdims=True) + L2_EPS)

    for reg in range(geo.regions):
        base = reg * stride + SUBLANES
        start = reg * region_rows
        xbuf_ref[base - taps:base, :] = cout_ref[reg]
        xbuf_ref[base:base + region_rows, :] = qkv_ref[start:start + region_rows, :]
        cout_ref[reg] = qkv_ref[start + region_rows - taps:start + region_rows, :]

    def prepare(group, ci):
        q_h, k_h, v_h, beta_h, gc, post = [], [], [], [], [], []
        for reg in group:
            rows = geo.row_slice(reg, ci)
            at = reg * stride + SUBLANES + ci * chunk
            conv = xbuf_ref[at:at + chunk, :] * cw_ref[taps:taps + 1, :]
            for j in range(taps):
                conv = conv + xbuf_ref[at - taps + j:at - taps + j + chunk, :] * cw_ref[j:j + 1, :]
            act = _silu(conv)
            ba = ba_ref[rows, :]
            beta = jax.nn.sigmoid(ba[:, :GDN_HEADS])
            g = -jnp.exp(alog_ref[...]) * _softplus(ba[:, GDN_HEADS:] + dtb_ref[...])
            gcum = _chunk_cumsum(tril, g)
            q_h += [l2(act[:, head_slice(0, h)]) * (GDN_DK ** -0.5) for h in heads]
            k_h += [l2(act[:, head_slice(GDN_QK, h)]) for h in heads]
            v_h += [act[:, head_slice(2 * GDN_QK, h)] for h in heads]
            beta_h += [beta[:, h:h + 1] for h in heads]
            gc += [gcum[:, h:h + 1] for h in heads]
            post += [(rows, head_slice(0, h)) for h in heads]
        yield
        gc_row = [jnp.sum(jnp.where(eye, g_, 0.0), axis=0, keepdims=True) for g_ in gc]
        decay = _each(lambda c, r: jnp.where(tril, jnp.exp(jnp.where(tril, c - r, 0.0)), 0.0), gc, gc_row)
        kb = _each(jnp.multiply, k_h, beta_h)
        exp_gc = [jnp.exp(g_) for g_ in gc]
        g_last = [g_[chunk - 1:chunk, :] for g_ in gc]
        lower = _each(lambda kb_, k_, d: jnp.where(strict, _mm_nt(kb_, k_) * d, 0.0), kb, k_h, decay)
        yield
        attn = _each(lambda q_, k_, d: _mm_nt(q_, k_) * d, q_h, k_h, decay)
        yield
        q_state = _each(jnp.multiply, q_h, exp_gc)
        k_tail_t = _each(lambda k_, gl, g_: (k_ * jnp.exp(gl - g_)).T, k_h, g_last, gc)
        inverses = yield from _unit_lower_inverses(lower, chunk)
        sol = _each(lambda inv, v_, b, kb_, e: _mm(inv, jnp.concatenate([v_ * b, kb_ * e], axis=1)),
                    inverses, v_h, beta_h, kb, exp_gc)
        yield
        return q_state, sol, attn, k_tail_t, [jnp.exp(gl) for gl in g_last], post

    states = {}

    def finish(group, ci, prepared):
        q_state, sol, attn, k_tail_t, state_decay, post = prepared
        key = group[0]
        if ci == 0:
            states[key] = [s_ref[reg, h] for reg in group for h in heads]
        v_new = _each(lambda s_, state: s_[:, :GDN_DV] - _mm(s_[:, GDN_DV:], state), sol, states[key])
        yield
        o_state = _each(_mm, q_state, states[key])
        yield
        outer = _each(_mm, k_tail_t, v_new)
        yield
        o_intra = _each(_mm, attn, v_new)
        yield
        states[key] = _each(lambda s, d, o: s * d + o, states[key], state_decay, outer)
        o_l = _each(jnp.add, o_state, o_intra)
        ms_l = [jnp.mean(o * o, -1, keepdims=True) for o in o_l]
        for (rows, os_), o, ms in zip(post, o_l, ms_l):
            o_ref[rows, os_] = o * lax.rsqrt(ms + NORM_EPS) * gnw_ref[...] * _silu(z_ref[rows, os_])
        if ci == geo.chunks - 1:
            for i, reg in enumerate(group):
                for h in heads:
                    s_ref[reg, h] = states[key][i * GDN_HEADS + h]

    _run_step(geo, prepare, finish)


def _gdn(qkv, z, ba, s0, conv0, params, batch, seq, geo):
    xbuf = pltpu.VMEM((geo.regions * (geo.chunks * geo.chunk + SUBLANES), GDN_QKV), F32)
    return _recurrent_call(_gdn_kernel, geo, batch, seq, [qkv, z, ba], [], [s0, conv0], list(params),
                           [GDN_VW], [xbuf], "gated_delta")


def _run_group(x, pos, s_ret, s_rwkv, s_shift, s_gdn, s_conv, geo, w):
    batch, seq, _ = x.shape
    n_tokens = batch * seq
    h0 = x.reshape(n_tokens, D_MODEL)
    p_ret, p_rwkv = _token_call(
        _even_in_kernel, n_tokens, {"x": h0}, {"nw": w["norm_e"], "w": w["w_in_e"]},
        [RET_W, RWKV_IN], ["x", "nw", "w"], "even_in")
    o_ret, ret_new = _retention(p_ret, pos, s_ret, w["ret_norm_w"], batch, seq, geo)
    o_rwkv, rwkv_new, shift_new = _rwkv(p_rwkv, s_rwkv, s_shift, w["rwkv_params"], batch, seq, geo)
    h1, qkv, z, ba = _token_call(
        _even_out_odd_in_kernel, n_tokens, {"h": h0, "oret": o_ret, "orwkv": o_rwkv},
        {"wout": w["w_out_e"], "nw": w["norm_o"], "win": w["w_in_o"], "wba": w["w_ba_o"]},
        [D_MODEL, GDN_QKV, GDN_VW, 2 * GDN_HEADS], ["h", "oret", "orwkv", "wout", "nw", "win", "wba"],
        "even_out_odd_in")
    o_gdn, gdn_new, conv_new = _gdn(qkv, z, ba, s_gdn, s_conv, w["gdn_params"], batch, seq, geo)
    (y,) = _token_call(
        _odd_out_final_kernel, n_tokens, {"h": h1, "o": o_gdn}, {"wout": w["w_out_o"], "nw": w["final_norm"]},
        [D_MODEL], ["h", "o", "wout", "nw"], "odd_out_final")
    return (y.reshape(batch, seq, D_MODEL), ret_new[None], rwkv_new[None], shift_new[None], gdn_new[None],
            conv_new[None])


def kernel(x_prompt, x_sample, state_ret, state_rwkv, state_shift, state_gdn, state_conv, norm_e, w_in_e, rwkv_mu, rwkv_w0, rwkv_w2, rwkv_a0, rwkv_a2, rwkv_kk, rwkv_ka, rwkv_rk, rwkv_ln_w, rwkv_ln_b, ret_norm_w, w_out_e, norm_o, w_in_o, gdn_conv_w, gdn_a_log, gdn_dt_bias, gdn_norm_w, w_out_o, final_norm):
    assert state_ret.shape[0] == 1 and state_gdn.shape[0] == 1, "one even and one odd layer"
    row = lambda a: a.reshape(1, -1)
    n_qkvz = GDN_QKV + GDN_VW
    w = {
        "norm_e": row(norm_e[0]), "w_in_e": w_in_e[0].astype(BF16), "ret_norm_w": row(ret_norm_w[0]),
        "rwkv_params": (row(rwkv_mu[0]), row(rwkv_w0[0]), rwkv_w2[0], row(rwkv_a0[0]), rwkv_a2[0], row(rwkv_kk[0]),
                        row(rwkv_ka[0]), row(rwkv_rk[0]), row(rwkv_ln_w[0]), row(rwkv_ln_b[0])),
        "w_out_e": w_out_e[0].astype(BF16), "norm_o": row(norm_o[0]),
        "w_in_o": w_in_o[0][:, :n_qkvz].astype(BF16), "w_ba_o": w_in_o[0][:, n_qkvz:].astype(BF16),
        "gdn_params": (gdn_conv_w[0], row(gdn_a_log[0]), row(gdn_dt_bias[0]), row(gdn_norm_w[0])),
        "w_out_o": w_out_o[0].astype(BF16), "final_norm": row(final_norm),
    }
    batch, seq, _ = x_prompt.shape
    dec_batch, dec_seq, _ = x_sample.shape
    zeros = lambda s: jnp.zeros((batch,) + s.shape[2:], F32)
    prompt = _run_group(x_prompt, jnp.arange(seq), zeros(state_ret), zeros(state_rwkv), zeros(state_shift),
                        zeros(state_gdn), zeros(state_conv),
                        _Geometry(math.gcd(seq, PROMPT_CHUNK), 1, PROMPT_CHUNKS_PER_STEP, 1), w)
    sample = _run_group(x_sample, PAST_LEN + jnp.arange(dec_seq), state_ret[0], state_rwkv[0], state_shift[0],
                        state_gdn[0], state_conv[0], _Geometry(dec_seq, SAMPLE_SEQS_PER_STEP, 1, SAMPLE_SEQS_JOINT), w)
    return (prompt[0], sample[0]) + prompt[1:] + sample[1:]
```

```python
import functools
import math
from typing import NamedTuple

import jax
import jax.numpy as jnp
from jax import lax
from jax.experimental import pallas as pl
from jax.experimental.pallas import tpu as pltpu

F32 = jnp.float32
BF16 = jnp.bfloat16

D_MODEL = 1024
PAST_LEN = 16384
RET_HEADS, RET_DK, RET_DV = 4, 64, 128
RET_ROPE_BASE = 10000.0
RET_Q = RET_HEADS * RET_DK
RET_V = RET_HEADS * RET_DV
RET_W = 2 * RET_Q + 2 * RET_V
RWKV_HEADS, RWKV_N = 8, 64
RWKV_W = RWKV_HEADS * RWKV_N
RWKV_LORA = 64
RWKV_GN_EPS = 64e-5
SHIFT_W = 3 * RWKV_W + 2 * RWKV_LORA
RWKV_IN = SHIFT_W + RWKV_W
GDN_HEADS, GDN_DK, GDN_DV, GDN_CONV = 8, 128, 128, 4
GDN_QK = GDN_HEADS * GDN_DK
GDN_VW = GDN_HEADS * GDN_DV
GDN_QKV = 2 * GDN_QK + GDN_VW
NORM_EPS = 1e-6
L2_EPS = 1e-12

SUBLANES = 8
INV_BLOCK = 16
VMEM_LIMIT = 56 * 1024 * 1024
PROMPT_CHUNK = 64
PROMPT_CHUNKS_PER_STEP = 4
SAMPLE_SEQS_PER_STEP = 8
SAMPLE_SEQS_JOINT = 2
PREPARE_STAGES_PER_FINISH_STAGE = 3
EVEN_IN_TILE = 512
ODD_IN_TILE = 512
ODD_OUT_TILE = 1024


def _split_bf16(x):
    hi = x.astype(BF16)
    return hi, (x - hi.astype(F32)).astype(BF16)


def _dot(a, b, dims, mode):
    dot = lambda x, y: lax.dot_general(x, y, (dims, ((), ())), preferred_element_type=F32)
    if mode == "bf16":
        return dot(a.astype(BF16), b.astype(BF16))
    assert mode == "bf16x3", mode
    a_hi, a_lo = _split_bf16(a)
    b_hi, b_lo = _split_bf16(b)
    return (dot(a_lo, b_hi) + dot(a_hi, b_lo)) + dot(a_hi, b_hi)


def _mm(a, b, mode="bf16"):
    return _dot(a, b, ((1,), (0,)), mode)


def _mm_nt(a, b, mode="bf16"):
    return _dot(a, b, ((1,), (1,)), mode)


def _mm_tn(a, b, mode="bf16"):
    return _dot(a, b, ((0,), (0,)), mode)


def _mm3(a, b):
    return _mm(a, b, "bf16x3")


def _chunk_cumsum(tril, x):
    ones = tril.astype(BF16)
    hi = x.astype(BF16)
    rest = x - hi.astype(F32)
    mid = rest.astype(BF16)
    lo = (rest - mid.astype(F32)).astype(BF16)
    dot = lambda p: jnp.dot(ones, p, preferred_element_type=F32)
    return (dot(lo) + dot(mid)) + dot(hi)


def _silu(x):
    return x * jax.nn.sigmoid(x)


def _softplus(x):
    return jnp.maximum(x, 0.0) + jnp.log1p(jnp.exp(-jnp.abs(x)))


def _square_masks(n):
    ri = lax.broadcasted_iota(jnp.int32, (n, n), 0)
    ci = lax.broadcasted_iota(jnp.int32, (n, n), 1)
    return ri, ci


def _each(fn, *seqs):
    return [fn(*args) for args in zip(*seqs)]


def _neumann_inverses(lows, eye, n):
    invs = [eye - low for low in lows]
    powers = list(lows)
    k = 2
    while k < n:
        powers = _each(_mm, powers, powers)
        yield
        invs = _each(lambda inv, p: inv + _mm(inv, p), invs, powers)
        yield
        k *= 2
    return invs


def _unit_lower_inverses(lows, n):
    ri, ci = _square_masks(n)
    eye = (ri == ci).astype(F32)
    if n <= INV_BLOCK:
        invs = yield from _neumann_inverses(lows, eye, n)
    else:
        shift = int(math.log2(INV_BLOCK))
        same_block = (ri >> shift) == (ci >> shift)
        diag_invs = yield from _neumann_inverses([jnp.where(same_block, low, 0.0) for low in lows], eye, INV_BLOCK)
        offs = _each(lambda d, low: _mm(d, jnp.where(same_block, 0.0, low)), diag_invs, lows)
        yield
        off_invs = yield from _neumann_inverses(offs, eye, n // INV_BLOCK)
        invs = _each(_mm, off_invs, diag_invs)
        yield
    residuals = _each(lambda low, inv: (eye - inv) - _mm3(low, inv), lows, invs)
    yield
    refined = _each(lambda inv, res: inv + _mm(inv, res), invs, residuals)
    yield
    return refined


def _rmsnorm_rows(x, w):
    return x * lax.rsqrt(jnp.mean(x * x, -1, keepdims=True) + NORM_EPS) * w


def _even_in_kernel(x_ref, nw_ref, w_ref, ret_ref, rwkv_ref):
    xn = _rmsnorm_rows(x_ref[...], nw_ref[...]).astype(BF16)
    ret_ref[...] = jnp.dot(xn, w_ref[:, :RET_W], preferred_element_type=F32)
    rwkv_ref[...] = jnp.dot(xn, w_ref[:, RET_W:], preferred_element_type=F32)


def _even_out_odd_in_kernel(h_ref, oret_ref, orwkv_ref, wout_ref, nw_ref, win_ref, wba_ref,
                            h1_ref, qkv_ref, z_ref, ba_ref):
    mix = jnp.dot(oret_ref[...].astype(BF16), wout_ref[:RET_V, :], preferred_element_type=F32)
    mix = mix + jnp.dot(orwkv_ref[...].astype(BF16), wout_ref[RET_V:, :], preferred_element_type=F32)
    h1 = h_ref[...] + mix
    h1_ref[...] = h1
    xn = _rmsnorm_rows(h1, nw_ref[...]).astype(BF16)
    qkv_ref[...] = jnp.dot(xn, win_ref[:, :GDN_QKV], preferred_element_type=F32)
    z_ref[...] = jnp.dot(xn, win_ref[:, GDN_QKV:], preferred_element_type=F32)
    ba_ref[...] = jnp.dot(xn, wba_ref[...], preferred_element_type=F32)


def _odd_out_final_kernel(h_ref, o_ref, wout_ref, nw_ref, y_ref):
    h2 = h_ref[...] + jnp.dot(o_ref[...].astype(BF16), wout_ref[...], preferred_element_type=F32)
    y_ref[...] = _rmsnorm_rows(h2, nw_ref[...])


def _row_spec(tile, width):
    return pl.BlockSpec((tile, width), lambda i: (i, 0))


def _full_spec(shape):
    return pl.BlockSpec(shape, lambda i: (0,) * len(shape), pipeline_mode=pl.Buffered(1))


def _token_call(kernel, n_tokens, row_inputs, full_inputs, out_widths, order, name, tile):
    tile = min(tile, n_tokens)
    assert n_tokens % tile == 0
    specs = {**{k: _row_spec(tile, v.shape[1]) for k, v in row_inputs.items()},
             **{k: _full_spec(v.shape) for k, v in full_inputs.items()}}
    arrays = {**row_inputs, **full_inputs}
    return pl.pallas_call(
        kernel,
        grid=(n_tokens // tile,),
        in_specs=[specs[k] for k in order],
        out_specs=[_row_spec(tile, w) for w in out_widths],
        out_shape=[jax.ShapeDtypeStruct((n_tokens, w), F32) for w in out_widths],
        compiler_params=pltpu.CompilerParams(dimension_semantics=("parallel",), vmem_limit_bytes=VMEM_LIMIT),
        name=name,
    )(*[arrays[k] for k in order])


class _Geometry(NamedTuple):
    chunk: int
    regions: int
    chunks: int
    joint: int

    @property
    def rows(self):
        return self.regions * self.chunks * self.chunk

    def row_slice(self, region, chunk_index):
        start = (region * self.chunks + chunk_index) * self.chunk
        return slice(start, start + self.chunk)


def _recurrent_call(body, geo, batch, seq, rows, tables, states, consts, out_widths, scratch_shapes, name):
    steps = seq // (geo.chunks * geo.chunk)
    assert batch % geo.regions == 0 and steps * geo.chunks * geo.chunk == seq
    row_spec = lambda w: pl.BlockSpec((geo.rows, w), lambda b, c: (b * steps + c, 0))
    table_spec = lambda a: pl.BlockSpec((geo.rows, a.shape[1]), lambda b, c: (c, 0))
    state_spec = lambda a: pl.BlockSpec((None, geo.regions) + a.shape[2:], lambda b, c: (0, b) + (0,) * (a.ndim - 2))
    const_spec = lambda a: pl.BlockSpec(a.shape, lambda b, c: (0,) * a.ndim)
    return pl.pallas_call(
        functools.partial(body, geo=geo),
        grid=(batch // geo.regions, steps),
        in_specs=[row_spec(a.shape[1]) for a in rows] + [table_spec(a) for a in tables]
        + [state_spec(a) for a in states] + [const_spec(a) for a in consts],
        out_specs=[row_spec(w) for w in out_widths] + [state_spec(a) for a in states],
        out_shape=[jax.ShapeDtypeStruct((batch * seq, w), F32) for w in out_widths]
        + [jax.ShapeDtypeStruct(a.shape, F32) for a in states],
        scratch_shapes=scratch_shapes,
        compiler_params=pltpu.CompilerParams(dimension_semantics=("parallel", "arbitrary"),
                                             vmem_limit_bytes=VMEM_LIMIT),
        name=name,
    )(*rows, *tables, *states, *consts)


def _run_step(geo, prepare, finish):
    groups = [list(range(g, g + geo.joint)) for g in range(0, geo.regions, geo.joint)]
    items = [(group, ci) for group in groups for ci in range(geo.chunks)]

    def advance(gen, stages):
        for _ in range(stages):
            try:
                next(gen)
            except StopIteration as stop:
                return True, stop.value
        return False, None

    _, prepared = advance(prepare(*items[0]), 10 ** 6)
    for n, item in enumerate(items):
        finishing = finish(*item, prepared)
        preparing = prepare(*items[n + 1]) if n + 1 < len(items) else None
        finished, prepared_next = False, preparing is None
        prepared = None
        while not (finished and prepared_next):
            if not finished:
                finished, _ = advance(finishing, 1)
            if not prepared_next:
                prepared_next, prepared = advance(preparing, PREPARE_STAGES_PER_FINISH_STAGE)


def _ret_kernel(p_ref, cos_ref, sin_ref, s0_ref, dec_ref, qs_ref, ks_ref, ss_ref, nw_ref, o_ref, s_ref, *, geo):
    @pl.when(pl.program_id(1) == 0)
    def _():
        s_ref[...] = s0_ref[...]

    lane = lax.broadcasted_iota(jnp.int32, (geo.chunk, RET_Q), 1)
    even = (lane & 1) == 0
    heads = range(RET_HEADS)
    k_slices = [slice(h * RET_DK, (h + 1) * RET_DK) for h in heads]
    v_slices = [slice(h * RET_DV, (h + 1) * RET_DV) for h in heads]

    def rotary(x, cos, sin):
        partner = jnp.where(even, pltpu.roll(x, RET_Q - 1, 1), pltpu.roll(x, 1, 1))
        return x * cos + partner * sin

    def prepare(group, ci):
        q_l, k_l, qs_l, ks_l, v_l, dec_l, ss_l, post = [], [], [], [], [], [], [], []
        for reg in group:
            rows = geo.row_slice(reg, ci)
            p = p_ref[rows, :]
            cos, sin = cos_ref[rows, :], sin_ref[rows, :]
            q = rotary(p[:, :RET_Q], cos, sin)
            k = rotary(p[:, RET_Q:2 * RET_Q], cos, sin) * (RET_DK ** -0.5)
            q_state = q * qs_ref[...]
            k_state = k * ks_ref[...]
            for h in heads:
                q_l.append(q[:, k_slices[h]])
                k_l.append(k[:, k_slices[h]])
                qs_l.append(q_state[:, k_slices[h]])
                ks_l.append(k_state[:, k_slices[h]])
                v_l.append(p[:, 2 * RET_Q + h * RET_DV:2 * RET_Q + (h + 1) * RET_DV])
                dec_l.append(dec_ref[h])
                ss_l.append(ss_ref[h])
                post.append((rows, v_slices[h], p[:, 2 * RET_Q + RET_V + h * RET_DV:
                                                  2 * RET_Q + RET_V + (h + 1) * RET_DV]))
        scores = _each(lambda q_, k_, d: _mm_nt(q_, k_) * d, q_l, k_l, dec_l)
        yield
        intra = _each(_mm, scores, v_l)
        yield
        outer = _each(_mm_tn, ks_l, v_l)
        yield
        return qs_l, ss_l, post, intra, outer

    states = {}

    def finish(group, ci, prepared):
        qs_l, ss_l, post, intra, outer = prepared
        key = group[0]
        if ci == 0:
            states[key] = [s_ref[reg, h] for reg in group for h in heads]
        from_state = _each(_mm, qs_l, states[key])
        yield
        states[key] = _each(lambda s, ss, o: s * ss + o, states[key], ss_l, outer)
        o_l = _each(jnp.add, from_state, intra)
        ms_l = [jnp.mean(o * o, -1, keepdims=True) for o in o_l]
        for (rows, vs_, gate), o, ms in zip(post, o_l, ms_l):
            o_ref[rows, vs_] = o * lax.rsqrt(ms + NORM_EPS) * nw_ref[:, vs_] * _silu(gate)
        if ci == geo.chunks - 1:
            for i, reg in enumerate(group):
                for h in heads:
                    s_ref[reg, h] = states[key][i * RET_HEADS + h]

    _run_step(geo, prepare, finish)


def _ret_tables(pos, geo):
    chunk = geo.chunk
    half = RET_DK // 2
    inv = 1.0 / (RET_ROPE_BASE ** jnp.linspace(0.0, 1.0, half, dtype=F32))
    ang = pos.astype(F32)[:, None] * inv[None, :]
    cos = jnp.repeat(jnp.cos(ang), 2, axis=-1)
    sin = jnp.stack([-jnp.sin(ang), jnp.sin(ang)], -1).reshape(ang.shape[0], RET_DK)
    cos, sin = jnp.tile(cos, (geo.regions, RET_HEADS)), jnp.tile(sin, (geo.regions, RET_HEADS))
    lg = jnp.log1p(-jnp.exp2(-jnp.linspace(5.0, 12.0, RET_HEADS, dtype=F32)))
    ci = jnp.arange(chunk, dtype=F32)
    diff = ci[:, None] - ci[None, :]
    causal = diff >= 0
    decay = jnp.where(causal, jnp.exp(jnp.where(causal, diff, 0.0) * lg[:, None, None]), 0.0)
    q_scale = jnp.exp((ci + 1.0) * lg[:, None])
    k_scale = jnp.exp((chunk - 1.0 - ci) * lg[:, None])
    s_scale = jnp.exp(chunk * lg)
    widen = lambda t: jnp.repeat(t.T, RET_DK, axis=1)
    s_scale = jnp.broadcast_to(s_scale[:, None, None], (RET_HEADS, 1, RET_DV))
    return cos, sin, decay, widen(q_scale), widen(k_scale), s_scale


def _retention(p_flat, pos, s0, norm_w, batch, seq, geo):
    cos, sin, decay, q_scale, k_scale, s_scale = _ret_tables(pos, geo)
    return _recurrent_call(_ret_kernel, geo, batch, seq, [p_flat], [cos, sin], [s0],
                           [decay, q_scale, k_scale, s_scale, norm_w], [RET_V], [], "retention")


def _rwkv_kernel(p_ref, s0_ref, sh0_ref, mu_ref, w0_ref, w2_ref, a0_ref, a2_ref, kk_ref, ka_ref, rk_ref,
                 lnw_ref, lnb_ref, o_ref, s_ref, shout_ref, *, geo):
    chunk = geo.chunk

    @pl.when(pl.program_id(1) == 0)
    def _():
        s_ref[...] = s0_ref[...]
        shout_ref[...] = sh0_ref[...]

    ri, ci_ = _square_masks(chunk)
    tril = ri >= ci_
    strict = ri > ci_
    first_row = lax.broadcasted_iota(jnp.int32, (chunk, SHIFT_W), 0) == 0
    heads = range(RWKV_HEADS)
    slices = [slice(h * RWKV_N, (h + 1) * RWKV_N) for h in heads]
    per_head = lambda t: [t[:, s] for s in slices]
    kr, kc = _square_masks(RWKV_N)
    key_eye = kr == kc

    def prepare(group, ci):
        r_h, k_h, v_h, kk_h, b_h, cum_h, ld_h, bonus_h, post = [], [], [], [], [], [], [], [], []
        for reg in group:
            rows = geo.row_slice(reg, ci)
            p = p_ref[rows, :]
            sh = p[:, :SHIFT_W]
            before = shout_ref[reg] if ci == 0 else p_ref[rows.start - 1:rows.start, :SHIFT_W]
            prev = jnp.where(first_row, before, pltpu.roll(sh, 1, 0))
            if ci == geo.chunks - 1:
                shout_ref[reg] = sh[chunk - 1:chunk, :]
            xs = sh + (prev - sh) * mu_ref[...]
            r = xs[:, :RWKV_W]
            k_in = xs[:, RWKV_W:2 * RWKV_W]
            v = xs[:, 2 * RWKV_W:3 * RWKV_W]
            wd = xs[:, 3 * RWKV_W:3 * RWKV_W + RWKV_LORA]
            ad = xs[:, 3 * RWKV_W + RWKV_LORA:]
            w = -_softplus(-(w0_ref[...] + _mm3(jnp.tanh(wd), w2_ref[...]))) - 0.5
            log_decay = -jnp.exp(w)
            a = jax.nn.sigmoid(a0_ref[...] + _mm3(ad, a2_ref[...]))
            k = k_in * (1.0 + (a - 1.0) * ka_ref[...])
            kk = [x * lax.rsqrt(jnp.sum(x * x, -1, keepdims=True) + L2_EPS) for x in per_head(k_in * kk_ref[...])]
            r_h += per_head(r)
            k_h += per_head(k)
            v_h += per_head(v)
            kk_h += kk
            b_h += _each(jnp.multiply, kk, per_head(a))
            cum_h += per_head(_chunk_cumsum(tril, log_decay))
            ld_h += per_head(log_decay)
            bonus_h += [jnp.sum(x, -1, keepdims=True) for x in per_head(r * k * rk_ref[...])]
            post += [(rows, slices[h], p[:, SHIFT_W + h * RWKV_N:SHIFT_W + (h + 1) * RWKV_N]) for h in heads]
        yield
        cum_last = [c[chunk - 1:chunk, :] for c in cum_h]
        grow = [jnp.exp(-c) for c in cum_h]
        tail = _each(lambda cl, c: jnp.exp(cl - c), cum_last, cum_h)
        lhs = _each(lambda kk, r_, c, ld: jnp.concatenate([kk * jnp.exp(c - ld), r_ * jnp.exp(c)], axis=0),
                    kk_h, r_h, cum_h, ld_h)
        g_b = _each(lambda x, b, g: _mm_nt(x, b * g), lhs, b_h, grow)
        yield
        inverting = _unit_lower_inverses([jnp.where(strict, g[:chunk], 0.0) for g in g_b], chunk)
        g_k = _each(lambda x, k_, g: _mm_nt(x, k_ * g), lhs, k_h, grow)
        yield
        from_v = _each(lambda g, v_: _mm(jnp.where(strict, g[:chunk], 0.0), v_), g_k, v_h)
        yield
        y_v = _each(lambda g, v_: _mm(jnp.where(tril, g[chunk:], 0.0), v_), g_k, v_h)
        yield
        m_rb = [jnp.where(tril, g[chunk:], 0.0) for g in g_b]
        k_tail_t = _each(lambda k_, b, t: jnp.concatenate([k_ * t, b * t], axis=0).T, k_h, b_h, tail)
        state_decay = [jnp.sum(jnp.where(key_eye, jnp.exp(cl), 0.0), axis=1, keepdims=True) for cl in cum_last]
        inverses = yield from inverting
        return lhs, inverses, from_v, y_v, m_rb, k_tail_t, v_h, state_decay, bonus_h, post

    states = {}

    def finish(group, ci, prepared):
        lhs, inverses, from_v, y_v, m_rb, k_tail_t, v_h, state_decay, bonus_h, post = prepared
        key = group[0]
        if ci == 0:
            states[key] = [s_ref[reg, h].T for reg in group for h in heads]
        from_state = _each(_mm, lhs, states[key])
        yield
        u = _each(lambda inv, fs, fv: _mm(inv, fs[:chunk] + fv), inverses, from_state, from_v)
        yield
        outer = _each(lambda kt, v_, u_: _mm(kt, jnp.concatenate([v_, -u_], axis=0)), k_tail_t, v_h, u)
        yield
        y_u = _each(_mm, m_rb, u)
        yield
        states[key] = _each(lambda s, d, o: s * d + o, states[key], state_decay, outer)
        y_l = _each(lambda fs, yv, yu: fs[chunk:] + yv - yu, from_state, y_v, y_u)
        mean_l = [jnp.mean(y, -1, keepdims=True) for y in y_l]
        cen_l = _each(jnp.subtract, y_l, mean_l)
        var_l = [jnp.mean(jnp.square(c), -1, keepdims=True) for c in cen_l]
        for (rows, hs, gate), cen, var, bonus, v_ in zip(post, cen_l, var_l, bonus_h, v_h):
            y = cen * lax.rsqrt(var + RWKV_GN_EPS) * lnw_ref[:, hs] + lnb_ref[:, hs]
            o_ref[rows, hs] = (y + bonus * v_) * _silu(gate)
        if ci == geo.chunks - 1:
            for i, reg in enumerate(group):
                for h in heads:
                    s_ref[reg, h] = states[key][i * RWKV_HEADS + h].T

    _run_step(geo, prepare, finish)


def _rwkv(p_flat, s0, shift0, params, batch, seq, geo):
    o, s, shift = _recurrent_call(_rwkv_kernel, geo, batch, seq, [p_flat], [],
                                  [s0, shift0.reshape(1, batch, 1, SHIFT_W)], list(params), [RWKV_W], [], "rwkv7")
    return o, s, shift.reshape(1, batch, SHIFT_W)


def _gdn_kernel(qkv_ref, z_ref, ba_ref, s0_ref, c0_ref, cw_ref, alog_ref, dtb_ref, gnw_ref,
                o_ref, s_ref, cout_ref, xbuf_ref, *, geo):
    chunk = geo.chunk
    taps = GDN_CONV - 1
    region_rows = geo.chunks * chunk
    stride = region_rows + SUBLANES

    @pl.when(pl.program_id(1) == 0)
    def _():
        s_ref[...] = s0_ref[...]
        cout_ref[...] = c0_ref[...]

    ri, ci_ = _square_masks(chunk)
    tril = ri >= ci_
    strict = ri > ci_
    eye = ri == ci_
    heads = range(GDN_HEADS)
    head_slice = lambda base, h: slice(base + h * GDN_DK, base + (h + 1) * GDN_DK)
    l2 = lambda x: x * lax.rsqrt(jnp.sum(x * x, -1, keepdims=True) + L2_EPS)

    for reg in range(geo.regions):
        base = reg * stride + SUBLANES
        start = reg * region_rows
        xbuf_ref[base - taps:base, :] = cout_ref[reg]
        xbuf_ref[base:base + region_rows, :] = qkv_ref[start:start + region_rows, :]
        cout_ref[reg] = qkv_ref[start + region_rows - taps:start + region_rows, :]

    def prepare(group, ci):
        q_h, k_h, v_h, beta_h, gc, post = [], [], [], [], [], []
        for reg in group:
            rows = geo.row_slice(reg, ci)
            at = reg * stride + SUBLANES + ci * chunk
            conv = xbuf_ref[at:at + chunk, :] * cw_ref[taps:taps + 1, :]
            for j in range(taps):
                conv = conv + xbuf_ref[at - taps + j:at - taps + j + chunk, :] * cw_ref[j:j + 1, :]
            act = _silu(conv)
            ba = ba_ref[rows, :]
            beta = jax.nn.sigmoid(ba[:, :GDN_HEADS])
            g = -jnp.exp(alog_ref[...]) * _softplus(ba[:, GDN_HEADS:] + dtb_ref[...])
            gcum = _chunk_cumsum(tril, g)
            q_h += [l2(act[:, head_slice(0, h)]) * (GDN_DK ** -0.5) for h in heads]
            k_h += [l2(act[:, head_slice(GDN_QK, h)]) for h in heads]
            v_h += [act[:, head_slice(2 * GDN_QK, h)] for h in heads]
            beta_h += [beta[:, h:h + 1] for h in heads]
            gc += [gcum[:, h:h + 1] for h in heads]
            post += [(rows, head_slice(0, h)) for h in heads]
        yield
        gc_row = [jnp.sum(jnp.where(eye, g_, 0.0), axis=0, keepdims=True) for g_ in gc]
        decay = _each(lambda c, r: jnp.where(tril, jnp.exp(jnp.where(tril, c - r, 0.0)), 0.0), gc, gc_row)
        kb = _each(jnp.multiply, k_h, beta_h)
        exp_gc = [jnp.exp(g_) for g_ in gc]
        g_last = [g_[chunk - 1:chunk, :] for g_ in gc]
        lower = _each(lambda kb_, k_, d: jnp.where(strict, _mm_nt(kb_, k_) * d, 0.0), kb, k_h, decay)
        yield
        attn = _each(lambda q_, k_, d: _mm_nt(q_, k_) * d, q_h, k_h, decay)
        yield
        q_state = _each(jnp.multiply, q_h, exp_gc)
        k_tail_t = _each(lambda k_, gl, g_: (k_ * jnp.exp(gl - g_)).T, k_h, g_last, gc)
        inverses = yield from _unit_lower_inverses(lower, chunk)
        sol = _each(lambda inv, v_, b, kb_, e: _mm(inv, jnp.concatenate([v_ * b, kb_ * e], axis=1)),
                    inverses, v_h, beta_h, kb, exp_gc)
        yield
        return q_state, sol, attn, k_tail_t, [jnp.exp(gl) for gl in g_last], post

    states = {}

    def finish(group, ci, prepared):
        q_state, sol, attn, k_tail_t, state_decay, post = prepared
        key = group[0]
        if ci == 0:
            states[key] = [s_ref[reg, h] for reg in group for h in heads]
        v_new = _each(lambda s_, state: s_[:, :GDN_DV] - _mm(s_[:, GDN_DV:], state), sol, states[key])
        yield
        o_state = _each(_mm, q_state, states[key])
        yield
        outer = _each(_mm, k_tail_t, v_new)
        yield
        o_intra = _each(_mm, attn, v_new)
        yield
        states[key] = _each(lambda s, d, o: s * d + o, states[key], state_decay, outer)
        o_l = _each(jnp.add, o_state, o_intra)
        ms_l = [jnp.mean(o * o, -1, keepdims=True) for o in o_l]
        for (rows, os_), o, ms in zip(post, o_l, ms_l):
            o_ref[rows, os_] = o * lax.rsqrt(ms + NORM_EPS) * gnw_ref[...] * _silu(z_ref[rows, os_])
        if ci == geo.chunks - 1:
            for i, reg in enumerate(group):
                for h in heads:
                    s_ref[reg, h] = states[key][i * GDN_HEADS + h]

    _run_step(geo, prepare, finish)


def _gdn(qkv, z, ba, s0, conv0, params, batch, seq, geo):
    xbuf = pltpu.VMEM((geo.regions * (geo.chunks * geo.chunk + SUBLANES), GDN_QKV), F32)
    return _recurrent_call(_gdn_kernel, geo, batch, seq, [qkv, z, ba], [], [s0, conv0], list(params),
                           [GDN_VW], [xbuf], "gated_delta")


def _run_group(x, pos, s_ret, s_rwkv, s_shift, s_gdn, s_conv, geo, w):
    batch, seq, _ = x.shape
    n_tokens = batch * seq
    h0 = x.reshape(n_tokens, D_MODEL)
    p_ret, p_rwkv = _token_call(
        _even_in_kernel, n_tokens, {"x": h0}, {"nw": w["norm_e"], "w": w["w_in_e"]},
        [RET_W, RWKV_IN], ["x", "nw", "w"], "even_in", EVEN_IN_TILE)
    o_ret, ret_new = _retention(p_ret, pos, s_ret, w["ret_norm_w"], batch, seq, geo)
    o_rwkv, rwkv_new, shift_new = _rwkv(p_rwkv, s_rwkv, s_shift, w["rwkv_params"], batch, seq, geo)
    h1, qkv, z, ba = _token_call(
        _even_out_odd_in_kernel, n_tokens, {"h": h0, "oret": o_ret, "orwkv": o_rwkv},
        {"wout": w["w_out_e"], "nw": w["norm_o"], "win": w["w_in_o"], "wba": w["w_ba_o"]},
        [D_MODEL, GDN_QKV, GDN_VW, 2 * GDN_HEADS], ["h", "oret", "orwkv", "wout", "nw", "win", "wba"],
        "even_out_odd_in", ODD_IN_TILE)
    o_gdn, gdn_new, conv_new = _gdn(qkv, z, ba, s_gdn, s_conv, w["gdn_params"], batch, seq, geo)
    (y,) = _token_call(
        _odd_out_final_kernel, n_tokens, {"h": h1, "o": o_gdn}, {"wout": w["w_out_o"], "nw": w["final_norm"]},
        [D_MODEL], ["h", "o", "wout", "nw"], "odd_out_final", ODD_OUT_TILE)
    return y.reshape(batch, seq, D_MODEL), ret_new, rwkv_new, shift_new, gdn_new, conv_new


def kernel(x_prompt, x_sample, state_ret, state_rwkv, state_shift, state_gdn, state_conv, norm_e, w_in_e, rwkv_mu, rwkv_w0, rwkv_w2, rwkv_a0, rwkv_a2, rwkv_kk, rwkv_ka, rwkv_rk, rwkv_ln_w, rwkv_ln_b, ret_norm_w, w_out_e, norm_o, w_in_o, gdn_conv_w, gdn_a_log, gdn_dt_bias, gdn_norm_w, w_out_o, final_norm):
    assert state_ret.shape[0] == 1 and state_gdn.shape[0] == 1, "one even and one odd layer"
    row = lambda a: a.reshape(1, -1)
    n_qkvz = GDN_QKV + GDN_VW
    w = {
        "norm_e": row(norm_e[0]), "w_in_e": w_in_e[0].astype(BF16), "ret_norm_w": row(ret_norm_w[0]),
        "rwkv_params": (row(rwkv_mu[0]), row(rwkv_w0[0]), rwkv_w2[0], row(rwkv_a0[0]), rwkv_a2[0], row(rwkv_kk[0]),
                        row(rwkv_ka[0]), row(rwkv_rk[0]), row(rwkv_ln_w[0]), row(rwkv_ln_b[0])),
        "w_out_e": w_out_e[0].astype(BF16), "norm_o": row(norm_o[0]),
        "w_in_o": w_in_o[0][:, :n_qkvz].astype(BF16), "w_ba_o": w_in_o[0][:, n_qkvz:].astype(BF16),
        "gdn_params": (gdn_conv_w[0], row(gdn_a_log[0]), row(gdn_dt_bias[0]), row(gdn_norm_w[0])),
        "w_out_o": w_out_o[0].astype(BF16), "final_norm": row(final_norm),
    }
    batch, seq, _ = x_prompt.shape
    dec_batch, dec_seq, _ = x_sample.shape
    zeros = lambda s: jnp.zeros((1, batch) + s.shape[2:], F32)
    prompt = _run_group(x_prompt, jnp.arange(seq), zeros(state_ret), zeros(state_rwkv), zeros(state_shift),
                        zeros(state_gdn), zeros(state_conv),
                        _Geometry(math.gcd(seq, PROMPT_CHUNK), 1, PROMPT_CHUNKS_PER_STEP, 1), w)
    sample = _run_group(x_sample, PAST_LEN + jnp.arange(dec_seq), state_ret, state_rwkv, state_shift,
                        state_gdn, state_conv, _Geometry(dec_seq, SAMPLE_SEQS_PER_STEP, 1, SAMPLE_SEQS_JOINT), w)
    return (prompt[0], sample[0]) + prompt[1:] + sample[1:]
```

```python
import functools
import math
from typing import NamedTuple

import jax
import jax.numpy as jnp
from jax import lax
from jax.experimental import pallas as pl
from jax.experimental.pallas import tpu as pltpu

F32 = jnp.float32
BF16 = jnp.bfloat16

D_MODEL = 1024
PAST_LEN = 16384
RET_HEADS, RET_DK, RET_DV = 4, 64, 128
RET_ROPE_BASE = 10000.0
RET_Q = RET_HEADS * RET_DK
RET_V = RET_HEADS * RET_DV
RET_W = 2 * RET_Q + 2 * RET_V
RWKV_HEADS, RWKV_N = 8, 64
RWKV_W = RWKV_HEADS * RWKV_N
RWKV_LORA = 64
RWKV_GN_EPS = 64e-5
SHIFT_W = 3 * RWKV_W + 2 * RWKV_LORA
RWKV_IN = SHIFT_W + RWKV_W
GDN_HEADS, GDN_DK, GDN_DV, GDN_CONV = 8, 128, 128, 4
GDN_QK = GDN_HEADS * GDN_DK
GDN_VW = GDN_HEADS * GDN_DV
GDN_QKV = 2 * GDN_QK + GDN_VW
NORM_EPS = 1e-6
L2_EPS = 1e-12

SUBLANES = 8
INV_BLOCK = 16
VMEM_LIMIT = 56 * 1024 * 1024
PROMPT_CHUNK = 64
PROMPT_CHUNKS_PER_STEP = 4
SAMPLE_SEQS_PER_STEP = 8
SAMPLE_SEQS_JOINT = 2
PREPARE_STAGES_PER_FINISH_STAGE = 3
EVEN_IN_TILE = 512
ODD_IN_TILE = 512
ODD_OUT_TILE = 1024


def _split_bf16(x):
    hi = x.astype(BF16)
    return hi, (x - hi.astype(F32)).astype(BF16)


def _dot(a, b, dims, mode):
    dot = lambda x, y: lax.dot_general(x, y, (dims, ((), ())), preferred_element_type=F32)
    if mode == "bf16":
        return dot(a.astype(BF16), b.astype(BF16))
    assert mode == "bf16x3", mode
    a_hi, a_lo = _split_bf16(a)
    b_hi, b_lo = _split_bf16(b)
    return (dot(a_lo, b_hi) + dot(a_hi, b_lo)) + dot(a_hi, b_hi)


def _mm(a, b, mode="bf16"):
    return _dot(a, b, ((1,), (0,)), mode)


def _mm_nt(a, b, mode="bf16"):
    return _dot(a, b, ((1,), (1,)), mode)


def _mm_tn(a, b, mode="bf16"):
    return _dot(a, b, ((0,), (0,)), mode)


def _mm3(a, b):
    return _mm(a, b, "bf16x3")


def _chunk_cumsum(tril, x):
    ones = tril.astype(BF16)
    hi = x.astype(BF16)
    rest = x - hi.astype(F32)
    mid = rest.astype(BF16)
    lo = (rest - mid.astype(F32)).astype(BF16)
    dot = lambda p: jnp.dot(ones, p, preferred_element_type=F32)
    return (dot(lo) + dot(mid)) + dot(hi)


def _silu(x):
    return x * jax.nn.sigmoid(x)


def _softplus(x):
    return jnp.maximum(x, 0.0) + jnp.log1p(jnp.exp(-jnp.abs(x)))


def _square_masks(n):
    ri = lax.broadcasted_iota(jnp.int32, (n, n), 0)
    ci = lax.broadcasted_iota(jnp.int32, (n, n), 1)
    return ri, ci


def _each(fn, *seqs):
    return [fn(*args) for args in zip(*seqs)]


def _block_rows(x, split):
    low_half = lax.broadcasted_iota(jnp.int32, x.shape, 1) < split
    return jnp.concatenate([jnp.where(low_half, x, 0.0), jnp.where(low_half, 0.0, x)], axis=0)


def _neumann_inverses(lows, eye, n, expand):
    invs = [eye - low for low in lows]
    powers = list(lows)
    k = 2
    while k < n:
        powers = _each(lambda p: _mm(p, expand(p)), powers)
        yield
        invs = _each(lambda inv, p: inv + _mm(inv, expand(p)), invs, powers)
        yield
        k *= 2
    return invs


def _unit_lower_inverses(lows, n, width=1):
    assert width in (1, 2)
    expand = (lambda x: x) if width == 1 else (lambda x: _block_rows(x, n))
    ri = lax.broadcasted_iota(jnp.int32, (n, width * n), 0)
    ci = lax.broadcasted_iota(jnp.int32, (n, width * n), 1) & (n - 1)
    eye = (ri == ci).astype(F32)
    if n <= INV_BLOCK:
        invs = yield from _neumann_inverses(lows, eye, n, expand)
    else:
        shift = int(math.log2(INV_BLOCK))
        same_block = (ri >> shift) == (ci >> shift)
        diag_invs = yield from _neumann_inverses([jnp.where(same_block, low, 0.0) for low in lows], eye, INV_BLOCK,
                                                 expand)
        offs = _each(lambda d, low: _mm(d, expand(jnp.where(same_block, 0.0, low))), diag_invs, lows)
        yield
        off_invs = yield from _neumann_inverses(offs, eye, n // INV_BLOCK, expand)
        invs = _each(lambda o, d: _mm(o, expand(d)), off_invs, diag_invs)
        yield
    residuals = _each(lambda low, inv: (eye - inv) - _mm3(low, expand(inv)), lows, invs)
    yield
    refined = _each(lambda inv, res: inv + _mm(inv, expand(res)), invs, residuals)
    yield
    return refined


def _rmsnorm_rows(x, w):
    return x * lax.rsqrt(jnp.mean(x * x, -1, keepdims=True) + NORM_EPS) * w


def _even_in_kernel(x_ref, nw_ref, w_ref, ret_ref, rwkv_ref):
    xn = _rmsnorm_rows(x_ref[...], nw_ref[...]).astype(BF16)
    ret_ref[...] = jnp.dot(xn, w_ref[:, :RET_W], preferred_element_type=F32)
    rwkv_ref[...] = jnp.dot(xn, w_ref[:, RET_W:], preferred_element_type=F32)


def _even_out_odd_in_kernel(h_ref, oret_ref, orwkv_ref, wout_ref, nw_ref, win_ref, wba_ref,
                            h1_ref, qkv_ref, z_ref, ba_ref):
    mix = jnp.dot(oret_ref[...].astype(BF16), wout_ref[:RET_V, :], preferred_element_type=F32)
    mix = mix + jnp.dot(orwkv_ref[...].astype(BF16), wout_ref[RET_V:, :], preferred_element_type=F32)
    h1 = h_ref[...] + mix
    h1_ref[...] = h1
    xn = _rmsnorm_rows(h1, nw_ref[...]).astype(BF16)
    qkv_ref[...] = jnp.dot(xn, win_ref[:, :GDN_QKV], preferred_element_type=F32)
    z_ref[...] = jnp.dot(xn, win_ref[:, GDN_QKV:], preferred_element_type=F32)
    ba_ref[...] = jnp.dot(xn, wba_ref[...], preferred_element_type=F32)


def _odd_out_final_kernel(h_ref, o_ref, wout_ref, nw_ref, y_ref):
    h2 = h_ref[...] + jnp.dot(o_ref[...].astype(BF16), wout_ref[...], preferred_element_type=F32)
    y_ref[...] = _rmsnorm_rows(h2, nw_ref[...])


def _row_spec(tile, width):
    return pl.BlockSpec((tile, width), lambda i: (i, 0))


def _full_spec(shape):
    return pl.BlockSpec(shape, lambda i: (0,) * len(shape), pipeline_mode=pl.Buffered(1))


def _token_call(kernel, n_tokens, row_inputs, full_inputs, out_widths, order, name, tile):
    tile = min(tile, n_tokens)
    assert n_tokens % tile == 0
    specs = {**{k: _row_spec(tile, v.shape[1]) for k, v in row_inputs.items()},
             **{k: _full_spec(v.shape) for k, v in full_inputs.items()}}
    arrays = {**row_inputs, **full_inputs}
    return pl.pallas_call(
        kernel,
        grid=(n_tokens // tile,),
        in_specs=[specs[k] for k in order],
        out_specs=[_row_spec(tile, w) for w in out_widths],
        out_shape=[jax.ShapeDtypeStruct((n_tokens, w), F32) for w in out_widths],
        compiler_params=pltpu.CompilerParams(dimension_semantics=("parallel",), vmem_limit_bytes=VMEM_LIMIT),
        name=name,
    )(*[arrays[k] for k in order])


class _Geometry(NamedTuple):
    chunk: int
    regions: int
    chunks: int
    joint: int

    @property
    def rows(self):
        return self.regions * self.chunks * self.chunk

    def row_slice(self, region, chunk_index):
        start = (region * self.chunks + chunk_index) * self.chunk
        return slice(start, start + self.chunk)


def _recurrent_call(body, geo, batch, seq, rows, tables, states, consts, out_widths, scratch_shapes, name):
    steps = seq // (geo.chunks * geo.chunk)
    assert batch % geo.regions == 0 and steps * geo.chunks * geo.chunk == seq
    row_spec = lambda w: pl.BlockSpec((geo.rows, w), lambda b, c: (b * steps + c, 0))
    table_spec = lambda a: pl.BlockSpec((geo.rows, a.shape[1]), lambda b, c: (c, 0))
    state_spec = lambda a: pl.BlockSpec((None, geo.regions) + a.shape[2:], lambda b, c: (0, b) + (0,) * (a.ndim - 2))
    const_spec = lambda a: pl.BlockSpec(a.shape, lambda b, c: (0,) * a.ndim)
    return pl.pallas_call(
        functools.partial(body, geo=geo),
        grid=(batch // geo.regions, steps),
        in_specs=[row_spec(a.shape[1]) for a in rows] + [table_spec(a) for a in tables]
        + [state_spec(a) for a in states] + [const_spec(a) for a in consts],
        out_specs=[row_spec(w) for w in out_widths] + [state_spec(a) for a in states],
        out_shape=[jax.ShapeDtypeStruct((batch * seq, w), F32) for w in out_widths]
        + [jax.ShapeDtypeStruct(a.shape, F32) for a in states],
        scratch_shapes=scratch_shapes,
        compiler_params=pltpu.CompilerParams(dimension_semantics=("parallel", "arbitrary"),
                                             vmem_limit_bytes=VMEM_LIMIT),
        name=name,
    )(*rows, *tables, *states, *consts)


def _run_step(geo, prepare, finish):
    groups = [list(range(g, g + geo.joint)) for g in range(0, geo.regions, geo.joint)]
    items = [(group, ci) for group in groups for ci in range(geo.chunks)]

    def advance(gen, stages):
        for _ in range(stages):
            try:
                next(gen)
            except StopIteration as stop:
                return True, stop.value
        return False, None

    _, prepared = advance(prepare(*items[0]), 10 ** 6)
    for n, item in enumerate(items):
        finishing = finish(*item, prepared)
        preparing = prepare(*items[n + 1]) if n + 1 < len(items) else None
        finished, prepared_next = False, preparing is None
        prepared = None
        while not (finished and prepared_next):
            if not finished:
                finished, _ = advance(finishing, 1)
            if not prepared_next:
                prepared_next, prepared = advance(preparing, PREPARE_STAGES_PER_FINISH_STAGE)


def _ret_kernel(p_ref, cos_ref, sin_ref, s0_ref, dec_ref, qs_ref, ks_ref, ss_ref, nw_ref, o_ref, s_ref, *, geo):
    @pl.when(pl.program_id(1) == 0)
    def _():
        s_ref[...] = s0_ref[...]

    lane = lax.broadcasted_iota(jnp.int32, (geo.chunk, RET_Q), 1)
    even = (lane & 1) == 0
    heads = range(RET_HEADS)
    k_slices = [slice(h * RET_DK, (h + 1) * RET_DK) for h in heads]
    v_slices = [slice(h * RET_DV, (h + 1) * RET_DV) for h in heads]

    def rotary(x, cos, sin):
        partner = jnp.where(even, pltpu.roll(x, RET_Q - 1, 1), pltpu.roll(x, 1, 1))
        return x * cos + partner * sin

    def prepare(group, ci):
        q_l, k_l, qs_l, ks_l, v_l, dec_l, ss_l, post = [], [], [], [], [], [], [], []
        for reg in group:
            rows = geo.row_slice(reg, ci)
            p = p_ref[rows, :]
            cos, sin = cos_ref[rows, :], sin_ref[rows, :]
            q = rotary(p[:, :RET_Q], cos, sin)
            k = rotary(p[:, RET_Q:2 * RET_Q], cos, sin) * (RET_DK ** -0.5)
            q_state = q * qs_ref[...]
            k_state = k * ks_ref[...]
            for h in heads:
                q_l.append(q[:, k_slices[h]])
                k_l.append(k[:, k_slices[h]])
                qs_l.append(q_state[:, k_slices[h]])
                ks_l.append(k_state[:, k_slices[h]])
                v_l.append(p[:, 2 * RET_Q + h * RET_DV:2 * RET_Q + (h + 1) * RET_DV])
                dec_l.append(dec_ref[h])
                ss_l.append(ss_ref[h])
                post.append((rows, v_slices[h], p[:, 2 * RET_Q + RET_V + h * RET_DV:
                                                  2 * RET_Q + RET_V + (h + 1) * RET_DV]))
        scores = _each(lambda q_, k_, d: _mm_nt(q_, k_) * d, q_l, k_l, dec_l)
        yield
        intra = _each(_mm, scores, v_l)
        yield
        outer = _each(_mm_tn, ks_l, v_l)
        yield
        return qs_l, ss_l, post, intra, outer

    states = {}

    def finish(group, ci, prepared):
        qs_l, ss_l, post, intra, outer = prepared
        key = group[0]
        if ci == 0:
            states[key] = [s_ref[reg, h] for reg in group for h in heads]
        from_state = _each(_mm, qs_l, states[key])
        yield
        states[key] = _each(lambda s, ss, o: s * ss + o, states[key], ss_l, outer)
        o_l = _each(jnp.add, from_state, intra)
        ms_l = [jnp.mean(o * o, -1, keepdims=True) for o in o_l]
        for (rows, vs_, gate), o, ms in zip(post, o_l, ms_l):
            o_ref[rows, vs_] = o * lax.rsqrt(ms + NORM_EPS) * nw_ref[:, vs_] * _silu(gate)
        if ci == geo.chunks - 1:
            for i, reg in enumerate(group):
                for h in heads:
                    s_ref[reg, h] = states[key][i * RET_HEADS + h]

    _run_step(geo, prepare, finish)


def _ret_tables(pos, geo):
    chunk = geo.chunk
    half = RET_DK // 2
    inv = 1.0 / (RET_ROPE_BASE ** jnp.linspace(0.0, 1.0, half, dtype=F32))
    ang = pos.astype(F32)[:, None] * inv[None, :]
    cos = jnp.repeat(jnp.cos(ang), 2, axis=-1)
    sin = jnp.stack([-jnp.sin(ang), jnp.sin(ang)], -1).reshape(ang.shape[0], RET_DK)
    cos, sin = jnp.tile(cos, (geo.regions, RET_HEADS)), jnp.tile(sin, (geo.regions, RET_HEADS))
    lg = jnp.log1p(-jnp.exp2(-jnp.linspace(5.0, 12.0, RET_HEADS, dtype=F32)))
    ci = jnp.arange(chunk, dtype=F32)
    diff = ci[:, None] - ci[None, :]
    causal = diff >= 0
    decay = jnp.where(causal, jnp.exp(jnp.where(causal, diff, 0.0) * lg[:, None, None]), 0.0)
    q_scale = jnp.exp((ci + 1.0) * lg[:, None])
    k_scale = jnp.exp((chunk - 1.0 - ci) * lg[:, None])
    s_scale = jnp.exp(chunk * lg)
    widen = lambda t: jnp.repeat(t.T, RET_DK, axis=1)
    s_scale = jnp.broadcast_to(s_scale[:, None, None], (RET_HEADS, 1, RET_DV))
    return cos, sin, decay, widen(q_scale), widen(k_scale), s_scale


def _retention(p_flat, pos, s0, norm_w, batch, seq, geo):
    cos, sin, decay, q_scale, k_scale, s_scale = _ret_tables(pos, geo)
    return _recurrent_call(_ret_kernel, geo, batch, seq, [p_flat], [cos, sin], [s0],
                           [decay, q_scale, k_scale, s_scale, norm_w], [RET_V], [], "retention")


def _rwkv_kernel(p_ref, s0_ref, sh0_ref, mu_ref, w0_ref, w2_ref, a0_ref, a2_ref, kk_ref, ka_ref, rk_ref,
                 lnw_ref, lnb_ref, o_ref, s_ref, shout_ref, *, geo):
    chunk = geo.chunk
    pair_w = 2 * RWKV_N
    n_pairs = RWKV_HEADS // 2

    @pl.when(pl.program_id(1) == 0)
    def _():
        s_ref[...] = s0_ref[...]
        shout_ref[...] = sh0_ref[...]

    ri = lax.broadcasted_iota(jnp.int32, (chunk, 2 * chunk), 0)
    ci_ = lax.broadcasted_iota(jnp.int32, (chunk, 2 * chunk), 1) & (chunk - 1)
    tril = ri >= ci_
    strict = ri > ci_
    sq_r, sq_c = _square_masks(chunk)
    tril_one = sq_r >= sq_c
    first_row = lax.broadcasted_iota(jnp.int32, (chunk, SHIFT_W), 0) == 0
    head0 = lax.broadcasted_iota(jnp.int32, (chunk, pair_w), 1) < RWKV_N
    pr, pc = _square_masks(pair_w)
    pair_eye = pr == pc
    same_head = (pr < RWKV_N) == (pc < RWKV_N)
    pairs = range(n_pairs)
    lanes = [slice(p * pair_w, (p + 1) * pair_w) for p in pairs]
    per_pair = lambda t: [t[:, s] for s in lanes]
    blocks = lambda x: _block_rows(x, RWKV_N)

    def head_sums(x):
        first = jnp.sum(jnp.where(head0, x, 0.0), -1, keepdims=True)
        second = jnp.sum(jnp.where(head0, 0.0, x), -1, keepdims=True)
        return jnp.where(head0, first, second)

    def prepare(group, ci):
        r_p, k_p, v_p, kk_p, b_p, cum_p, ld_p, bonus_p, post = [], [], [], [], [], [], [], [], []
        for reg in group:
            rows = geo.row_slice(reg, ci)
            p = p_ref[rows, :]
            sh = p[:, :SHIFT_W]
            before = shout_ref[reg] if ci == 0 else p_ref[rows.start - 1:rows.start, :SHIFT_W]
            prev = jnp.where(first_row, before, pltpu.roll(sh, 1, 0))
            if ci == geo.chunks - 1:
                shout_ref[reg] = sh[chunk - 1:chunk, :]
            xs = sh + (prev - sh) * mu_ref[...]
            r = xs[:, :RWKV_W]
            k_in = xs[:, RWKV_W:2 * RWKV_W]
            v = xs[:, 2 * RWKV_W:3 * RWKV_W]
            wd = xs[:, 3 * RWKV_W:3 * RWKV_W + RWKV_LORA]
            ad = xs[:, 3 * RWKV_W + RWKV_LORA:]
            w = -_softplus(-(w0_ref[...] + _mm3(jnp.tanh(wd), w2_ref[...]))) - 0.5
            log_decay = -jnp.exp(w)
            a = jax.nn.sigmoid(a0_ref[...] + _mm3(ad, a2_ref[...]))
            k = k_in * (1.0 + (a - 1.0) * ka_ref[...])
            kk = [x * lax.rsqrt(head_sums(x * x) + L2_EPS) for x in per_pair(k_in * kk_ref[...])]
            r_p += per_pair(r)
            k_p += per_pair(k)
            v_p += per_pair(v)
            kk_p += kk
            b_p += _each(jnp.multiply, kk, per_pair(a))
            cum_p += per_pair(_chunk_cumsum(tril_one, log_decay))
            ld_p += per_pair(log_decay)
            bonus_p += [head_sums(x) for x in per_pair(r * k * rk_ref[...])]
            post += [(rows, lanes[q], p[:, SHIFT_W + q * pair_w:SHIFT_W + (q + 1) * pair_w]) for q in pairs]
        yield
        cum_last = [c[chunk - 1:chunk, :] for c in cum_p]
        grow = [jnp.exp(-c) for c in cum_p]
        tail = _each(lambda cl, c: jnp.exp(cl - c), cum_last, cum_p)
        lhs = _each(lambda kk, r_, c, ld: jnp.concatenate([kk * jnp.exp(c - ld), r_ * jnp.exp(c)], axis=0),
                    kk_p, r_p, cum_p, ld_p)
        g_b = _each(lambda x, b, g: _mm_nt(x, blocks(b * g)), lhs, b_p, grow)
        yield
        inverting = _unit_lower_inverses([jnp.where(strict, g[:chunk], 0.0) for g in g_b], chunk, width=2)
        g_k = _each(lambda x, k_, g: _mm_nt(x, blocks(k_ * g)), lhs, k_p, grow)
        yield
        v_blocks = [blocks(v_) for v_ in v_p]
        from_v = _each(lambda g, vb: _mm(jnp.where(strict, g[:chunk], 0.0), vb), g_k, v_blocks)
        yield
        y_v = _each(lambda g, vb: _mm(jnp.where(tril, g[chunk:], 0.0), vb), g_k, v_blocks)
        yield
        m_rb = [jnp.where(tril, g[chunk:], 0.0) for g in g_b]
        k_tail_t = _each(lambda k_, b, t: jnp.concatenate([k_ * t, b * t], axis=0).T, k_p, b_p, tail)
        state_decay = [jnp.sum(jnp.where(pair_eye, jnp.exp(cl), 0.0), axis=1, keepdims=True) for cl in cum_last]
        inverses = yield from inverting
        return lhs, inverses, from_v, y_v, m_rb, k_tail_t, v_p, state_decay, bonus_p, post

    states = {}

    def load_state(reg, q):
        zero = jnp.zeros((RWKV_N, RWKV_N), F32)
        return jnp.concatenate([jnp.concatenate([s_ref[reg, 2 * q].T, zero], axis=1),
                                jnp.concatenate([zero, s_ref[reg, 2 * q + 1].T], axis=1)], axis=0)

    def finish(group, ci, prepared):
        lhs, inverses, from_v, y_v, m_rb, k_tail_t, v_p, state_decay, bonus_p, post = prepared
        key = group[0]
        if ci == 0:
            states[key] = [load_state(reg, q) for reg in group for q in pairs]
        from_state = _each(_mm, lhs, states[key])
        yield
        u = _each(lambda inv, fs, fv: _mm(inv, blocks(fs[:chunk] + fv)), inverses, from_state, from_v)
        yield
        outer = _each(lambda kt, v_, u_: _mm(kt, jnp.concatenate([v_, -u_], axis=0)), k_tail_t, v_p, u)
        yield
        y_u = _each(lambda m, u_: _mm(m, blocks(u_)), m_rb, u)
        yield
        states[key] = _each(lambda s, d, o: s * d + jnp.where(same_head, o, 0.0), states[key], state_decay, outer)
        y_l = _each(lambda fs, yv, yu: fs[chunk:] + yv - yu, from_state, y_v, y_u)
        mean_l = [head_sums(y) * (1.0 / RWKV_N) for y in y_l]
        cen_l = _each(jnp.subtract, y_l, mean_l)
        var_l = [head_sums(jnp.square(c)) * (1.0 / RWKV_N) for c in cen_l]
        for (rows, ls, gate), cen, var, bonus, v_ in zip(post, cen_l, var_l, bonus_p, v_p):
            y = cen * lax.rsqrt(var + RWKV_GN_EPS) * lnw_ref[:, ls] + lnb_ref[:, ls]
            o_ref[rows, ls] = (y + bonus * v_) * _silu(gate)
        if ci == geo.chunks - 1:
            for i, reg in enumerate(group):
                for q in pairs:
                    state = states[key][i * n_pairs + q]
                    s_ref[reg, 2 * q] = state[:RWKV_N, :RWKV_N].T
                    s_ref[reg, 2 * q + 1] = state[RWKV_N:, RWKV_N:].T

    _run_step(geo, prepare, finish)


def _rwkv(p_flat, s0, shift0, params, batch, seq, geo):
    o, s, shift = _recurrent_call(_rwkv_kernel, geo, batch, seq, [p_flat], [],
                                  [s0, shift0.reshape(1, batch, 1, SHIFT_W)], list(params), [RWKV_W], [], "rwkv7")
    return o, s, shift.reshape(1, batch, SHIFT_W)


def _gdn_kernel(qkv_ref, z_ref, ba_ref, s0_ref, c0_ref, cw_ref, alog_ref, dtb_ref, gnw_ref,
                o_ref, s_ref, cout_ref, xbuf_ref, *, geo):
    chunk = geo.chunk
    taps = GDN_CONV - 1
    region_rows = geo.chunks * chunk
    stride = region_rows + SUBLANES

    @pl.when(pl.program_id(1) == 0)
    def _():
        s_ref[...] = s0_ref[...]
        cout_ref[...] = c0_ref[...]

    ri, ci_ = _square_masks(chunk)
    tril = ri >= ci_
    strict = ri > ci_
    eye = ri == ci_
    heads = range(GDN_HEADS)
    head_slice = lambda base, h: slice(base + h * GDN_DK, base + (h + 1) * GDN_DK)
    l2 = lambda x: x * lax.rsqrt(jnp.sum(x * x, -1, keepdims=True) + L2_EPS)

    for reg in range(geo.regions):
        base = reg * stride + SUBLANES
        start = reg * region_rows
        xbuf_ref[base - taps:base, :] = cout_ref[reg]
        xbuf_ref[base:base + region_rows, :] = qkv_ref[start:start + region_rows, :]
        cout_ref[reg] = qkv_ref[start + region_rows - taps:start + region_rows, :]

    def prepare(group, ci):
        q_h, k_h, v_h, beta_h, gc, post = [], [], [], [], [], []
        for reg in group:
            rows = geo.row_slice(reg, ci)
            at = reg * stride + SUBLANES + ci * chunk
            conv = xbuf_ref[at:at + chunk, :] * cw_ref[taps:taps + 1, :]
            for j in range(taps):
                conv = conv + xbuf_ref[at - taps + j:at - taps + j + chunk, :] * cw_ref[j:j + 1, :]
            act = _silu(conv)
            ba = ba_ref[rows, :]
            beta = jax.nn.sigmoid(ba[:, :GDN_HEADS])
            g = -jnp.exp(alog_ref[...]) * _softplus(ba[:, GDN_HEADS:] + dtb_ref[...])
            gcum = _chunk_cumsum(tril, g)
            q_h += [l2(act[:, head_slice(0, h)]) * (GDN_DK ** -0.5) for h in heads]
            k_h += [l2(act[:, head_slice(GDN_QK, h)]) for h in heads]
            v_h += [act[:, head_slice(2 * GDN_QK, h)] for h in heads]
            beta_h += [beta[:, h:h + 1] for h in heads]
            gc += [gcum[:, h:h + 1] for h in heads]
            post += [(rows, head_slice(0, h)) for h in heads]
        yield
        gc_row = [jnp.sum(jnp.where(eye, g_, 0.0), axis=0, keepdims=True) for g_ in gc]
        decay = _each(lambda c, r: jnp.where(tril, jnp.exp(jnp.where(tril, c - r, 0.0)), 0.0), gc, gc_row)
        kb = _each(jnp.multiply, k_h, beta_h)
        exp_gc = [jnp.exp(g_) for g_ in gc]
        g_last = [g_[chunk - 1:chunk, :] for g_ in gc]
        lower = _each(lambda kb_, k_, d: jnp.where(strict, _mm_nt(kb_, k_) * d, 0.0), kb, k_h, decay)
        yield
        attn = _each(lambda q_, k_, d: _mm_nt(q_, k_) * d, q_h, k_h, decay)
        yield
        q_state = _each(jnp.multiply, q_h, exp_gc)
        k_tail_t = _each(lambda k_, gl, g_: (k_ * jnp.exp(gl - g_)).T, k_h, g_last, gc)
        inverses = yield from _unit_lower_inverses(lower, chunk)
        sol = _each(lambda inv, v_, b, kb_, e: _mm(inv, jnp.concatenate([v_ * b, kb_ * e], axis=1)),
                    inverses, v_h, beta_h, kb, exp_gc)
        yield
        return q_state, sol, attn, k_tail_t, [jnp.exp(gl) for gl in g_last], post

    states = {}

    def finish(group, ci, prepared):
        q_state, sol, attn, k_tail_t, state_decay, post = prepared
        key = group[0]
        if ci == 0:
            states[key] = [s_ref[reg, h] for reg in group for h in heads]
        v_new = _each(lambda s_, state: s_[:, :GDN_DV] - _mm(s_[:, GDN_DV:], state), sol, states[key])
        yield
        o_state = _each(_mm, q_state, states[key])
        yield
        outer = _each(_mm, k_tail_t, v_new)
        yield
        o_intra = _each(_mm, attn, v_new)
        yield
        states[key] = _each(lambda s, d, o: s * d + o, states[key], state_decay, outer)
        o_l = _each(jnp.add, o_state, o_intra)
        ms_l = [jnp.mean(o * o, -1, keepdims=True) for o in o_l]
        for (rows, os_), o, ms in zip(post, o_l, ms_l):
            o_ref[rows, os_] = o * lax.rsqrt(ms + NORM_EPS) * gnw_ref[...] * _silu(z_ref[rows, os_])
        if ci == geo.chunks - 1:
            for i, reg in enumerate(group):
                for h in heads:
                    s_ref[reg, h] = states[key][i * GDN_HEADS + h]

    _run_step(geo, prepare, finish)


def _gdn(qkv, z, ba, s0, conv0, params, batch, seq, geo):
    xbuf = pltpu.VMEM((geo.regions * (geo.chunks * geo.chunk + SUBLANES), GDN_QKV), F32)
    return _recurrent_call(_gdn_kernel, geo, batch, seq, [qkv, z, ba], [], [s0, conv0], list(params),
                           [GDN_VW], [xbuf], "gated_delta")


def _run_group(x, pos, s_ret, s_rwkv, s_shift, s_gdn, s_conv, geo, w):
    batch, seq, _ = x.shape
    n_tokens = batch * seq
    h0 = x.reshape(n_tokens, D_MODEL)
    p_ret, p_rwkv = _token_call(
        _even_in_kernel, n_tokens, {"x": h0}, {"nw": w["norm_e"], "w": w["w_in_e"]},
        [RET_W, RWKV_IN], ["x", "nw", "w"], "even_in", EVEN_IN_TILE)
    o_ret, ret_new = _retention(p_ret, pos, s_ret, w["ret_norm_w"], batch, seq, geo)
    o_rwkv, rwkv_new, shift_new = _rwkv(p_rwkv, s_rwkv, s_shift, w["rwkv_params"], batch, seq, geo)
    h1, qkv, z, ba = _token_call(
        _even_out_odd_in_kernel, n_tokens, {"h": h0, "oret": o_ret, "orwkv": o_rwkv},
        {"wout": w["w_out_e"], "nw": w["norm_o"], "win": w["w_in_o"], "wba": w["w_ba_o"]},
        [D_MODEL, GDN_QKV, GDN_VW, 2 * GDN_HEADS], ["h", "oret", "orwkv", "wout", "nw", "win", "wba"],
        "even_out_odd_in", ODD_IN_TILE)
    o_gdn, gdn_new, conv_new = _gdn(qkv, z, ba, s_gdn, s_conv, w["gdn_params"], batch, seq, geo)
    (y,) = _token_call(
        _odd_out_final_kernel, n_tokens, {"h": h1, "o": o_gdn}, {"wout": w["w_out_o"], "nw": w["final_norm"]},
        [D_MODEL], ["h", "o", "wout", "nw"], "odd_out_final", ODD_OUT_TILE)
    return y.reshape(batch, seq, D_MODEL), ret_new, rwkv_new, shift_new, gdn_new, conv_new


def kernel(x_prompt, x_sample, state_ret, state_rwkv, state_shift, state_gdn, state_conv, norm_e, w_in_e, rwkv_mu, rwkv_w0, rwkv_w2, rwkv_a0, rwkv_a2, rwkv_kk, rwkv_ka, rwkv_rk, rwkv_ln_w, rwkv_ln_b, ret_norm_w, w_out_e, norm_o, w_in_o, gdn_conv_w, gdn_a_log, gdn_dt_bias, gdn_norm_w, w_out_o, final_norm):
    assert state_ret.shape[0] == 1 and state_gdn.shape[0] == 1, "one even and one odd layer"
    row = lambda a: a.reshape(1, -1)
    n_qkvz = GDN_QKV + GDN_VW
    w = {
        "norm_e": row(norm_e[0]), "w_in_e": w_in_e[0].astype(BF16), "ret_norm_w": row(ret_norm_w[0]),
        "rwkv_params": (row(rwkv_mu[0]), row(rwkv_w0[0]), rwkv_w2[0], row(rwkv_a0[0]), rwkv_a2[0], row(rwkv_kk[0]),
                        row(rwkv_ka[0]), row(rwkv_rk[0]), row(rwkv_ln_w[0]), row(rwkv_ln_b[0])),
        "w_out_e": w_out_e[0].astype(BF16), "norm_o": row(norm_o[0]),
        "w_in_o": w_in_o[0][:, :n_qkvz].astype(BF16), "w_ba_o": w_in_o[0][:, n_qkvz:].astype(BF16),
        "gdn_params": (gdn_conv_w[0], row(gdn_a_log[0]), row(gdn_dt_bias[0]), row(gdn_norm_w[0])),
        "w_out_o": w_out_o[0].astype(BF16), "final_norm": row(final_norm),
    }
    batch, seq, _ = x_prompt.shape
    dec_batch, dec_seq, _ = x_sample.shape
    zeros = lambda s: jnp.zeros((1, batch) + s.shape[2:], F32)
    prompt = _run_group(x_prompt, jnp.arange(seq), zeros(state_ret), zeros(state_rwkv), zeros(state_shift),
                        zeros(state_gdn), zeros(state_conv),
                        _Geometry(math.gcd(seq, PROMPT_CHUNK), 1, PROMPT_CHUNKS_PER_STEP, 1), w)
    sample = _run_group(x_sample, PAST_LEN + jnp.arange(dec_seq), state_ret, state_rwkv, state_shift,
                        state_gdn, state_conv, _Geometry(dec_seq, SAMPLE_SEQS_PER_STEP, 1, SAMPLE_SEQS_JOINT), w)
    return (prompt[0], sample[0]) + prompt[1:] + sample[1:]
```

```python
import functools
import math
from typing import NamedTuple

import jax
import jax.numpy as jnp
from jax import lax
from jax.experimental import pallas as pl
from jax.experimental.pallas import tpu as pltpu

F32 = jnp.float32
BF16 = jnp.bfloat16

D_MODEL = 1024
PAST_LEN = 16384
RET_HEADS, RET_DK, RET_DV = 4, 64, 128
RET_ROPE_BASE = 10000.0
RET_Q = RET_HEADS * RET_DK
RET_V = RET_HEADS * RET_DV
RET_W = 2 * RET_Q + 2 * RET_V
RWKV_HEADS, RWKV_N = 8, 64
RWKV_W = RWKV_HEADS * RWKV_N
RWKV_LORA = 64
RWKV_GN_EPS = 64e-5
SHIFT_W = 3 * RWKV_W + 2 * RWKV_LORA
RWKV_IN = SHIFT_W + RWKV_W
GDN_HEADS, GDN_DK, GDN_DV, GDN_CONV = 8, 128, 128, 4
GDN_QK = GDN_HEADS * GDN_DK
GDN_VW = GDN_HEADS * GDN_DV
GDN_QKV = 2 * GDN_QK + GDN_VW
NORM_EPS = 1e-6
L2_EPS = 1e-12

SUBLANES = 8
INV_BLOCK = 16
VMEM_LIMIT = 56 * 1024 * 1024
PROMPT_CHUNK = 64
PROMPT_CHUNKS_PER_STEP = 2
PROMPT_SEQS_PER_STEP = 4
SAMPLE_SEQS_PER_STEP = 8
SAMPLE_SEQS_JOINT = 2
PREPARE_STAGES_PER_FINISH_STAGE = 3
EVEN_IN_TILE = 512
ODD_IN_TILE = 512
ODD_OUT_TILE = 1024


def _split_bf16(x):
    hi = x.astype(BF16)
    return hi, (x - hi.astype(F32)).astype(BF16)


def _dot(a, b, dims, mode):
    dot = lambda x, y: lax.dot_general(x, y, (dims, ((), ())), preferred_element_type=F32)
    if mode == "bf16":
        return dot(a.astype(BF16), b.astype(BF16))
    assert mode == "bf16x3", mode
    a_hi, a_lo = _split_bf16(a)
    b_hi, b_lo = _split_bf16(b)
    return (dot(a_lo, b_hi) + dot(a_hi, b_lo)) + dot(a_hi, b_hi)


def _mm(a, b, mode="bf16"):
    return _dot(a, b, ((1,), (0,)), mode)


def _mm_nt(a, b, mode="bf16"):
    return _dot(a, b, ((1,), (1,)), mode)


def _mm_tn(a, b, mode="bf16"):
    return _dot(a, b, ((0,), (0,)), mode)


def _mm3(a, b):
    return _mm(a, b, "bf16x3")


def _chunk_cumsum(tril, x):
    ones = tril.astype(BF16)
    hi = x.astype(BF16)
    rest = x - hi.astype(F32)
    mid = rest.astype(BF16)
    lo = (rest - mid.astype(F32)).astype(BF16)
    dot = lambda p: jnp.dot(ones, p, preferred_element_type=F32)
    return (dot(lo) + dot(mid)) + dot(hi)


def _silu(x):
    return x * jax.nn.sigmoid(x)


def _softplus(x):
    return jnp.maximum(x, 0.0) + jnp.log1p(jnp.exp(-jnp.abs(x)))


def _square_masks(n):
    ri = lax.broadcasted_iota(jnp.int32, (n, n), 0)
    ci = lax.broadcasted_iota(jnp.int32, (n, n), 1)
    return ri, ci


def _each(fn, *seqs):
    return [fn(*args) for args in zip(*seqs)]


def _block_rows(x, split):
    low_half = lax.broadcasted_iota(jnp.int32, x.shape, 1) < split
    return jnp.concatenate([jnp.where(low_half, x, 0.0), jnp.where(low_half, 0.0, x)], axis=0)


def _neumann_inverses(lows, eye, n, expand):
    invs = [eye - low for low in lows]
    powers = list(lows)
    k = 2
    while k < n:
        powers = _each(lambda p: _mm(p, expand(p)), powers)
        yield
        invs = _each(lambda inv, p: inv + _mm(inv, expand(p)), invs, powers)
        yield
        k *= 2
    return invs


def _unit_lower_inverses(lows, n, width=1):
    assert width in (1, 2)
    expand = (lambda x: x) if width == 1 else (lambda x: _block_rows(x, n))
    ri = lax.broadcasted_iota(jnp.int32, (n, width * n), 0)
    ci = lax.broadcasted_iota(jnp.int32, (n, width * n), 1) & (n - 1)
    eye = (ri == ci).astype(F32)
    if n <= INV_BLOCK:
        invs = yield from _neumann_inverses(lows, eye, n, expand)
    else:
        shift = int(math.log2(INV_BLOCK))
        same_block = (ri >> shift) == (ci >> shift)
        diag_invs = yield from _neumann_inverses([jnp.where(same_block, low, 0.0) for low in lows], eye, INV_BLOCK,
                                                 expand)
        offs = _each(lambda d, low: _mm(d, expand(jnp.where(same_block, 0.0, low))), diag_invs, lows)
        yield
        off_invs = yield from _neumann_inverses(offs, eye, n // INV_BLOCK, expand)
        invs = _each(lambda o, d: _mm(o, expand(d)), off_invs, diag_invs)
        yield
    residuals = _each(lambda low, inv: (eye - inv) - _mm3(low, expand(inv)), lows, invs)
    yield
    refined = _each(lambda inv, res: inv + _mm(inv, expand(res)), invs, residuals)
    yield
    return refined


def _rmsnorm_rows(x, w):
    return x * lax.rsqrt(jnp.mean(x * x, -1, keepdims=True) + NORM_EPS) * w


def _even_in_kernel(x_ref, nw_ref, w_ref, ret_ref, rwkv_ref):
    xn = _rmsnorm_rows(x_ref[...], nw_ref[...]).astype(BF16)
    ret_ref[...] = jnp.dot(xn, w_ref[:, :RET_W], preferred_element_type=F32)
    rwkv_ref[...] = jnp.dot(xn, w_ref[:, RET_W:], preferred_element_type=F32)


def _even_out_odd_in_kernel(h_ref, oret_ref, orwkv_ref, wout_ref, nw_ref, win_ref, wba_ref,
                            h1_ref, qkv_ref, z_ref, ba_ref):
    mix = jnp.dot(oret_ref[...].astype(BF16), wout_ref[:RET_V, :], preferred_element_type=F32)
    mix = mix + jnp.dot(orwkv_ref[...].astype(BF16), wout_ref[RET_V:, :], preferred_element_type=F32)
    h1 = h_ref[...] + mix
    h1_ref[...] = h1
    xn = _rmsnorm_rows(h1, nw_ref[...]).astype(BF16)
    qkv_ref[...] = jnp.dot(xn, win_ref[:, :GDN_QKV], preferred_element_type=F32)
    z_ref[...] = jnp.dot(xn, win_ref[:, GDN_QKV:], preferred_element_type=F32)
    ba_ref[...] = jnp.dot(xn, wba_ref[...], preferred_element_type=F32)


def _odd_out_final_kernel(h_ref, o_ref, wout_ref, nw_ref, y_ref):
    h2 = h_ref[...] + jnp.dot(o_ref[...].astype(BF16), wout_ref[...], preferred_element_type=F32)
    y_ref[...] = _rmsnorm_rows(h2, nw_ref[...])


def _row_spec(tile, width):
    return pl.BlockSpec((tile, width), lambda i: (i, 0))


def _full_spec(shape):
    return pl.BlockSpec(shape, lambda i: (0,) * len(shape), pipeline_mode=pl.Buffered(1))


def _token_call(kernel, n_tokens, row_inputs, full_inputs, out_widths, order, name, tile):
    tile = min(tile, n_tokens)
    assert n_tokens % tile == 0
    specs = {**{k: _row_spec(tile, v.shape[1]) for k, v in row_inputs.items()},
             **{k: _full_spec(v.shape) for k, v in full_inputs.items()}}
    arrays = {**row_inputs, **full_inputs}
    return pl.pallas_call(
        kernel,
        grid=(n_tokens // tile,),
        in_specs=[specs[k] for k in order],
        out_specs=[_row_spec(tile, w) for w in out_widths],
        out_shape=[jax.ShapeDtypeStruct((n_tokens, w), F32) for w in out_widths],
        compiler_params=pltpu.CompilerParams(dimension_semantics=("parallel",), vmem_limit_bytes=VMEM_LIMIT),
        name=name,
    )(*[arrays[k] for k in order])


class _Geometry(NamedTuple):
    chunk: int
    regions: int
    chunks: int
    joint: int

    @property
    def region_rows(self):
        return self.chunks * self.chunk

    def rows(self, region, chunk_index, lanes=slice(None)):
        return region, slice(chunk_index * self.chunk, (chunk_index + 1) * self.chunk), lanes


def _recurrent_call(body, geo, batch, seq, rows, tables, states, consts, out_widths, scratch_shapes, name):
    steps = seq // geo.region_rows
    assert batch % geo.regions == 0 and steps * geo.region_rows == seq
    row_spec = lambda w: pl.BlockSpec((geo.regions, geo.region_rows, w), lambda b, c: (b, c, 0))
    table_spec = lambda a: pl.BlockSpec((geo.region_rows, a.shape[1]), lambda b, c: (c, 0))
    state_spec = lambda a: pl.BlockSpec((None, geo.regions) + a.shape[2:], lambda b, c: (0, b) + (0,) * (a.ndim - 2))
    const_spec = lambda a: pl.BlockSpec(a.shape, lambda b, c: (0,) * a.ndim)
    outs = pl.pallas_call(
        functools.partial(body, geo=geo),
        grid=(batch // geo.regions, steps),
        in_specs=[row_spec(a.shape[1]) for a in rows] + [table_spec(a) for a in tables]
        + [state_spec(a) for a in states] + [const_spec(a) for a in consts],
        out_specs=[row_spec(w) for w in out_widths] + [state_spec(a) for a in states],
        out_shape=[jax.ShapeDtypeStruct((batch, seq, w), F32) for w in out_widths]
        + [jax.ShapeDtypeStruct(a.shape, F32) for a in states],
        scratch_shapes=scratch_shapes,
        compiler_params=pltpu.CompilerParams(dimension_semantics=("parallel", "arbitrary"),
                                             vmem_limit_bytes=VMEM_LIMIT),
        name=name,
    )(*[a.reshape(batch, seq, a.shape[1]) for a in rows], *tables, *states, *consts)
    return [o.reshape(batch * seq, o.shape[2]) for o in outs[:len(out_widths)]] + list(outs[len(out_widths):])


def _run_step(geo, prepare, finish):
    groups = [list(range(g, g + geo.joint)) for g in range(0, geo.regions, geo.joint)]
    items = [(group, ci) for group in groups for ci in range(geo.chunks)]

    def advance(gen, stages):
        for _ in range(stages):
            try:
                next(gen)
            except StopIteration as stop:
                return True, stop.value
        return False, None

    _, prepared = advance(prepare(*items[0]), 10 ** 6)
    for n, item in enumerate(items):
        finishing = finish(*item, prepared)
        preparing = prepare(*items[n + 1]) if n + 1 < len(items) else None
        finished, prepared_next = False, preparing is None
        prepared = None
        while not (finished and prepared_next):
            if not finished:
                finished, _ = advance(finishing, 1)
            if not prepared_next:
                prepared_next, prepared = advance(preparing, PREPARE_STAGES_PER_FINISH_STAGE)


def _ret_kernel(p_ref, cos_ref, sin_ref, s0_ref, dec_ref, qs_ref, ks_ref, ss_ref, nw_ref, o_ref, s_ref, *, geo):
    @pl.when(pl.program_id(1) == 0)
    def _():
        s_ref[...] = s0_ref[...]

    lane = lax.broadcasted_iota(jnp.int32, (geo.chunk, RET_Q), 1)
    even = (lane & 1) == 0
    heads = range(RET_HEADS)
    k_slices = [slice(h * RET_DK, (h + 1) * RET_DK) for h in heads]
    v_slices = [slice(h * RET_DV, (h + 1) * RET_DV) for h in heads]

    def rotary(x, cos, sin):
        partner = jnp.where(even, pltpu.roll(x, RET_Q - 1, 1), pltpu.roll(x, 1, 1))
        return x * cos + partner * sin

    def prepare(group, ci):
        q_l, k_l, qs_l, ks_l, v_l, dec_l, ss_l, post = [], [], [], [], [], [], [], []
        for reg in group:
            rows = geo.rows(reg, ci)
            p = p_ref[rows]
            cos, sin = cos_ref[rows[1], :], sin_ref[rows[1], :]
            q = rotary(p[:, :RET_Q], cos, sin)
            k = rotary(p[:, RET_Q:2 * RET_Q], cos, sin) * (RET_DK ** -0.5)
            q_state = q * qs_ref[...]
            k_state = k * ks_ref[...]
            for h in heads:
                q_l.append(q[:, k_slices[h]])
                k_l.append(k[:, k_slices[h]])
                qs_l.append(q_state[:, k_slices[h]])
                ks_l.append(k_state[:, k_slices[h]])
                v_l.append(p[:, 2 * RET_Q + h * RET_DV:2 * RET_Q + (h + 1) * RET_DV])
                dec_l.append(dec_ref[h])
                ss_l.append(ss_ref[h])
                post.append((rows, v_slices[h], p[:, 2 * RET_Q + RET_V + h * RET_DV:
                                                  2 * RET_Q + RET_V + (h + 1) * RET_DV]))
        scores = _each(lambda q_, k_, d: _mm_nt(q_, k_) * d, q_l, k_l, dec_l)
        yield
        intra = _each(_mm, scores, v_l)
        yield
        outer = _each(_mm_tn, ks_l, v_l)
        yield
        return qs_l, ss_l, post, intra, outer

    states = {}

    def finish(group, ci, prepared):
        qs_l, ss_l, post, intra, outer = prepared
        key = group[0]
        if ci == 0:
            states[key] = [s_ref[reg, h] for reg in group for h in heads]
        from_state = _each(_mm, qs_l, states[key])
        yield
        states[key] = _each(lambda s, ss, o: s * ss + o, states[key], ss_l, outer)
        o_l = _each(jnp.add, from_state, intra)
        ms_l = [jnp.mean(o * o, -1, keepdims=True) for o in o_l]
        for (rows, vs_, gate), o, ms in zip(post, o_l, ms_l):
            o_ref[rows[0], rows[1], vs_] = o * lax.rsqrt(ms + NORM_EPS) * nw_ref[:, vs_] * _silu(gate)
        if ci == geo.chunks - 1:
            for i, reg in enumerate(group):
                for h in heads:
                    s_ref[reg, h] = states[key][i * RET_HEADS + h]

    _run_step(geo, prepare, finish)


def _ret_tables(pos, geo):
    chunk = geo.chunk
    half = RET_DK // 2
    inv = 1.0 / (RET_ROPE_BASE ** jnp.linspace(0.0, 1.0, half, dtype=F32))
    ang = pos.astype(F32)[:, None] * inv[None, :]
    cos = jnp.repeat(jnp.cos(ang), 2, axis=-1)
    sin = jnp.stack([-jnp.sin(ang), jnp.sin(ang)], -1).reshape(ang.shape[0], RET_DK)
    cos, sin = jnp.tile(cos, (1, RET_HEADS)), jnp.tile(sin, (1, RET_HEADS))
    lg = jnp.log1p(-jnp.exp2(-jnp.linspace(5.0, 12.0, RET_HEADS, dtype=F32)))
    ci = jnp.arange(chunk, dtype=F32)
    diff = ci[:, None] - ci[None, :]
    causal = diff >= 0
    decay = jnp.where(causal, jnp.exp(jnp.where(causal, diff, 0.0) * lg[:, None, None]), 0.0)
    q_scale = jnp.exp((ci + 1.0) * lg[:, None])
    k_scale = jnp.exp((chunk - 1.0 - ci) * lg[:, None])
    s_scale = jnp.exp(chunk * lg)
    widen = lambda t: jnp.repeat(t.T, RET_DK, axis=1)
    s_scale = jnp.broadcast_to(s_scale[:, None, None], (RET_HEADS, 1, RET_DV))
    return cos, sin, decay, widen(q_scale), widen(k_scale), s_scale


def _retention(p_flat, pos, s0, norm_w, batch, seq, geo):
    cos, sin, decay, q_scale, k_scale, s_scale = _ret_tables(pos, geo)
    return _recurrent_call(_ret_kernel, geo, batch, seq, [p_flat], [cos, sin], [s0],
                           [decay, q_scale, k_scale, s_scale, norm_w], [RET_V], [], "retention")


def _rwkv_kernel(p_ref, s0_ref, sh0_ref, mu_ref, w0_ref, w2_ref, a0_ref, a2_ref, kk_ref, ka_ref, rk_ref,
                 lnw_ref, lnb_ref, o_ref, s_ref, shout_ref, *, geo):
    chunk = geo.chunk
    pair_w = 2 * RWKV_N
    n_pairs = RWKV_HEADS // 2

    @pl.when(pl.program_id(1) == 0)
    def _():
        s_ref[...] = s0_ref[...]
        shout_ref[...] = sh0_ref[...]

    ri = lax.broadcasted_iota(jnp.int32, (chunk, 2 * chunk), 0)
    ci_ = lax.broadcasted_iota(jnp.int32, (chunk, 2 * chunk), 1) & (chunk - 1)
    tril = ri >= ci_
    strict = ri > ci_
    sq_r, sq_c = _square_masks(chunk)
    tril_one = sq_r >= sq_c
    first_row = lax.broadcasted_iota(jnp.int32, (chunk, SHIFT_W), 0) == 0
    head0 = lax.broadcasted_iota(jnp.int32, (chunk, pair_w), 1) < RWKV_N
    pr, pc = _square_masks(pair_w)
    pair_eye = pr == pc
    same_head = (pr < RWKV_N) == (pc < RWKV_N)
    pairs = range(n_pairs)
    lanes = [slice(p * pair_w, (p + 1) * pair_w) for p in pairs]
    per_pair = lambda t: [t[:, s] for s in lanes]
    blocks = lambda x: _block_rows(x, RWKV_N)

    def head_sums(x):
        first = jnp.sum(jnp.where(head0, x, 0.0), -1, keepdims=True)
        second = jnp.sum(jnp.where(head0, 0.0, x), -1, keepdims=True)
        return jnp.where(head0, first, second)

    def prepare(group, ci):
        r_p, k_p, v_p, kk_p, b_p, cum_p, ld_p, bonus_p, post = [], [], [], [], [], [], [], [], []
        for reg in group:
            rows = geo.rows(reg, ci)
            p = p_ref[rows]
            sh = p[:, :SHIFT_W]
            before = shout_ref[reg] if ci == 0 else p_ref[reg, rows[1].start - 1:rows[1].start, :SHIFT_W]
            prev = jnp.where(first_row, before, pltpu.roll(sh, 1, 0))
            if ci == geo.chunks - 1:
                shout_ref[reg] = sh[chunk - 1:chunk, :]
            xs = sh + (prev - sh) * mu_ref[...]
            r = xs[:, :RWKV_W]
            k_in = xs[:, RWKV_W:2 * RWKV_W]
            v = xs[:, 2 * RWKV_W:3 * RWKV_W]
            wd = xs[:, 3 * RWKV_W:3 * RWKV_W + RWKV_LORA]
            ad = xs[:, 3 * RWKV_W + RWKV_LORA:]
            w = -_softplus(-(w0_ref[...] + _mm3(jnp.tanh(wd), w2_ref[...]))) - 0.5
            log_decay = -jnp.exp(w)
            a = jax.nn.sigmoid(a0_ref[...] + _mm3(ad, a2_ref[...]))
            k = k_in * (1.0 + (a - 1.0) * ka_ref[...])
            kk = [x * lax.rsqrt(head_sums(x * x) + L2_EPS) for x in per_pair(k_in * kk_ref[...])]
            r_p += per_pair(r)
            k_p += per_pair(k)
            v_p += per_pair(v)
            kk_p += kk
            b_p += _each(jnp.multiply, kk, per_pair(a))
            cum_p += per_pair(_chunk_cumsum(tril_one, log_decay))
            ld_p += per_pair(log_decay)
            bonus_p += [head_sums(x) for x in per_pair(r * k * rk_ref[...])]
            post += [(rows, lanes[q], p[:, SHIFT_W + q * pair_w:SHIFT_W + (q + 1) * pair_w]) for q in pairs]
        yield
        cum_last = [c[chunk - 1:chunk, :] for c in cum_p]
        grow = [jnp.exp(-c) for c in cum_p]
        tail = _each(lambda cl, c: jnp.exp(cl - c), cum_last, cum_p)
        lhs = _each(lambda kk, r_, c, ld: jnp.concatenate([kk * jnp.exp(c - ld), r_ * jnp.exp(c)], axis=0),
                    kk_p, r_p, cum_p, ld_p)
        g_b = _each(lambda x, b, g: _mm_nt(x, blocks(b * g)), lhs, b_p, grow)
        yield
        inverting = _unit_lower_inverses([jnp.where(strict, g[:chunk], 0.0) for g in g_b], chunk, width=2)
        g_k = _each(lambda x, k_, g: _mm_nt(x, blocks(k_ * g)), lhs, k_p, grow)
        yield
        v_blocks = [blocks(v_) for v_ in v_p]
        from_v = _each(lambda g, vb: _mm(jnp.where(strict, g[:chunk], 0.0), vb), g_k, v_blocks)
        yield
        y_v = _each(lambda g, vb: _mm(jnp.where(tril, g[chunk:], 0.0), vb), g_k, v_blocks)
        yield
        m_rb = [jnp.where(tril, g[chunk:], 0.0) for g in g_b]
        k_tail_t = _each(lambda k_, b, t: jnp.concatenate([k_ * t, b * t], axis=0).T, k_p, b_p, tail)
        state_decay = [jnp.sum(jnp.where(pair_eye, jnp.exp(cl), 0.0), axis=1, keepdims=True) for cl in cum_last]
        inverses = yield from inverting
        return lhs, inverses, from_v, y_v, m_rb, k_tail_t, v_p, state_decay, bonus_p, post

    states = {}

    def load_state(reg, q):
        zero = jnp.zeros((RWKV_N, RWKV_N), F32)
        return jnp.concatenate([jnp.concatenate([s_ref[reg, 2 * q].T, zero], axis=1),
                                jnp.concatenate([zero, s_ref[reg, 2 * q + 1].T], axis=1)], axis=0)

    def finish(group, ci, prepared):
        lhs, inverses, from_v, y_v, m_rb, k_tail_t, v_p, state_decay, bonus_p, post = prepared
        key = group[0]
        if ci == 0:
            states[key] = [load_state(reg, q) for reg in group for q in pairs]
        from_state = _each(_mm, lhs, states[key])
        yield
        u = _each(lambda inv, fs, fv: _mm(inv, blocks(fs[:chunk] + fv)), inverses, from_state, from_v)
        yield
        outer = _each(lambda kt, v_, u_: _mm(kt, jnp.concatenate([v_, -u_], axis=0)), k_tail_t, v_p, u)
        yield
        y_u = _each(lambda m, u_: _mm(m, blocks(u_)), m_rb, u)
        yield
        states[key] = _each(lambda s, d, o: s * d + jnp.where(same_head, o, 0.0), states[key], state_decay, outer)
        y_l = _each(lambda fs, yv, yu: fs[chunk:] + yv - yu, from_state, y_v, y_u)
        mean_l = [head_sums(y) * (1.0 / RWKV_N) for y in y_l]
        cen_l = _each(jnp.subtract, y_l, mean_l)
        var_l = [head_sums(jnp.square(c)) * (1.0 / RWKV_N) for c in cen_l]
        for (rows, ls, gate), cen, var, bonus, v_ in zip(post, cen_l, var_l, bonus_p, v_p):
            y = cen * lax.rsqrt(var + RWKV_GN_EPS) * lnw_ref[:, ls] + lnb_ref[:, ls]
            o_ref[rows[0], rows[1], ls] = (y + bonus * v_) * _silu(gate)
        if ci == geo.chunks - 1:
            for i, reg in enumerate(group):
                for q in pairs:
                    state = states[key][i * n_pairs + q]
                    s_ref[reg, 2 * q] = state[:RWKV_N, :RWKV_N].T
                    s_ref[reg, 2 * q + 1] = state[RWKV_N:, RWKV_N:].T

    _run_step(geo, prepare, finish)


def _rwkv(p_flat, s0, shift0, params, batch, seq, geo):
    o, s, shift = _recurrent_call(_rwkv_kernel, geo, batch, seq, [p_flat], [],
                                  [s0, shift0.reshape(1, batch, 1, SHIFT_W)], list(params), [RWKV_W], [], "rwkv7")
    return o, s, shift.reshape(1, batch, SHIFT_W)


def _gdn_kernel(qkv_ref, z_ref, ba_ref, s0_ref, c0_ref, cw_ref, alog_ref, dtb_ref, gnw_ref,
                o_ref, s_ref, cout_ref, xbuf_ref, *, geo):
    chunk = geo.chunk
    taps = GDN_CONV - 1
    region_rows = geo.region_rows
    stride = region_rows + SUBLANES

    @pl.when(pl.program_id(1) == 0)
    def _():
        s_ref[...] = s0_ref[...]
        cout_ref[...] = c0_ref[...]

    ri, ci_ = _square_masks(chunk)
    tril = ri >= ci_
    strict = ri > ci_
    eye = ri == ci_
    heads = range(GDN_HEADS)
    head_slice = lambda base, h: slice(base + h * GDN_DK, base + (h + 1) * GDN_DK)
    l2 = lambda x: x * lax.rsqrt(jnp.sum(x * x, -1, keepdims=True) + L2_EPS)

    for reg in range(geo.regions):
        base = reg * stride + SUBLANES
        xbuf_ref[base - taps:base, :] = cout_ref[reg]
        xbuf_ref[base:base + region_rows, :] = qkv_ref[reg]
        cout_ref[reg] = qkv_ref[reg, region_rows - taps:region_rows, :]

    def prepare(group, ci):
        q_h, k_h, v_h, beta_h, gc, post = [], [], [], [], [], []
        for reg in group:
            rows = geo.rows(reg, ci)
            at = reg * stride + SUBLANES + ci * chunk
            conv = xbuf_ref[at:at + chunk, :] * cw_ref[taps:taps + 1, :]
            for j in range(taps):
                conv = conv + xbuf_ref[at - taps + j:at - taps + j + chunk, :] * cw_ref[j:j + 1, :]
            act = _silu(conv)
            ba = ba_ref[rows]
            beta = jax.nn.sigmoid(ba[:, :GDN_HEADS])
            g = -jnp.exp(alog_ref[...]) * _softplus(ba[:, GDN_HEADS:] + dtb_ref[...])
            gcum = _chunk_cumsum(tril, g)
            q_h += [l2(act[:, head_slice(0, h)]) * (GDN_DK ** -0.5) for h in heads]
            k_h += [l2(act[:, head_slice(GDN_QK, h)]) for h in heads]
            v_h += [act[:, head_slice(2 * GDN_QK, h)] for h in heads]
            beta_h += [beta[:, h:h + 1] for h in heads]
            gc += [gcum[:, h:h + 1] for h in heads]
            post += [(rows, head_slice(0, h)) for h in heads]
        yield
        gc_row = [jnp.sum(jnp.where(eye, g_, 0.0), axis=0, keepdims=True) for g_ in gc]
        decay = _each(lambda c, r: jnp.where(tril, jnp.exp(jnp.where(tril, c - r, 0.0)), 0.0), gc, gc_row)
        kb = _each(jnp.multiply, k_h, beta_h)
        exp_gc = [jnp.exp(g_) for g_ in gc]
        g_last = [g_[chunk - 1:chunk, :] for g_ in gc]
        lower = _each(lambda kb_, k_, d: jnp.where(strict, _mm_nt(kb_, k_) * d, 0.0), kb, k_h, decay)
        yield
        attn = _each(lambda q_, k_, d: _mm_nt(q_, k_) * d, q_h, k_h, decay)
        yield
        q_state = _each(jnp.multiply, q_h, exp_gc)
        k_tail_t = _each(lambda k_, gl, g_: (k_ * jnp.exp(gl - g_)).T, k_h, g_last, gc)
        inverses = yield from _unit_lower_inverses(lower, chunk)
        sol = _each(lambda inv, v_, b, kb_, e: _mm(inv, jnp.concatenate([v_ * b, kb_ * e], axis=1)),
                    inverses, v_h, beta_h, kb, exp_gc)
        yield
        return q_state, sol, attn, k_tail_t, [jnp.exp(gl) for gl in g_last], post

    states = {}

    def finish(group, ci, prepared):
        q_state, sol, attn, k_tail_t, state_decay, post = prepared
        key = group[0]
        if ci == 0:
            states[key] = [s_ref[reg, h] for reg in group for h in heads]
        v_new = _each(lambda s_, state: s_[:, :GDN_DV] - _mm(s_[:, GDN_DV:], state), sol, states[key])
        yield
        o_state = _each(_mm, q_state, states[key])
        yield
        outer = _each(_mm, k_tail_t, v_new)
        yield
        o_intra = _each(_mm, attn, v_new)
        yield
        states[key] = _each(lambda s, d, o: s * d + o, states[key], state_decay, outer)
        o_l = _each(jnp.add, o_state, o_intra)
        ms_l = [jnp.mean(o * o, -1, keepdims=True) for o in o_l]
        for (rows, os_), o, ms in zip(post, o_l, ms_l):
            o_ref[rows[0], rows[1], os_] = (o * lax.rsqrt(ms + NORM_EPS) * gnw_ref[...]
                                            * _silu(z_ref[rows[0], rows[1], os_]))
        if ci == geo.chunks - 1:
            for i, reg in enumerate(group):
                for h in heads:
                    s_ref[reg, h] = states[key][i * GDN_HEADS + h]

    _run_step(geo, prepare, finish)


def _gdn(qkv, z, ba, s0, conv0, params, batch, seq, geo):
    xbuf = pltpu.VMEM((geo.regions * (geo.region_rows + SUBLANES), GDN_QKV), F32)
    return _recurrent_call(_gdn_kernel, geo, batch, seq, [qkv, z, ba], [], [s0, conv0], list(params),
                           [GDN_VW], [xbuf], "gated_delta")


def _run_group(x, pos, s_ret, s_rwkv, s_shift, s_gdn, s_conv, geo, w):
    batch, seq, _ = x.shape
    n_tokens = batch * seq
    h0 = x.reshape(n_tokens, D_MODEL)
    p_ret, p_rwkv = _token_call(
        _even_in_kernel, n_tokens, {"x": h0}, {"nw": w["norm_e"], "w": w["w_in_e"]},
        [RET_W, RWKV_IN], ["x", "nw", "w"], "even_in", EVEN_IN_TILE)
    o_ret, ret_new = _retention(p_ret, pos, s_ret, w["ret_norm_w"], batch, seq, geo)
    o_rwkv, rwkv_new, shift_new = _rwkv(p_rwkv, s_rwkv, s_shift, w["rwkv_params"], batch, seq, geo)
    h1, qkv, z, ba = _token_call(
        _even_out_odd_in_kernel, n_tokens, {"h": h0, "oret": o_ret, "orwkv": o_rwkv},
        {"wout": w["w_out_e"], "nw": w["norm_o"], "win": w["w_in_o"], "wba": w["w_ba_o"]},
        [D_MODEL, GDN_QKV, GDN_VW, 2 * GDN_HEADS], ["h", "oret", "orwkv", "wout", "nw", "win", "wba"],
        "even_out_odd_in", ODD_IN_TILE)
    o_gdn, gdn_new, conv_new = _gdn(qkv, z, ba, s_gdn, s_conv, w["gdn_params"], batch, seq, geo)
    (y,) = _token_call(
        _odd_out_final_kernel, n_tokens, {"h": h1, "o": o_gdn}, {"wout": w["w_out_o"], "nw": w["final_norm"]},
        [D_MODEL], ["h", "o", "wout", "nw"], "odd_out_final", ODD_OUT_TILE)
    return y.reshape(batch, seq, D_MODEL), ret_new, rwkv_new, shift_new, gdn_new, conv_new


def kernel(x_prompt, x_sample, state_ret, state_rwkv, state_shift, state_gdn, state_conv, norm_e, w_in_e, rwkv_mu, rwkv_w0, rwkv_w2, rwkv_a0, rwkv_a2, rwkv_kk, rwkv_ka, rwkv_rk, rwkv_ln_w, rwkv_ln_b, ret_norm_w, w_out_e, norm_o, w_in_o, gdn_conv_w, gdn_a_log, gdn_dt_bias, gdn_norm_w, w_out_o, final_norm):
    assert state_ret.shape[0] == 1 and state_gdn.shape[0] == 1, "one even and one odd layer"
    row = lambda a: a.reshape(1, -1)
    n_qkvz = GDN_QKV + GDN_VW
    w = {
        "norm_e": row(norm_e[0]), "w_in_e": w_in_e[0].astype(BF16), "ret_norm_w": row(ret_norm_w[0]),
        "rwkv_params": (row(rwkv_mu[0]), row(rwkv_w0[0]), rwkv_w2[0], row(rwkv_a0[0]), rwkv_a2[0], row(rwkv_kk[0]),
                        row(rwkv_ka[0]), row(rwkv_rk[0]), row(rwkv_ln_w[0]), row(rwkv_ln_b[0])),
        "w_out_e": w_out_e[0].astype(BF16), "norm_o": row(norm_o[0]),
        "w_in_o": w_in_o[0][:, :n_qkvz].astype(BF16), "w_ba_o": w_in_o[0][:, n_qkvz:].astype(BF16),
        "gdn_params": (gdn_conv_w[0], row(gdn_a_log[0]), row(gdn_dt_bias[0]), row(gdn_norm_w[0])),
        "w_out_o": w_out_o[0].astype(BF16), "final_norm": row(final_norm),
    }
    batch, seq, _ = x_prompt.shape
    dec_batch, dec_seq, _ = x_sample.shape
    zeros = lambda s: jnp.zeros((1, batch) + s.shape[2:], F32)
    prompt = _run_group(x_prompt, jnp.arange(seq), zeros(state_ret), zeros(state_rwkv), zeros(state_shift),
                        zeros(state_gdn), zeros(state_conv),
                        _Geometry(math.gcd(seq, PROMPT_CHUNK), PROMPT_SEQS_PER_STEP, PROMPT_CHUNKS_PER_STEP,
                                  PROMPT_SEQS_PER_STEP), w)
    sample = _run_group(x_sample, PAST_LEN + jnp.arange(dec_seq), state_ret, state_rwkv, state_shift,
                        state_gdn, state_conv, _Geometry(dec_seq, SAMPLE_SEQS_PER_STEP, 1, SAMPLE_SEQS_JOINT), w)
    return (prompt[0], sample[0]) + prompt[1:] + sample[1:]
```

```python
import functools
import math
from typing import NamedTuple

import jax
import jax.numpy as jnp
from jax import lax
from jax.experimental import pallas as pl
from jax.experimental.pallas import tpu as pltpu

F32 = jnp.float32
BF16 = jnp.bfloat16

D_MODEL = 1024
PAST_LEN = 16384
RET_HEADS, RET_DK, RET_DV = 4, 64, 128
RET_ROPE_BASE = 10000.0
RET_Q = RET_HEADS * RET_DK
RET_V = RET_HEADS * RET_DV
RET_W = 2 * RET_Q + 2 * RET_V
RWKV_HEADS, RWKV_N = 8, 64
RWKV_W = RWKV_HEADS * RWKV_N
RWKV_LORA = 64
RWKV_GN_EPS = 64e-5
SHIFT_W = 3 * RWKV_W + 2 * RWKV_LORA
RWKV_IN = SHIFT_W + RWKV_W
GDN_HEADS, GDN_DK, GDN_DV, GDN_CONV = 8, 128, 128, 4
GDN_QK = GDN_HEADS * GDN_DK
GDN_VW = GDN_HEADS * GDN_DV
GDN_QKV = 2 * GDN_QK + GDN_VW
NORM_EPS = 1e-6
L2_EPS = 1e-12

SUBLANES = 8
INV_BLOCK = 16
VMEM_LIMIT = 56 * 1024 * 1024
PROMPT_CHUNK = 64
PROMPT_CHUNKS_PER_STEP = 2
PROMPT_SEQS_PER_STEP = 4
SAMPLE_SEQS_PER_STEP = 16
SAMPLE_SEQS_JOINT = 8
PREPARE_STAGES_PER_FINISH_STAGE = 3
EVEN_IN_TILE = 512
ODD_IN_TILE = 512
ODD_OUT_TILE = 1024


def _split_bf16(x):
    hi = x.astype(BF16)
    return hi, (x - hi.astype(F32)).astype(BF16)


def _dot(a, b, dims, mode):
    dot = lambda x, y: lax.dot_general(x, y, (dims, ((), ())), preferred_element_type=F32)
    if mode == "bf16":
        return dot(a.astype(BF16), b.astype(BF16))
    assert mode == "bf16x3", mode
    a_hi, a_lo = _split_bf16(a)
    b_hi, b_lo = _split_bf16(b)
    return (dot(a_lo, b_hi) + dot(a_hi, b_lo)) + dot(a_hi, b_hi)


def _mm(a, b, mode="bf16"):
    return _dot(a, b, ((1,), (0,)), mode)


def _mm_nt(a, b, mode="bf16"):
    return _dot(a, b, ((1,), (1,)), mode)


def _mm_tn(a, b, mode="bf16"):
    return _dot(a, b, ((0,), (0,)), mode)


def _mm3(a, b):
    return _mm(a, b, "bf16x3")


def _chunk_cumsum(tril, x):
    ones = tril.astype(BF16)
    hi = x.astype(BF16)
    rest = x - hi.astype(F32)
    mid = rest.astype(BF16)
    lo = (rest - mid.astype(F32)).astype(BF16)
    dot = lambda p: jnp.dot(ones, p, preferred_element_type=F32)
    return (dot(lo) + dot(mid)) + dot(hi)


def _silu(x):
    return x * jax.nn.sigmoid(x)


def _softplus(x):
    return jnp.maximum(x, 0.0) + jnp.log1p(jnp.exp(-jnp.abs(x)))


def _square_masks(n):
    ri = lax.broadcasted_iota(jnp.int32, (n, n), 0)
    ci = lax.broadcasted_iota(jnp.int32, (n, n), 1)
    return ri, ci


def _each(fn, *seqs):
    return [fn(*args) for args in zip(*seqs)]


def _block_rows(x, split):
    low_half = lax.broadcasted_iota(jnp.int32, x.shape, 1) < split
    return jnp.concatenate([jnp.where(low_half, x, 0.0), jnp.where(low_half, 0.0, x)], axis=0)


def _neumann_inverses(lows, eye, n, expand):
    invs = [eye - low for low in lows]
    powers = list(lows)
    k = 2
    while k < n:
        powers = _each(lambda p: _mm(p, expand(p)), powers)
        yield
        invs = _each(lambda inv, p: inv + _mm(inv, expand(p)), invs, powers)
        yield
        k *= 2
    return invs


def _unit_lower_inverses(lows, n, width=1):
    assert width in (1, 2)
    expand = (lambda x: x) if width == 1 else (lambda x: _block_rows(x, n))
    ri = lax.broadcasted_iota(jnp.int32, (n, width * n), 0)
    ci = lax.broadcasted_iota(jnp.int32, (n, width * n), 1) & (n - 1)
    eye = (ri == ci).astype(F32)
    if n <= INV_BLOCK:
        invs = yield from _neumann_inverses(lows, eye, n, expand)
    else:
        shift = int(math.log2(INV_BLOCK))
        same_block = (ri >> shift) == (ci >> shift)
        diag_invs = yield from _neumann_inverses([jnp.where(same_block, low, 0.0) for low in lows], eye, INV_BLOCK,
                                                 expand)
        offs = _each(lambda d, low: _mm(d, expand(jnp.where(same_block, 0.0, low))), diag_invs, lows)
        yield
        off_invs = yield from _neumann_inverses(offs, eye, n // INV_BLOCK, expand)
        invs = _each(lambda o, d: _mm(o, expand(d)), off_invs, diag_invs)
        yield
    residuals = _each(lambda low, inv: (eye - inv) - _mm3(low, expand(inv)), lows, invs)
    yield
    refined = _each(lambda inv, res: inv + _mm(inv, expand(res)), invs, residuals)
    yield
    return refined


def _rmsnorm_rows(x, w):
    return x * lax.rsqrt(jnp.mean(x * x, -1, keepdims=True) + NORM_EPS) * w


def _even_in_kernel(x_ref, nw_ref, w_ref, ret_ref, rwkv_ref):
    xn = _rmsnorm_rows(x_ref[...], nw_ref[...]).astype(BF16)
    ret_ref[...] = jnp.dot(xn, w_ref[:, :RET_W], preferred_element_type=F32)
    rwkv_ref[...] = jnp.dot(xn, w_ref[:, RET_W:], preferred_element_type=F32)


def _even_out_odd_in_kernel(h_ref, oret_ref, orwkv_ref, wout_ref, nw_ref, win_ref, wba_ref,
                            h1_ref, qkv_ref, z_ref, ba_ref):
    mix = jnp.dot(oret_ref[...].astype(BF16), wout_ref[:RET_V, :], preferred_element_type=F32)
    mix = mix + jnp.dot(orwkv_ref[...].astype(BF16), wout_ref[RET_V:, :], preferred_element_type=F32)
    h1 = h_ref[...] + mix
    h1_ref[...] = h1
    xn = _rmsnorm_rows(h1, nw_ref[...]).astype(BF16)
    qkv_ref[...] = jnp.dot(xn, win_ref[:, :GDN_QKV], preferred_element_type=F32)
    z_ref[...] = jnp.dot(xn, win_ref[:, GDN_QKV:], preferred_element_type=F32)
    ba_ref[...] = jnp.dot(xn, wba_ref[...], preferred_element_type=F32)


def _odd_out_final_kernel(h_ref, o_ref, wout_ref, nw_ref, y_ref):
    h2 = h_ref[...] + jnp.dot(o_ref[...].astype(BF16), wout_ref[...], preferred_element_type=F32)
    y_ref[...] = _rmsnorm_rows(h2, nw_ref[...])


def _row_spec(tile, width):
    return pl.BlockSpec((tile, width), lambda i: (i, 0))


def _full_spec(shape):
    return pl.BlockSpec(shape, lambda i: (0,) * len(shape), pipeline_mode=pl.Buffered(1))


def _token_call(kernel, n_tokens, row_inputs, full_inputs, out_widths, order, name, tile):
    tile = min(tile, n_tokens)
    assert n_tokens % tile == 0
    specs = {**{k: _row_spec(tile, v.shape[1]) for k, v in row_inputs.items()},
             **{k: _full_spec(v.shape) for k, v in full_inputs.items()}}
    arrays = {**row_inputs, **full_inputs}
    return pl.pallas_call(
        kernel,
        grid=(n_tokens // tile,),
        in_specs=[specs[k] for k in order],
        out_specs=[_row_spec(tile, w) for w in out_widths],
        out_shape=[jax.ShapeDtypeStruct((n_tokens, w), F32) for w in out_widths],
        compiler_params=pltpu.CompilerParams(dimension_semantics=("parallel",), vmem_limit_bytes=VMEM_LIMIT),
        name=name,
    )(*[arrays[k] for k in order])


class _Geometry(NamedTuple):
    chunk: int
    regions: int
    chunks: int
    joint: int

    @property
    def region_rows(self):
        return self.chunks * self.chunk

    def rows(self, region, chunk_index, lanes=slice(None)):
        return region, slice(chunk_index * self.chunk, (chunk_index + 1) * self.chunk), lanes


def _recurrent_call(body, geo, batch, seq, rows, tables, states, consts, out_widths, scratch_shapes, name):
    steps = seq // geo.region_rows
    assert batch % geo.regions == 0 and steps * geo.region_rows == seq
    row_spec = lambda w: pl.BlockSpec((geo.regions, geo.region_rows, w), lambda b, c: (b, c, 0))
    table_spec = lambda a: pl.BlockSpec((geo.region_rows, a.shape[1]), lambda b, c: (c, 0))
    state_spec = lambda a: pl.BlockSpec((None, geo.regions) + a.shape[2:], lambda b, c: (0, b) + (0,) * (a.ndim - 2))
    const_spec = lambda a: pl.BlockSpec(a.shape, lambda b, c: (0,) * a.ndim)
    outs = pl.pallas_call(
        functools.partial(body, geo=geo),
        grid=(batch // geo.regions, steps),
        in_specs=[row_spec(a.shape[1]) for a in rows] + [table_spec(a) for a in tables]
        + [state_spec(a) for a in states] + [const_spec(a) for a in consts],
        out_specs=[row_spec(w) for w in out_widths] + [state_spec(a) for a in states],
        out_shape=[jax.ShapeDtypeStruct((batch, seq, w), F32) for w in out_widths]
        + [jax.ShapeDtypeStruct(a.shape, F32) for a in states],
        scratch_shapes=scratch_shapes,
        compiler_params=pltpu.CompilerParams(dimension_semantics=("parallel", "arbitrary"),
                                             vmem_limit_bytes=VMEM_LIMIT),
        name=name,
    )(*[a.reshape(batch, seq, a.shape[1]) for a in rows], *tables, *states, *consts)
    return [o.reshape(batch * seq, o.shape[2]) for o in outs[:len(out_widths)]] + list(outs[len(out_widths):])


def _run_step(geo, prepare, finish):
    groups = [list(range(g, g + geo.joint)) for g in range(0, geo.regions, geo.joint)]
    items = [(group, ci) for group in groups for ci in range(geo.chunks)]

    def advance(gen, stages):
        for _ in range(stages):
            try:
                next(gen)
            except StopIteration as stop:
                return True, stop.value
        return False, None

    _, prepared = advance(prepare(*items[0]), 10 ** 6)
    for n, item in enumerate(items):
        finishing = finish(*item, prepared)
        preparing = prepare(*items[n + 1]) if n + 1 < len(items) else None
        finished, prepared_next = False, preparing is None
        prepared = None
        while not (finished and prepared_next):
            if not finished:
                finished, _ = advance(finishing, 1)
            if not prepared_next:
                prepared_next, prepared = advance(preparing, PREPARE_STAGES_PER_FINISH_STAGE)


def _ret_kernel(p_ref, cos_ref, sin_ref, s0_ref, dec_ref, qs_ref, ks_ref, ss_ref, nw_ref, o_ref, s_ref, *, geo):
    @pl.when(pl.program_id(1) == 0)
    def _():
        s_ref[...] = s0_ref[...]

    lane = lax.broadcasted_iota(jnp.int32, (geo.chunk, RET_Q), 1)
    even = (lane & 1) == 0
    heads = range(RET_HEADS)
    k_slices = [slice(h * RET_DK, (h + 1) * RET_DK) for h in heads]
    v_slices = [slice(h * RET_DV, (h + 1) * RET_DV) for h in heads]

    def rotary(x, cos, sin):
        partner = jnp.where(even, pltpu.roll(x, RET_Q - 1, 1), pltpu.roll(x, 1, 1))
        return x * cos + partner * sin

    def prepare(group, ci):
        q_l, k_l, qs_l, ks_l, v_l, dec_l, ss_l, post = [], [], [], [], [], [], [], []
        for reg in group:
            rows = geo.rows(reg, ci)
            p = p_ref[rows]
            cos, sin = cos_ref[rows[1], :], sin_ref[rows[1], :]
            q = rotary(p[:, :RET_Q], cos, sin)
            k = rotary(p[:, RET_Q:2 * RET_Q], cos, sin) * (RET_DK ** -0.5)
            q_state = q * qs_ref[...]
            k_state = k * ks_ref[...]
            for h in heads:
                q_l.append(q[:, k_slices[h]])
                k_l.append(k[:, k_slices[h]])
                qs_l.append(q_state[:, k_slices[h]])
                ks_l.append(k_state[:, k_slices[h]])
                v_l.append(p[:, 2 * RET_Q + h * RET_DV:2 * RET_Q + (h + 1) * RET_DV])
                dec_l.append(dec_ref[h])
                ss_l.append(ss_ref[h])
                post.append((rows, v_slices[h], p[:, 2 * RET_Q + RET_V + h * RET_DV:
                                                  2 * RET_Q + RET_V + (h + 1) * RET_DV]))
        scores = _each(lambda q_, k_, d: _mm_nt(q_, k_) * d, q_l, k_l, dec_l)
        yield
        intra = _each(_mm, scores, v_l)
        yield
        outer = _each(_mm_tn, ks_l, v_l)
        yield
        return qs_l, ss_l, post, intra, outer

    states = {}

    def finish(group, ci, prepared):
        qs_l, ss_l, post, intra, outer = prepared
        key = group[0]
        if ci == 0:
            states[key] = [s_ref[reg, h] for reg in group for h in heads]
        from_state = _each(_mm, qs_l, states[key])
        yield
        states[key] = _each(lambda s, ss, o: s * ss + o, states[key], ss_l, outer)
        o_l = _each(jnp.add, from_state, intra)
        ms_l = [jnp.mean(o * o, -1, keepdims=True) for o in o_l]
        for (rows, vs_, gate), o, ms in zip(post, o_l, ms_l):
            o_ref[rows[0], rows[1], vs_] = o * lax.rsqrt(ms + NORM_EPS) * nw_ref[:, vs_] * _silu(gate)
        if ci == geo.chunks - 1:
            for i, reg in enumerate(group):
                for h in heads:
                    s_ref[reg, h] = states[key][i * RET_HEADS + h]

    _run_step(geo, prepare, finish)


def _ret_tables(pos, geo):
    chunk = geo.chunk
    half = RET_DK // 2
    inv = 1.0 / (RET_ROPE_BASE ** jnp.linspace(0.0, 1.0, half, dtype=F32))
    ang = pos.astype(F32)[:, None] * inv[None, :]
    cos = jnp.repeat(jnp.cos(ang), 2, axis=-1)
    sin = jnp.stack([-jnp.sin(ang), jnp.sin(ang)], -1).reshape(ang.shape[0], RET_DK)
    cos, sin = jnp.tile(cos, (1, RET_HEADS)), jnp.tile(sin, (1, RET_HEADS))
    lg = jnp.log1p(-jnp.exp2(-jnp.linspace(5.0, 12.0, RET_HEADS, dtype=F32)))
    ci = jnp.arange(chunk, dtype=F32)
    diff = ci[:, None] - ci[None, :]
    causal = diff >= 0
    decay = jnp.where(causal, jnp.exp(jnp.where(causal, diff, 0.0) * lg[:, None, None]), 0.0)
    q_scale = jnp.exp((ci + 1.0) * lg[:, None])
    k_scale = jnp.exp((chunk - 1.0 - ci) * lg[:, None])
    s_scale = jnp.exp(chunk * lg)
    widen = lambda t: jnp.repeat(t.T, RET_DK, axis=1)
    s_scale = jnp.broadcast_to(s_scale[:, None, None], (RET_HEADS, 1, RET_DV))
    return cos, sin, decay, widen(q_scale), widen(k_scale), s_scale


def _retention(p_flat, pos, s0, norm_w, batch, seq, geo):
    cos, sin, decay, q_scale, k_scale, s_scale = _ret_tables(pos, geo)
    return _recurrent_call(_ret_kernel, geo, batch, seq, [p_flat], [cos, sin], [s0],
                           [decay, q_scale, k_scale, s_scale, norm_w], [RET_V], [], "retention")


def _rwkv_kernel(p_ref, s0_ref, sh0_ref, mu_ref, w0_ref, w2_ref, a0_ref, a2_ref, kk_ref, ka_ref, rk_ref,
                 lnw_ref, lnb_ref, o_ref, s_ref, shout_ref, *, geo):
    chunk = geo.chunk
    pair_w = 2 * RWKV_N
    n_pairs = RWKV_HEADS // 2

    @pl.when(pl.program_id(1) == 0)
    def _():
        s_ref[...] = s0_ref[...]
        shout_ref[...] = sh0_ref[...]

    ri = lax.broadcasted_iota(jnp.int32, (chunk, 2 * chunk), 0)
    ci_ = lax.broadcasted_iota(jnp.int32, (chunk, 2 * chunk), 1) & (chunk - 1)
    tril = ri >= ci_
    strict = ri > ci_
    sq_r, sq_c = _square_masks(chunk)
    tril_one = sq_r >= sq_c
    first_row = lax.broadcasted_iota(jnp.int32, (chunk, SHIFT_W), 0) == 0
    head0 = lax.broadcasted_iota(jnp.int32, (chunk, pair_w), 1) < RWKV_N
    pr, pc = _square_masks(pair_w)
    pair_eye = pr == pc
    same_head = (pr < RWKV_N) == (pc < RWKV_N)
    pairs = range(n_pairs)
    lanes = [slice(p * pair_w, (p + 1) * pair_w) for p in pairs]
    per_pair = lambda t: [t[:, s] for s in lanes]
    blocks = lambda x: _block_rows(x, RWKV_N)

    def head_sums(x):
        first = jnp.sum(jnp.where(head0, x, 0.0), -1, keepdims=True)
        second = jnp.sum(jnp.where(head0, 0.0, x), -1, keepdims=True)
        return jnp.where(head0, first, second)

    def prepare(group, ci):
        r_p, k_p, v_p, kk_p, b_p, cum_p, ld_p, bonus_p, post = [], [], [], [], [], [], [], [], []
        for reg in group:
            rows = geo.rows(reg, ci)
            p = p_ref[rows]
            sh = p[:, :SHIFT_W]
            before = shout_ref[reg] if ci == 0 else p_ref[reg, rows[1].start - 1:rows[1].start, :SHIFT_W]
            prev = jnp.where(first_row, before, pltpu.roll(sh, 1, 0))
            if ci == geo.chunks - 1:
                shout_ref[reg] = sh[chunk - 1:chunk, :]
            xs = sh + (prev - sh) * mu_ref[...]
            r = xs[:, :RWKV_W]
            k_in = xs[:, RWKV_W:2 * RWKV_W]
            v = xs[:, 2 * RWKV_W:3 * RWKV_W]
            wd = xs[:, 3 * RWKV_W:3 * RWKV_W + RWKV_LORA]
            ad = xs[:, 3 * RWKV_W + RWKV_LORA:]
            w = -_softplus(-(w0_ref[...] + _mm3(jnp.tanh(wd), w2_ref[...]))) - 0.5
            log_decay = -jnp.exp(w)
            a = jax.nn.sigmoid(a0_ref[...] + _mm3(ad, a2_ref[...]))
            k = k_in * (1.0 + (a - 1.0) * ka_ref[...])
            kk = [x * lax.rsqrt(head_sums(x * x) + L2_EPS) for x in per_pair(k_in * kk_ref[...])]
            r_p += per_pair(r)
            k_p += per_pair(k)
            v_p += per_pair(v)
            kk_p += kk
            b_p += _each(jnp.multiply, kk, per_pair(a))
            cum_p += per_pair(_chunk_cumsum(tril_one, log_decay))
            ld_p += per_pair(log_decay)
            bonus_p += [head_sums(x) for x in per_pair(r * k * rk_ref[...])]
            post += [(rows, lanes[q], p[:, SHIFT_W + q * pair_w:SHIFT_W + (q + 1) * pair_w]) for q in pairs]
        yield
        cum_last = [c[chunk - 1:chunk, :] for c in cum_p]
        grow = [jnp.exp(-c) for c in cum_p]
        tail = _each(lambda cl, c: jnp.exp(cl - c), cum_last, cum_p)
        lhs = _each(lambda kk, r_, c, ld: jnp.concatenate([kk * jnp.exp(c - ld), r_ * jnp.exp(c)], axis=0),
                    kk_p, r_p, cum_p, ld_p)
        g_b = _each(lambda x, b, g: _mm_nt(x, blocks(b * g)), lhs, b_p, grow)
        yield
        inverting = _unit_lower_inverses([jnp.where(strict, g[:chunk], 0.0) for g in g_b], chunk, width=2)
        g_k = _each(lambda x, k_, g: _mm_nt(x, blocks(k_ * g)), lhs, k_p, grow)
        yield
        v_blocks = [blocks(v_) for v_ in v_p]
        from_v = _each(lambda g, vb: _mm(jnp.where(strict, g[:chunk], 0.0), vb), g_k, v_blocks)
        yield
        y_v = _each(lambda g, vb: _mm(jnp.where(tril, g[chunk:], 0.0), vb), g_k, v_blocks)
        yield
        m_rb = [jnp.where(tril, g[chunk:], 0.0) for g in g_b]
        k_tail_t = _each(lambda k_, b, t: jnp.concatenate([k_ * t, b * t], axis=0).T, k_p, b_p, tail)
        state_decay = [jnp.sum(jnp.where(pair_eye, jnp.exp(cl), 0.0), axis=1, keepdims=True) for cl in cum_last]
        inverses = yield from inverting
        return lhs, inverses, from_v, y_v, m_rb, k_tail_t, v_p, state_decay, bonus_p, post

    states = {}

    def load_state(reg, q):
        zero = jnp.zeros((RWKV_N, RWKV_N), F32)
        return jnp.concatenate([jnp.concatenate([s_ref[reg, 2 * q].T, zero], axis=1),
                                jnp.concatenate([zero, s_ref[reg, 2 * q + 1].T], axis=1)], axis=0)

    def finish(group, ci, prepared):
        lhs, inverses, from_v, y_v, m_rb, k_tail_t, v_p, state_decay, bonus_p, post = prepared
        key = group[0]
        if ci == 0:
            states[key] = [load_state(reg, q) for reg in group for q in pairs]
        from_state = _each(_mm, lhs, states[key])
        yield
        u = _each(lambda inv, fs, fv: _mm(inv, blocks(fs[:chunk] + fv)), inverses, from_state, from_v)
        yield
        outer = _each(lambda kt, v_, u_: _mm(kt, jnp.concatenate([v_, -u_], axis=0)), k_tail_t, v_p, u)
        yield
        y_u = _each(lambda m, u_: _mm(m, blocks(u_)), m_rb, u)
        yield
        states[key] = _each(lambda s, d, o: s * d + jnp.where(same_head, o, 0.0), states[key], state_decay, outer)
        y_l = _each(lambda fs, yv, yu: fs[chunk:] + yv - yu, from_state, y_v, y_u)
        mean_l = [head_sums(y) * (1.0 / RWKV_N) for y in y_l]
        cen_l = _each(jnp.subtract, y_l, mean_l)
        var_l = [head_sums(jnp.square(c)) * (1.0 / RWKV_N) for c in cen_l]
        for (rows, ls, gate), cen, var, bonus, v_ in zip(post, cen_l, var_l, bonus_p, v_p):
            y = cen * lax.rsqrt(var + RWKV_GN_EPS) * lnw_ref[:, ls] + lnb_ref[:, ls]
            o_ref[rows[0], rows[1], ls] = (y + bonus * v_) * _silu(gate)
        if ci == geo.chunks - 1:
            for i, reg in enumerate(group):
                for q in pairs:
                    state = states[key][i * n_pairs + q]
                    s_ref[reg, 2 * q] = state[:RWKV_N, :RWKV_N].T
                    s_ref[reg, 2 * q + 1] = state[RWKV_N:, RWKV_N:].T

    _run_step(geo, prepare, finish)


def _rwkv(p_flat, s0, shift0, params, batch, seq, geo):
    o, s, shift = _recurrent_call(_rwkv_kernel, geo, batch, seq, [p_flat], [],
                                  [s0, shift0.reshape(1, batch, 1, SHIFT_W)], list(params), [RWKV_W], [], "rwkv7")
    return o, s, shift.reshape(1, batch, SHIFT_W)


def _gdn_kernel(qkv_ref, z_ref, ba_ref, s0_ref, c0_ref, cw_ref, alog_ref, dtb_ref, gnw_ref,
                o_ref, s_ref, cout_ref, xbuf_ref, *, geo):
    chunk = geo.chunk
    taps = GDN_CONV - 1
    region_rows = geo.region_rows
    stride = region_rows + SUBLANES

    @pl.when(pl.program_id(1) == 0)
    def _():
        s_ref[...] = s0_ref[...]
        cout_ref[...] = c0_ref[...]

    ri = lax.broadcasted_iota(jnp.int32, (chunk, 2 * chunk), 0)
    lane = lax.broadcasted_iota(jnp.int32, (chunk, 2 * chunk), 1)
    ci_ = lane & (chunk - 1)
    first_half = lane < chunk
    tril = ri >= ci_
    strict = ri > ci_
    eye = ri == ci_
    sq_r, sq_c = _square_masks(chunk)
    tril_one = sq_r >= sq_c
    heads = range(GDN_HEADS)
    head_slice = lambda base, h: slice(base + h * GDN_DK, base + (h + 1) * GDN_DK)
    l2 = lambda x: x * lax.rsqrt(jnp.sum(x * x, -1, keepdims=True) + L2_EPS)

    for reg in range(geo.regions):
        base = reg * stride + SUBLANES
        xbuf_ref[base - taps:base, :] = cout_ref[reg]
        xbuf_ref[base:base + region_rows, :] = qkv_ref[reg]
        cout_ref[reg] = qkv_ref[reg, region_rows - taps:region_rows, :]

    def prepare(group, ci):
        q_h, k_h, v_h, beta_h, gc, post = [], [], [], [], [], []
        for reg in group:
            rows = geo.rows(reg, ci)
            at = reg * stride + SUBLANES + ci * chunk
            conv = xbuf_ref[at:at + chunk, :] * cw_ref[taps:taps + 1, :]
            for j in range(taps):
                conv = conv + xbuf_ref[at - taps + j:at - taps + j + chunk, :] * cw_ref[j:j + 1, :]
            act = _silu(conv)
            ba = ba_ref[rows]
            beta = jax.nn.sigmoid(ba[:, :GDN_HEADS])
            g = -jnp.exp(alog_ref[...]) * _softplus(ba[:, GDN_HEADS:] + dtb_ref[...])
            gcum = _chunk_cumsum(tril_one, g)
            q_h += [l2(act[:, head_slice(0, h)]) * (GDN_DK ** -0.5) for h in heads]
            k_h += [l2(act[:, head_slice(GDN_QK, h)]) for h in heads]
            v_h += [act[:, head_slice(2 * GDN_QK, h)] for h in heads]
            beta_h += [beta[:, h:h + 1] for h in heads]
            gc += [gcum[:, h:h + 1] for h in heads]
            post += [(rows, head_slice(0, h)) for h in heads]
        yield
        pair = lambda t: [jnp.concatenate(t[i:i + 2], axis=1) for i in range(0, len(t), 2)]
        kb = _each(jnp.multiply, k_h, beta_h)
        exp_gc = [jnp.exp(g_) for g_ in gc]
        g_last = [g_[chunk - 1:chunk, :] for g_ in gc]
        gc_col = [jnp.where(first_half, gc[i], gc[i + 1]) for i in range(0, len(gc), 2)]
        gc_row = [jnp.sum(jnp.where(eye, g_, 0.0), axis=0, keepdims=True) for g_ in gc_col]
        decay = _each(lambda c, r: jnp.where(tril, jnp.exp(jnp.where(tril, c - r, 0.0)), 0.0), gc_col, gc_row)
        k_blocks = [_block_rows(k_, GDN_DK) for k_ in pair(k_h)]
        lower = _each(lambda kb_, kbl, d: jnp.where(strict, _mm_nt(kb_, kbl) * d, 0.0), pair(kb), k_blocks, decay)
        yield
        attn = _each(lambda q_, kbl, d: _mm_nt(q_, kbl) * d, pair(q_h), k_blocks, decay)
        yield
        q_state = _each(jnp.multiply, q_h, exp_gc)
        k_tail_t = _each(lambda k_, gl, g_: (k_ * jnp.exp(gl - g_)).T, k_h, g_last, gc)
        inverses = yield from _unit_lower_inverses(lower, chunk, width=2)
        rhs = _each(lambda v_, b, kb_, e: jnp.concatenate([v_ * b, kb_ * e], axis=1), v_h, beta_h, kb, exp_gc)
        sol = _each(lambda inv, r_: _mm(inv, _block_rows(r_, GDN_DV + GDN_DK)), inverses, pair(rhs))
        yield
        return q_state, sol, attn, k_tail_t, [jnp.exp(gl) for gl in g_last], post

    states = {}

    def finish(group, ci, prepared):
        q_state, sol, attn, k_tail_t, state_decay, post = prepared
        key = group[0]
        if ci == 0:
            states[key] = [s_ref[reg, h] for reg in group for h in heads]
        width = GDN_DV + GDN_DK
        u_w = [s_[:, i * width:(i + 1) * width] for s_ in sol for i in range(2)]
        v_new = _each(lambda s_, state: s_[:, :GDN_DV] - _mm(s_[:, GDN_DV:], state), u_w, states[key])
        yield
        o_state = _each(_mm, q_state, states[key])
        yield
        outer = _each(_mm, k_tail_t, v_new)
        yield
        v_pairs = [jnp.concatenate(v_new[i:i + 2], axis=1) for i in range(0, len(v_new), 2)]
        o_pairs = _each(lambda a_, v_: _mm(a_, _block_rows(v_, GDN_DV)), attn, v_pairs)
        o_intra = [o_[:, i * GDN_DV:(i + 1) * GDN_DV] for o_ in o_pairs for i in range(2)]
        yield
        states[key] = _each(lambda s, d, o: s * d + o, states[key], state_decay, outer)
        o_l = _each(jnp.add, o_state, o_intra)
        ms_l = [jnp.mean(o * o, -1, keepdims=True) for o in o_l]
        for (rows, os_), o, ms in zip(post, o_l, ms_l):
            o_ref[rows[0], rows[1], os_] = (o * lax.rsqrt(ms + NORM_EPS) * gnw_ref[...]
                                            * _silu(z_ref[rows[0], rows[1], os_]))
        if ci == geo.chunks - 1:
            for i, reg in enumerate(group):
                for h in heads:
                    s_ref[reg, h] = states[key][i * GDN_HEADS + h]

    _run_step(geo, prepare, finish)


def _gdn(qkv, z, ba, s0, conv0, params, batch, seq, geo):
    xbuf = pltpu.VMEM((geo.regions * (geo.region_rows + SUBLANES), GDN_QKV), F32)
    return _recurrent_call(_gdn_kernel, geo, batch, seq, [qkv, z, ba], [], [s0, conv0], list(params),
                           [GDN_VW], [xbuf], "gated_delta")


def _run_group(x, pos, s_ret, s_rwkv, s_shift, s_gdn, s_conv, geo, w):
    batch, seq, _ = x.shape
    n_tokens = batch * seq
    h0 = x.reshape(n_tokens, D_MODEL)
    p_ret, p_rwkv = _token_call(
        _even_in_kernel, n_tokens, {"x": h0}, {"nw": w["norm_e"], "w": w["w_in_e"]},
        [RET_W, RWKV_IN], ["x", "nw", "w"], "even_in", EVEN_IN_TILE)
    o_ret, ret_new = _retention(p_ret, pos, s_ret, w["ret_norm_w"], batch, seq, geo)
    o_rwkv, rwkv_new, shift_new = _rwkv(p_rwkv, s_rwkv, s_shift, w["rwkv_params"], batch, seq, geo)
    h1, qkv, z, ba = _token_call(
        _even_out_odd_in_kernel, n_tokens, {"h": h0, "oret": o_ret, "orwkv": o_rwkv},
        {"wout": w["w_out_e"], "nw": w["norm_o"], "win": w["w_in_o"], "wba": w["w_ba_o"]},
        [D_MODEL, GDN_QKV, GDN_VW, 2 * GDN_HEADS], ["h", "oret", "orwkv", "wout", "nw", "win", "wba"],
        "even_out_odd_in", ODD_IN_TILE)
    o_gdn, gdn_new, conv_new = _gdn(qkv, z, ba, s_gdn, s_conv, w["gdn_params"], batch, seq, geo)
    (y,) = _token_call(
        _odd_out_final_kernel, n_tokens, {"h": h1, "o": o_gdn}, {"wout": w["w_out_o"], "nw": w["final_norm"]},
        [D_MODEL], ["h", "o", "wout", "nw"], "odd_out_final", ODD_OUT_TILE)
    return y.reshape(batch, seq, D_MODEL), ret_new, rwkv_new, shift_new, gdn_new, conv_new


def kernel(x_prompt, x_sample, state_ret, state_rwkv, state_shift, state_gdn, state_conv, norm_e, w_in_e, rwkv_mu, rwkv_w0, rwkv_w2, rwkv_a0, rwkv_a2, rwkv_kk, rwkv_ka, rwkv_rk, rwkv_ln_w, rwkv_ln_b, ret_norm_w, w_out_e, norm_o, w_in_o, gdn_conv_w, gdn_a_log, gdn_dt_bias, gdn_norm_w, w_out_o, final_norm):
    assert state_ret.shape[0] == 1 and state_gdn.shape[0] == 1, "one even and one odd layer"
    row = lambda a: a.reshape(1, -1)
    n_qkvz = GDN_QKV + GDN_VW
    w = {
        "norm_e": row(norm_e[0]), "w_in_e": w_in_e[0].astype(BF16), "ret_norm_w": row(ret_norm_w[0]),
        "rwkv_params": (row(rwkv_mu[0]), row(rwkv_w0[0]), rwkv_w2[0], row(rwkv_a0[0]), rwkv_a2[0], row(rwkv_kk[0]),
                        row(rwkv_ka[0]), row(rwkv_rk[0]), row(rwkv_ln_w[0]), row(rwkv_ln_b[0])),
        "w_out_e": w_out_e[0].astype(BF16), "norm_o": row(norm_o[0]),
        "w_in_o": w_in_o[0][:, :n_qkvz].astype(BF16), "w_ba_o": w_in_o[0][:, n_qkvz:].astype(BF16),
        "gdn_params": (gdn_conv_w[0], row(gdn_a_log[0]), row(gdn_dt_bias[0]), row(gdn_norm_w[0])),
        "w_out_o": w_out_o[0].astype(BF16), "final_norm": row(final_norm),
    }
    batch, seq, _ = x_prompt.shape
    dec_batch, dec_seq, _ = x_sample.shape
    zeros = lambda s: jnp.zeros((1, batch) + s.shape[2:], F32)
    prompt = _run_group(x_prompt, jnp.arange(seq), zeros(state_ret), zeros(state_rwkv), zeros(state_shift),
                        zeros(state_gdn), zeros(state_conv),
                        _Geometry(math.gcd(seq, PROMPT_CHUNK), PROMPT_SEQS_PER_STEP, PROMPT_CHUNKS_PER_STEP,
                                  PROMPT_SEQS_PER_STEP), w)
    sample = _run_group(x_sample, PAST_LEN + jnp.arange(dec_seq), state_ret, state_rwkv, state_shift,
                        state_gdn, state_conv, _Geometry(dec_seq, SAMPLE_SEQS_PER_STEP, 1, SAMPLE_SEQS_JOINT), w)
    return (prompt[0], sample[0]) + prompt[1:] + sample[1:]
```

```python
import functools
import math
from typing import NamedTuple

import jax
import jax.numpy as jnp
from jax import lax
from jax.experimental import pallas as pl
from jax.experimental.pallas import tpu as pltpu

F32 = jnp.float32
BF16 = jnp.bfloat16

D_MODEL = 1024
PAST_LEN = 16384
RET_HEADS, RET_DK, RET_DV = 4, 64, 128
RET_ROPE_BASE = 10000.0
RET_Q = RET_HEADS * RET_DK
RET_V = RET_HEADS * RET_DV
RET_W = 2 * RET_Q + 2 * RET_V
RWKV_HEADS, RWKV_N = 8, 64
RWKV_W = RWKV_HEADS * RWKV_N
RWKV_LORA = 64
RWKV_GN_EPS = 64e-5
SHIFT_W = 3 * RWKV_W + 2 * RWKV_LORA
RWKV_IN = SHIFT_W + RWKV_W
GDN_HEADS, GDN_DK, GDN_DV, GDN_CONV = 8, 128, 128, 4
GDN_QK = GDN_HEADS * GDN_DK
GDN_VW = GDN_HEADS * GDN_DV
GDN_QKV = 2 * GDN_QK + GDN_VW
NORM_EPS = 1e-6
L2_EPS = 1e-12

SUBLANES = 8
INV_BLOCK = 16
VMEM_LIMIT = 56 * 1024 * 1024
PROMPT_CHUNK = 64
PROMPT_CHUNKS_PER_STEP = 2
PROMPT_SEQS_PER_STEP = 4
SAMPLE_SEQS_PER_STEP = 16
SAMPLE_SEQS_JOINT = 8
PREPARE_STAGES_PER_FINISH_STAGE = 3
EVEN_IN_TILE = 512
ODD_IN_TILE = 512
ODD_OUT_TILE = 1024


def _split_bf16(x):
    hi = x.astype(BF16)
    return hi, (x - hi.astype(F32)).astype(BF16)


def _dot(a, b, dims, mode):
    dot = lambda x, y: lax.dot_general(x, y, (dims, ((), ())), preferred_element_type=F32)
    if mode == "bf16":
        return dot(a.astype(BF16), b.astype(BF16))
    assert mode == "bf16x3", mode
    a_hi, a_lo = _split_bf16(a)
    b_hi, b_lo = _split_bf16(b)
    return (dot(a_lo, b_hi) + dot(a_hi, b_lo)) + dot(a_hi, b_hi)


def _mm(a, b, mode="bf16"):
    return _dot(a, b, ((1,), (0,)), mode)


def _mm_nt(a, b, mode="bf16"):
    return _dot(a, b, ((1,), (1,)), mode)


def _mm_tn(a, b, mode="bf16"):
    return _dot(a, b, ((0,), (0,)), mode)


def _mm3(a, b):
    return _mm(a, b, "bf16x3")


def _chunk_cumsum(tril, x):
    ones = tril.astype(BF16)
    hi = x.astype(BF16)
    rest = x - hi.astype(F32)
    mid = rest.astype(BF16)
    lo = (rest - mid.astype(F32)).astype(BF16)
    dot = lambda p: jnp.dot(ones, p, preferred_element_type=F32)
    return (dot(lo) + dot(mid)) + dot(hi)


def _silu(x):
    return x * jax.nn.sigmoid(x)


def _softplus(x):
    return jnp.maximum(x, 0.0) + jnp.log1p(jnp.exp(-jnp.abs(x)))


def _square_masks(n):
    ri = lax.broadcasted_iota(jnp.int32, (n, n), 0)
    ci = lax.broadcasted_iota(jnp.int32, (n, n), 1)
    return ri, ci


def _each(fn, *seqs):
    return [fn(*args) for args in zip(*seqs)]


def _block_rows(x, split):
    low_half = lax.broadcasted_iota(jnp.int32, x.shape, 1) < split
    return jnp.concatenate([jnp.where(low_half, x, 0.0), jnp.where(low_half, 0.0, x)], axis=0)


def _neumann_inverses(lows, eye, n, expand):
    invs = [eye - low for low in lows]
    powers = list(lows)
    k = 2
    while k < n:
        powers = _each(lambda p: _mm(p, expand(p)), powers)
        yield
        invs = _each(lambda inv, p: inv + _mm(inv, expand(p)), invs, powers)
        yield
        k *= 2
    return invs


def _unit_lower_inverses(lows, n, width=1):
    assert width in (1, 2)
    expand = (lambda x: x) if width == 1 else (lambda x: _block_rows(x, n))
    ri = lax.broadcasted_iota(jnp.int32, (n, width * n), 0)
    ci = lax.broadcasted_iota(jnp.int32, (n, width * n), 1) & (n - 1)
    eye = (ri == ci).astype(F32)
    if n <= INV_BLOCK:
        invs = yield from _neumann_inverses(lows, eye, n, expand)
    else:
        shift = int(math.log2(INV_BLOCK))
        same_block = (ri >> shift) == (ci >> shift)
        diag_invs = yield from _neumann_inverses([jnp.where(same_block, low, 0.0) for low in lows], eye, INV_BLOCK,
                                                 expand)
        offs = _each(lambda d, low: _mm(d, expand(jnp.where(same_block, 0.0, low))), diag_invs, lows)
        yield
        off_invs = yield from _neumann_inverses(offs, eye, n // INV_BLOCK, expand)
        invs = _each(lambda o, d: _mm(o, expand(d)), off_invs, diag_invs)
        yield
    residuals = _each(lambda low, inv: (eye - inv) - _mm3(low, expand(inv)), lows, invs)
    yield
    refined = _each(lambda inv, res: inv + _mm(inv, expand(res)), invs, residuals)
    yield
    return refined


def _rmsnorm_rows(x, w):
    return x * lax.rsqrt(jnp.mean(x * x, -1, keepdims=True) + NORM_EPS) * w


def _even_in_kernel(x_ref, nw_ref, w_ref, ret_ref, rwkv_ref):
    xn = _rmsnorm_rows(x_ref[...], nw_ref[...]).astype(BF16)
    ret_ref[...] = jnp.dot(xn, w_ref[:, :RET_W], preferred_element_type=F32)
    rwkv_ref[...] = jnp.dot(xn, w_ref[:, RET_W:], preferred_element_type=F32)


def _even_out_odd_in_kernel(h_ref, oret_ref, orwkv_ref, wout_ref, nw_ref, win_ref,
                            h1_ref, qkv_ref, z_ref, ba_ref):
    mix = jnp.dot(oret_ref[...].astype(BF16), wout_ref[:RET_V, :], preferred_element_type=F32)
    mix = mix + jnp.dot(orwkv_ref[...].astype(BF16), wout_ref[RET_V:, :], preferred_element_type=F32)
    h1 = h_ref[...] + mix
    h1_ref[...] = h1
    xn = _rmsnorm_rows(h1, nw_ref[...]).astype(BF16)
    qkv_ref[...] = jnp.dot(xn, win_ref[:, :GDN_QKV], preferred_element_type=F32)
    z_ref[...] = jnp.dot(xn, win_ref[:, GDN_QKV:GDN_QKV + GDN_VW], preferred_element_type=F32)
    ba_ref[...] = jnp.dot(xn, win_ref[:, GDN_QKV + GDN_VW:], preferred_element_type=F32)


def _odd_out_final_kernel(h_ref, o_ref, wout_ref, nw_ref, y_ref):
    h2 = h_ref[...] + jnp.dot(o_ref[...].astype(BF16), wout_ref[...], preferred_element_type=F32)
    y_ref[...] = _rmsnorm_rows(h2, nw_ref[...])


def _row_spec(tile, width):
    return pl.BlockSpec((tile, width), lambda i: (i, 0))


def _full_spec(shape):
    return pl.BlockSpec(shape, lambda i: (0,) * len(shape), pipeline_mode=pl.Buffered(1))


def _token_call(kernel, n_tokens, row_inputs, full_inputs, out_widths, order, name, tile):
    tile = min(tile, n_tokens)
    assert n_tokens % tile == 0
    specs = {**{k: _row_spec(tile, v.shape[1]) for k, v in row_inputs.items()},
             **{k: _full_spec(v.shape) for k, v in full_inputs.items()}}
    arrays = {**row_inputs, **full_inputs}
    return pl.pallas_call(
        kernel,
        grid=(n_tokens // tile,),
        in_specs=[specs[k] for k in order],
        out_specs=[_row_spec(tile, w) for w in out_widths],
        out_shape=[jax.ShapeDtypeStruct((n_tokens, w), F32) for w in out_widths],
        compiler_params=pltpu.CompilerParams(dimension_semantics=("parallel",), vmem_limit_bytes=VMEM_LIMIT),
        name=name,
    )(*[arrays[k] for k in order])


class _Geometry(NamedTuple):
    chunk: int
    regions: int
    chunks: int
    joint: int

    @property
    def region_rows(self):
        return self.chunks * self.chunk

    def rows(self, region, chunk_index, lanes=slice(None)):
        return region, slice(chunk_index * self.chunk, (chunk_index + 1) * self.chunk), lanes


def _recurrent_call(body, geo, batch, seq, rows, tables, states, consts, out_widths, scratch_shapes, name):
    steps = seq // geo.region_rows
    assert batch % geo.regions == 0 and steps * geo.region_rows == seq
    row_spec = lambda w: pl.BlockSpec((geo.regions, geo.region_rows, w), lambda b, c: (b, c, 0))
    table_spec = lambda a: pl.BlockSpec((geo.region_rows, a.shape[1]), lambda b, c: (c, 0))
    state_spec = lambda a: pl.BlockSpec((None, geo.regions) + a.shape[2:], lambda b, c: (0, b) + (0,) * (a.ndim - 2))
    const_spec = lambda a: pl.BlockSpec(a.shape, lambda b, c: (0,) * a.ndim)
    outs = pl.pallas_call(
        functools.partial(body, geo=geo),
        grid=(batch // geo.regions, steps),
        in_specs=[row_spec(a.shape[1]) for a in rows] + [table_spec(a) for a in tables]
        + [state_spec(a) for a in states] + [const_spec(a) for a in consts],
        out_specs=[row_spec(w) for w in out_widths] + [state_spec(a) for a in states],
        out_shape=[jax.ShapeDtypeStruct((batch, seq, w), F32) for w in out_widths]
        + [jax.ShapeDtypeStruct(a.shape, F32) for a in states],
        scratch_shapes=scratch_shapes,
        compiler_params=pltpu.CompilerParams(dimension_semantics=("parallel", "arbitrary"),
                                             vmem_limit_bytes=VMEM_LIMIT),
        name=name,
    )(*[a.reshape(batch, seq, a.shape[1]) for a in rows], *tables, *states, *consts)
    return [o.reshape(batch * seq, o.shape[2]) for o in outs[:len(out_widths)]] + list(outs[len(out_widths):])


def _run_step(geo, prepare, finish):
    groups = [list(range(g, g + geo.joint)) for g in range(0, geo.regions, geo.joint)]
    items = [(group, ci) for group in groups for ci in range(geo.chunks)]

    def advance(gen, stages):
        for _ in range(stages):
            try:
                next(gen)
            except StopIteration as stop:
                return True, stop.value
        return False, None

    _, prepared = advance(prepare(*items[0]), 10 ** 6)
    for n, item in enumerate(items):
        finishing = finish(*item, prepared)
        preparing = prepare(*items[n + 1]) if n + 1 < len(items) else None
        finished, prepared_next = False, preparing is None
        prepared = None
        while not (finished and prepared_next):
            if not finished:
                finished, _ = advance(finishing, 1)
            if not prepared_next:
                prepared_next, prepared = advance(preparing, PREPARE_STAGES_PER_FINISH_STAGE)


def _ret_stages(p_ref, cos_ref, sin_ref, s0_ref, dec_ref, qs_ref, ks_ref, ss_ref, nw_ref, o_ref, s_ref, geo):
    @pl.when(pl.program_id(1) == 0)
    def _():
        s_ref[...] = s0_ref[...]

    lane = lax.broadcasted_iota(jnp.int32, (geo.chunk, RET_Q), 1)
    even = (lane & 1) == 0
    heads = range(RET_HEADS)
    k_slices = [slice(h * RET_DK, (h + 1) * RET_DK) for h in heads]
    v_slices = [slice(h * RET_DV, (h + 1) * RET_DV) for h in heads]

    def rotary(x, cos, sin):
        partner = jnp.where(even, pltpu.roll(x, RET_Q - 1, 1), pltpu.roll(x, 1, 1))
        return x * cos + partner * sin

    def prepare(group, ci):
        q_l, k_l, qs_l, ks_l, v_l, dec_l, ss_l, post = [], [], [], [], [], [], [], []
        for reg in group:
            rows = geo.rows(reg, ci)
            p = p_ref[rows]
            cos, sin = cos_ref[rows[1], :], sin_ref[rows[1], :]
            q = rotary(p[:, :RET_Q], cos, sin)
            k = rotary(p[:, RET_Q:2 * RET_Q], cos, sin) * (RET_DK ** -0.5)
            q_state = q * qs_ref[...]
            k_state = k * ks_ref[...]
            for h in heads:
                q_l.append(q[:, k_slices[h]])
                k_l.append(k[:, k_slices[h]])
                qs_l.append(q_state[:, k_slices[h]])
                ks_l.append(k_state[:, k_slices[h]])
                v_l.append(p[:, 2 * RET_Q + h * RET_DV:2 * RET_Q + (h + 1) * RET_DV])
                dec_l.append(dec_ref[h])
                ss_l.append(ss_ref[h])
                post.append((rows, v_slices[h], p[:, 2 * RET_Q + RET_V + h * RET_DV:
                                                  2 * RET_Q + RET_V + (h + 1) * RET_DV]))
        scores = _each(lambda q_, k_, d: _mm_nt(q_, k_) * d, q_l, k_l, dec_l)
        yield
        intra = _each(_mm, scores, v_l)
        yield
        outer = _each(_mm_tn, ks_l, v_l)
        yield
        return qs_l, ss_l, post, intra, outer

    states = {}

    def finish(group, ci, prepared):
        qs_l, ss_l, post, intra, outer = prepared
        key = group[0]
        if ci == 0:
            states[key] = [s_ref[reg, h] for reg in group for h in heads]
        from_state = _each(_mm, qs_l, states[key])
        yield
        states[key] = _each(lambda s, ss, o: s * ss + o, states[key], ss_l, outer)
        o_l = _each(jnp.add, from_state, intra)
        ms_l = [jnp.mean(o * o, -1, keepdims=True) for o in o_l]
        for (rows, vs_, gate), o, ms in zip(post, o_l, ms_l):
            o_ref[rows[0], rows[1], vs_] = o * lax.rsqrt(ms + NORM_EPS) * nw_ref[:, vs_] * _silu(gate)
        if ci == geo.chunks - 1:
            for i, reg in enumerate(group):
                for h in heads:
                    s_ref[reg, h] = states[key][i * RET_HEADS + h]

    return prepare, finish


def _ret_tables(pos, geo):
    chunk = geo.chunk
    half = RET_DK // 2
    inv = 1.0 / (RET_ROPE_BASE ** jnp.linspace(0.0, 1.0, half, dtype=F32))
    ang = pos.astype(F32)[:, None] * inv[None, :]
    cos = jnp.repeat(jnp.cos(ang), 2, axis=-1)
    sin = jnp.stack([-jnp.sin(ang), jnp.sin(ang)], -1).reshape(ang.shape[0], RET_DK)
    cos, sin = jnp.tile(cos, (1, RET_HEADS)), jnp.tile(sin, (1, RET_HEADS))
    lg = jnp.log1p(-jnp.exp2(-jnp.linspace(5.0, 12.0, RET_HEADS, dtype=F32)))
    ci = jnp.arange(chunk, dtype=F32)
    diff = ci[:, None] - ci[None, :]
    causal = diff >= 0
    decay = jnp.where(causal, jnp.exp(jnp.where(causal, diff, 0.0) * lg[:, None, None]), 0.0)
    q_scale = jnp.exp((ci + 1.0) * lg[:, None])
    k_scale = jnp.exp((chunk - 1.0 - ci) * lg[:, None])
    s_scale = jnp.exp(chunk * lg)
    widen = lambda t: jnp.repeat(t.T, RET_DK, axis=1)
    s_scale = jnp.broadcast_to(s_scale[:, None, None], (RET_HEADS, 1, RET_DV))
    return cos, sin, decay, widen(q_scale), widen(k_scale), s_scale


def _rwkv_stages(p_ref, s0_ref, sh0_ref, mu_ref, w0_ref, w2_ref, a0_ref, a2_ref, kk_ref, ka_ref, rk_ref,
                 lnw_ref, lnb_ref, o_ref, s_ref, shout_ref, geo):
    chunk = geo.chunk
    pair_w = 2 * RWKV_N
    n_pairs = RWKV_HEADS // 2

    @pl.when(pl.program_id(1) == 0)
    def _():
        s_ref[...] = s0_ref[...]
        shout_ref[...] = sh0_ref[...]

    ri = lax.broadcasted_iota(jnp.int32, (chunk, 2 * chunk), 0)
    ci_ = lax.broadcasted_iota(jnp.int32, (chunk, 2 * chunk), 1) & (chunk - 1)
    tril = ri >= ci_
    strict = ri > ci_
    sq_r, sq_c = _square_masks(chunk)
    tril_one = sq_r >= sq_c
    first_row = lax.broadcasted_iota(jnp.int32, (chunk, SHIFT_W), 0) == 0
    head0 = lax.broadcasted_iota(jnp.int32, (chunk, pair_w), 1) < RWKV_N
    pr, pc = _square_masks(pair_w)
    pair_eye = pr == pc
    same_head = (pr < RWKV_N) == (pc < RWKV_N)
    pairs = range(n_pairs)
    lanes = [slice(p * pair_w, (p + 1) * pair_w) for p in pairs]
    per_pair = lambda t: [t[:, s] for s in lanes]
    blocks = lambda x: _block_rows(x, RWKV_N)

    def head_sums(x):
        first = jnp.sum(jnp.where(head0, x, 0.0), -1, keepdims=True)
        second = jnp.sum(jnp.where(head0, 0.0, x), -1, keepdims=True)
        return jnp.where(head0, first, second)

    def prepare(group, ci):
        r_p, k_p, v_p, kk_p, b_p, cum_p, ld_p, bonus_p, post = [], [], [], [], [], [], [], [], []
        for reg in group:
            rows = geo.rows(reg, ci)
            p = p_ref[rows]
            sh = p[:, :SHIFT_W]
            before = shout_ref[reg] if ci == 0 else p_ref[reg, rows[1].start - 1:rows[1].start, :SHIFT_W]
            prev = jnp.where(first_row, before, pltpu.roll(sh, 1, 0))
            if ci == geo.chunks - 1:
                shout_ref[reg] = sh[chunk - 1:chunk, :]
            xs = sh + (prev - sh) * mu_ref[...]
            r = xs[:, :RWKV_W]
            k_in = xs[:, RWKV_W:2 * RWKV_W]
            v = xs[:, 2 * RWKV_W:3 * RWKV_W]
            wd = xs[:, 3 * RWKV_W:3 * RWKV_W + RWKV_LORA]
            ad = xs[:, 3 * RWKV_W + RWKV_LORA:]
            w = -_softplus(-(w0_ref[...] + _mm3(jnp.tanh(wd), w2_ref[...]))) - 0.5
            log_decay = -jnp.exp(w)
            a = jax.nn.sigmoid(a0_ref[...] + _mm3(ad, a2_ref[...]))
            k = k_in * (1.0 + (a - 1.0) * ka_ref[...])
            kk = [x * lax.rsqrt(head_sums(x * x) + L2_EPS) for x in per_pair(k_in * kk_ref[...])]
            r_p += per_pair(r)
            k_p += per_pair(k)
            v_p += per_pair(v)
            kk_p += kk
            b_p += _each(jnp.multiply, kk, per_pair(a))
            cum_p += per_pair(_chunk_cumsum(tril_one, log_decay))
            ld_p += per_pair(log_decay)
            bonus_p += [head_sums(x) for x in per_pair(r * k * rk_ref[...])]
            post += [(rows, lanes[q], p[:, SHIFT_W + q * pair_w:SHIFT_W + (q + 1) * pair_w]) for q in pairs]
        yield
        cum_last = [c[chunk - 1:chunk, :] for c in cum_p]
        grow = [jnp.exp(-c) for c in cum_p]
        tail = _each(lambda cl, c: jnp.exp(cl - c), cum_last, cum_p)
        lhs = _each(lambda kk, r_, c, ld: jnp.concatenate([kk * jnp.exp(c - ld), r_ * jnp.exp(c)], axis=0),
                    kk_p, r_p, cum_p, ld_p)
        g_b = _each(lambda x, b, g: _mm_nt(x, blocks(b * g)), lhs, b_p, grow)
        yield
        inverting = _unit_lower_inverses([jnp.where(strict, g[:chunk], 0.0) for g in g_b], chunk, width=2)
        g_k = _each(lambda x, k_, g: _mm_nt(x, blocks(k_ * g)), lhs, k_p, grow)
        yield
        v_blocks = [blocks(v_) for v_ in v_p]
        from_v = _each(lambda g, vb: _mm(jnp.where(strict, g[:chunk], 0.0), vb), g_k, v_blocks)
        yield
        y_v = _each(lambda g, vb: _mm(jnp.where(tril, g[chunk:], 0.0), vb), g_k, v_blocks)
        yield
        m_rb = [jnp.where(tril, g[chunk:], 0.0) for g in g_b]
        k_tail_t = _each(lambda k_, b, t: jnp.concatenate([k_ * t, b * t], axis=0).T, k_p, b_p, tail)
        state_decay = [jnp.sum(jnp.where(pair_eye, jnp.exp(cl), 0.0), axis=1, keepdims=True) for cl in cum_last]
        inverses = yield from inverting
        return lhs, inverses, from_v, y_v, m_rb, k_tail_t, v_p, state_decay, bonus_p, post

    states = {}

    def load_state(reg, q):
        zero = jnp.zeros((RWKV_N, RWKV_N), F32)
        return jnp.concatenate([jnp.concatenate([s_ref[reg, 2 * q].T, zero], axis=1),
                                jnp.concatenate([zero, s_ref[reg, 2 * q + 1].T], axis=1)], axis=0)

    def finish(group, ci, prepared):
        lhs, inverses, from_v, y_v, m_rb, k_tail_t, v_p, state_decay, bonus_p, post = prepared
        key = group[0]
        if ci == 0:
            states[key] = [load_state(reg, q) for reg in group for q in pairs]
        from_state = _each(_mm, lhs, states[key])
        yield
        u = _each(lambda inv, fs, fv: _mm(inv, blocks(fs[:chunk] + fv)), inverses, from_state, from_v)
        yield
        outer = _each(lambda kt, v_, u_: _mm(kt, jnp.concatenate([v_, -u_], axis=0)), k_tail_t, v_p, u)
        yield
        y_u = _each(lambda m, u_: _mm(m, blocks(u_)), m_rb, u)
        yield
        states[key] = _each(lambda s, d, o: s * d + jnp.where(same_head, o, 0.0), states[key], state_decay, outer)
        y_l = _each(lambda fs, yv, yu: fs[chunk:] + yv - yu, from_state, y_v, y_u)
        mean_l = [head_sums(y) * (1.0 / RWKV_N) for y in y_l]
        cen_l = _each(jnp.subtract, y_l, mean_l)
        var_l = [head_sums(jnp.square(c)) * (1.0 / RWKV_N) for c in cen_l]
        for (rows, ls, gate), cen, var, bonus, v_ in zip(post, cen_l, var_l, bonus_p, v_p):
            y = cen * lax.rsqrt(var + RWKV_GN_EPS) * lnw_ref[:, ls] + lnb_ref[:, ls]
            o_ref[rows[0], rows[1], ls] = (y + bonus * v_) * _silu(gate)
        if ci == geo.chunks - 1:
            for i, reg in enumerate(group):
                for q in pairs:
                    state = states[key][i * n_pairs + q]
                    s_ref[reg, 2 * q] = state[:RWKV_N, :RWKV_N].T
                    s_ref[reg, 2 * q + 1] = state[RWKV_N:, RWKV_N:].T

    return prepare, finish


def _alternate(*generators):
    values = [None] * len(generators)
    live = list(range(len(generators)))
    while live:
        for n in list(live):
            try:
                next(generators[n])
            except StopIteration as stop:
                values[n] = stop.value
                live.remove(n)
        yield
    return values


N_RET_CONSTS = 5
N_RWKV_CONSTS = 10


def _even_mixers_kernel(p_ret_ref, p_rwkv_ref, cos_ref, sin_ref, s0_ret_ref, s0_rwkv_ref, sh0_ref, *rest, geo):
    ret_consts, rest = rest[:N_RET_CONSTS], rest[N_RET_CONSTS:]
    rwkv_consts, rest = rest[:N_RWKV_CONSTS], rest[N_RWKV_CONSTS:]
    o_ret_ref, o_rwkv_ref, s_ret_ref, s_rwkv_ref, shout_ref = rest
    ret_prepare, ret_finish = _ret_stages(p_ret_ref, cos_ref, sin_ref, s0_ret_ref, *ret_consts, o_ret_ref,
                                          s_ret_ref, geo)
    rwkv_prepare, rwkv_finish = _rwkv_stages(p_rwkv_ref, s0_rwkv_ref, sh0_ref, *rwkv_consts, o_rwkv_ref,
                                             s_rwkv_ref, shout_ref, geo)

    def prepare(group, ci):
        return _alternate(rwkv_prepare(group, ci), ret_prepare(group, ci))

    def finish(group, ci, prepared):
        return _alternate(rwkv_finish(group, ci, prepared[0]), ret_finish(group, ci, prepared[1]))

    _run_step(geo, prepare, finish)


def _even_mixers(p_ret, p_rwkv, pos, s_ret, s_rwkv, shift0, ret_norm_w, rwkv_params, batch, seq, geo):
    cos, sin, decay, q_scale, k_scale, s_scale = _ret_tables(pos, geo)
    ret_consts = [decay, q_scale, k_scale, s_scale, ret_norm_w]
    assert len(ret_consts) == N_RET_CONSTS and len(rwkv_params) == N_RWKV_CONSTS
    o_ret, o_rwkv, ret_new, rwkv_new, shift = _recurrent_call(
        _even_mixers_kernel, geo, batch, seq, [p_ret, p_rwkv], [cos, sin],
        [s_ret, s_rwkv, shift0.reshape(1, batch, 1, SHIFT_W)], ret_consts + list(rwkv_params),
        [RET_V, RWKV_W], [], "even_mixers")
    return o_ret, o_rwkv, ret_new, rwkv_new, shift.reshape(1, batch, SHIFT_W)


def _gdn_kernel(qkv_ref, z_ref, ba_ref, s0_ref, c0_ref, cw_ref, alog_ref, dtb_ref, gnw_ref,
                o_ref, s_ref, cout_ref, xbuf_ref, *, geo):
    chunk = geo.chunk
    taps = GDN_CONV - 1
    region_rows = geo.region_rows
    stride = region_rows + SUBLANES

    @pl.when(pl.program_id(1) == 0)
    def _():
        s_ref[...] = s0_ref[...]
        cout_ref[...] = c0_ref[...]

    ri = lax.broadcasted_iota(jnp.int32, (chunk, 2 * chunk), 0)
    lane = lax.broadcasted_iota(jnp.int32, (chunk, 2 * chunk), 1)
    ci_ = lane & (chunk - 1)
    first_half = lane < chunk
    tril = ri >= ci_
    strict = ri > ci_
    eye = ri == ci_
    sq_r, sq_c = _square_masks(chunk)
    tril_one = sq_r >= sq_c
    heads = range(GDN_HEADS)
    head_slice = lambda base, h: slice(base + h * GDN_DK, base + (h + 1) * GDN_DK)
    l2 = lambda x: x * lax.rsqrt(jnp.sum(x * x, -1, keepdims=True) + L2_EPS)

    for reg in range(geo.regions):
        base = reg * stride + SUBLANES
        xbuf_ref[base - taps:base, :] = cout_ref[reg]
        xbuf_ref[base:base + region_rows, :] = qkv_ref[reg]
        cout_ref[reg] = qkv_ref[reg, region_rows - taps:region_rows, :]

    def prepare(group, ci):
        q_h, k_h, v_h, beta_h, gc, post = [], [], [], [], [], []
        for reg in group:
            rows = geo.rows(reg, ci)
            at = reg * stride + SUBLANES + ci * chunk
            conv = xbuf_ref[at:at + chunk, :] * cw_ref[taps:taps + 1, :]
            for j in range(taps):
                conv = conv + xbuf_ref[at - taps + j:at - taps + j + chunk, :] * cw_ref[j:j + 1, :]
            act = _silu(conv)
            ba = ba_ref[rows]
            beta = jax.nn.sigmoid(ba[:, :GDN_HEADS])
            g = -jnp.exp(alog_ref[...]) * _softplus(ba[:, GDN_HEADS:] + dtb_ref[...])
            gcum = _chunk_cumsum(tril_one, g)
            q_h += [l2(act[:, head_slice(0, h)]) * (GDN_DK ** -0.5) for h in heads]
            k_h += [l2(act[:, head_slice(GDN_QK, h)]) for h in heads]
            v_h += [act[:, head_slice(2 * GDN_QK, h)] for h in heads]
            beta_h += [beta[:, h:h + 1] for h in heads]
            gc += [gcum[:, h:h + 1] for h in heads]
            post += [(rows, head_slice(0, h)) for h in heads]
        yield
        pair = lambda t: [jnp.concatenate(t[i:i + 2], axis=1) for i in range(0, len(t), 2)]
        kb = _each(jnp.multiply, k_h, beta_h)
        exp_gc = [jnp.exp(g_) for g_ in gc]
        g_last = [g_[chunk - 1:chunk, :] for g_ in gc]
        gc_col = [jnp.where(first_half, gc[i], gc[i + 1]) for i in range(0, len(gc), 2)]
        gc_row = [jnp.sum(jnp.where(eye, g_, 0.0), axis=0, keepdims=True) for g_ in gc_col]
        decay = _each(lambda c, r: jnp.where(tril, jnp.exp(jnp.where(tril, c - r, 0.0)), 0.0), gc_col, gc_row)
        k_blocks = [_block_rows(k_, GDN_DK) for k_ in pair(k_h)]
        lower = _each(lambda kb_, kbl, d: jnp.where(strict, _mm_nt(kb_, kbl) * d, 0.0), pair(kb), k_blocks, decay)
        yield
        attn = _each(lambda q_, kbl, d: _mm_nt(q_, kbl) * d, pair(q_h), k_blocks, decay)
        yield
        q_state = _each(jnp.multiply, q_h, exp_gc)
        k_tail_t = _each(lambda k_, gl, g_: (k_ * jnp.exp(gl - g_)).T, k_h, g_last, gc)
        inverses = yield from _unit_lower_inverses(lower, chunk, width=2)
        rhs = _each(lambda v_, b, kb_, e: jnp.concatenate([v_ * b, kb_ * e], axis=1), v_h, beta_h, kb, exp_gc)
        sol = _each(lambda inv, r_: _mm(inv, _block_rows(r_, GDN_DV + GDN_DK)), inverses, pair(rhs))
        yield
        return q_state, sol, attn, k_tail_t, [jnp.exp(gl) for gl in g_last], post

    states = {}

    def finish(group, ci, prepared):
        q_state, sol, attn, k_tail_t, state_decay, post = prepared
        key = group[0]
        if ci == 0:
            states[key] = [s_ref[reg, h] for reg in group for h in heads]
        width = GDN_DV + GDN_DK
        u_w = [s_[:, i * width:(i + 1) * width] for s_ in sol for i in range(2)]
        v_new = _each(lambda s_, state: s_[:, :GDN_DV] - _mm(s_[:, GDN_DV:], state), u_w, states[key])
        yield
        o_state = _each(_mm, q_state, states[key])
        yield
        outer = _each(_mm, k_tail_t, v_new)
        yield
        v_pairs = [jnp.concatenate(v_new[i:i + 2], axis=1) for i in range(0, len(v_new), 2)]
        o_pairs = _each(lambda a_, v_: _mm(a_, _block_rows(v_, GDN_DV)), attn, v_pairs)
        o_intra = [o_[:, i * GDN_DV:(i + 1) * GDN_DV] for o_ in o_pairs for i in range(2)]
        yield
        states[key] = _each(lambda s, d, o: s * d + o, states[key], state_decay, outer)
        o_l = _each(jnp.add, o_state, o_intra)
        ms_l = [jnp.mean(o * o, -1, keepdims=True) for o in o_l]
        for (rows, os_), o, ms in zip(post, o_l, ms_l):
            o_ref[rows[0], rows[1], os_] = (o * lax.rsqrt(ms + NORM_EPS) * gnw_ref[...]
                                            * _silu(z_ref[rows[0], rows[1], os_]))
        if ci == geo.chunks - 1:
            for i, reg in enumerate(group):
                for h in heads:
                    s_ref[reg, h] = states[key][i * GDN_HEADS + h]

    _run_step(geo, prepare, finish)


def _gdn(qkv, z, ba, s0, conv0, params, batch, seq, geo):
    xbuf = pltpu.VMEM((geo.regions * (geo.region_rows + SUBLANES), GDN_QKV), F32)
    return _recurrent_call(_gdn_kernel, geo, batch, seq, [qkv, z, ba], [], [s0, conv0], list(params),
                           [GDN_VW], [xbuf], "gated_delta")


def _run_group(x, pos, s_ret, s_rwkv, s_shift, s_gdn, s_conv, geo, w):
    batch, seq, _ = x.shape
    n_tokens = batch * seq
    h0 = x.reshape(n_tokens, D_MODEL)
    p_ret, p_rwkv = _token_call(
        _even_in_kernel, n_tokens, {"x": h0}, {"nw": w["norm_e"], "w": w["w_in_e"]},
        [RET_W, RWKV_IN], ["x", "nw", "w"], "even_in", EVEN_IN_TILE)
    o_ret, o_rwkv, ret_new, rwkv_new, shift_new = _even_mixers(
        p_ret, p_rwkv, pos, s_ret, s_rwkv, s_shift, w["ret_norm_w"], w["rwkv_params"], batch, seq, geo)
    h1, qkv, z, ba = _token_call(
        _even_out_odd_in_kernel, n_tokens, {"h": h0, "oret": o_ret, "orwkv": o_rwkv},
        {"wout": w["w_out_e"], "nw": w["norm_o"], "win": w["w_in_o"]},
        [D_MODEL, GDN_QKV, GDN_VW, 2 * GDN_HEADS], ["h", "oret", "orwkv", "wout", "nw", "win"],
        "even_out_odd_in", ODD_IN_TILE)
    o_gdn, gdn_new, conv_new = _gdn(qkv, z, ba, s_gdn, s_conv, w["gdn_params"], batch, seq, geo)
    (y,) = _token_call(
        _odd_out_final_kernel, n_tokens, {"h": h1, "o": o_gdn}, {"wout": w["w_out_o"], "nw": w["final_norm"]},
        [D_MODEL], ["h", "o", "wout", "nw"], "odd_out_final", ODD_OUT_TILE)
    return y.reshape(batch, seq, D_MODEL), ret_new, rwkv_new, shift_new, gdn_new, conv_new


def kernel(x_prompt, x_sample, state_ret, state_rwkv, state_shift, state_gdn, state_conv, norm_e, w_in_e, rwkv_mu, rwkv_w0, rwkv_w2, rwkv_a0, rwkv_a2, rwkv_kk, rwkv_ka, rwkv_rk, rwkv_ln_w, rwkv_ln_b, ret_norm_w, w_out_e, norm_o, w_in_o, gdn_conv_w, gdn_a_log, gdn_dt_bias, gdn_norm_w, w_out_o, final_norm):
    assert state_ret.shape[0] == 1 and state_gdn.shape[0] == 1, "one even and one odd layer"
    row = lambda a: a.reshape(1, -1)
    w = {
        "norm_e": row(norm_e[0]), "w_in_e": w_in_e[0].astype(BF16), "ret_norm_w": row(ret_norm_w[0]),
        "rwkv_params": (row(rwkv_mu[0]), row(rwkv_w0[0]), rwkv_w2[0], row(rwkv_a0[0]), rwkv_a2[0], row(rwkv_kk[0]),
                        row(rwkv_ka[0]), row(rwkv_rk[0]), row(rwkv_ln_w[0]), row(rwkv_ln_b[0])),
        "w_out_e": w_out_e[0].astype(BF16), "norm_o": row(norm_o[0]),
        "w_in_o": w_in_o[0].astype(BF16),
        "gdn_params": (gdn_conv_w[0], row(gdn_a_log[0]), row(gdn_dt_bias[0]), row(gdn_norm_w[0])),
        "w_out_o": w_out_o[0].astype(BF16), "final_norm": row(final_norm),
    }
    batch, seq, _ = x_prompt.shape
    dec_batch, dec_seq, _ = x_sample.shape
    zeros = lambda s: jnp.zeros((1, batch) + s.shape[2:], F32)
    prompt = _run_group(x_prompt, jnp.arange(seq), zeros(state_ret), zeros(state_rwkv), zeros(state_shift),
                        zeros(state_gdn), zeros(state_conv),
                        _Geometry(math.gcd(seq, PROMPT_CHUNK), PROMPT_SEQS_PER_STEP, PROMPT_CHUNKS_PER_STEP,
                                  PROMPT_SEQS_PER_STEP), w)
    sample = _run_group(x_sample, PAST_LEN + jnp.arange(dec_seq), state_ret, state_rwkv, state_shift,
                        state_gdn, state_conv, _Geometry(dec_seq, SAMPLE_SEQS_PER_STEP, 1, SAMPLE_SEQS_JOINT), w)
    return (prompt[0], sample[0]) + prompt[1:] + sample[1:]
```

```python
import functools
import math
from typing import NamedTuple

import jax
import jax.numpy as jnp
from jax import lax
from jax.experimental import pallas as pl
from jax.experimental.pallas import tpu as pltpu

F32 = jnp.float32
BF16 = jnp.bfloat16

D_MODEL = 1024
PAST_LEN = 16384
RET_HEADS, RET_DK, RET_DV = 4, 64, 128
RET_ROPE_BASE = 10000.0
RET_Q = RET_HEADS * RET_DK
RET_V = RET_HEADS * RET_DV
RET_W = 2 * RET_Q + 2 * RET_V
RWKV_HEADS, RWKV_N = 8, 64
RWKV_W = RWKV_HEADS * RWKV_N
RWKV_LORA = 64
RWKV_GN_EPS = 64e-5
SHIFT_W = 3 * RWKV_W + 2 * RWKV_LORA
RWKV_IN = SHIFT_W + RWKV_W
GDN_HEADS, GDN_DK, GDN_DV, GDN_CONV = 8, 128, 128, 4
GDN_QK = GDN_HEADS * GDN_DK
GDN_VW = GDN_HEADS * GDN_DV
GDN_QKV = 2 * GDN_QK + GDN_VW
NORM_EPS = 1e-6
L2_EPS = 1e-12

SUBLANES = 8
INV_BLOCK = 16
VMEM_LIMIT = 56 * 1024 * 1024
PROMPT_CHUNK = 64
PROMPT_CHUNKS_PER_STEP = 2
PROMPT_SEQS_PER_STEP = 4
SAMPLE_SEQS_PER_STEP = 16
SAMPLE_SEQS_JOINT = 8
PREPARE_STAGES_PER_FINISH_STAGE = 3
EVEN_IN_TILE = 512
ODD_IN_TILE = 512


def _split_bf16(x):
    hi = x.astype(BF16)
    return hi, (x - hi.astype(F32)).astype(BF16)


def _dot(a, b, dims, mode):
    dot = lambda x, y: lax.dot_general(x, y, (dims, ((), ())), preferred_element_type=F32)
    if mode == "bf16":
        return dot(a.astype(BF16), b.astype(BF16))
    assert mode == "bf16x3", mode
    a_hi, a_lo = _split_bf16(a)
    b_hi, b_lo = _split_bf16(b)
    return (dot(a_lo, b_hi) + dot(a_hi, b_lo)) + dot(a_hi, b_hi)


def _mm(a, b, mode="bf16"):
    return _dot(a, b, ((1,), (0,)), mode)


def _mm_nt(a, b, mode="bf16"):
    return _dot(a, b, ((1,), (1,)), mode)


def _mm_tn(a, b, mode="bf16"):
    return _dot(a, b, ((0,), (0,)), mode)


def _mm3(a, b):
    return _mm(a, b, "bf16x3")


def _chunk_cumsum(tril, x):
    ones = tril.astype(BF16)
    hi = x.astype(BF16)
    rest = x - hi.astype(F32)
    mid = rest.astype(BF16)
    lo = (rest - mid.astype(F32)).astype(BF16)
    dot = lambda p: jnp.dot(ones, p, preferred_element_type=F32)
    return (dot(lo) + dot(mid)) + dot(hi)


def _silu(x):
    return x * jax.nn.sigmoid(x)


def _softplus(x):
    return jnp.maximum(x, 0.0) + jnp.log1p(jnp.exp(-jnp.abs(x)))


def _square_masks(n):
    ri = lax.broadcasted_iota(jnp.int32, (n, n), 0)
    ci = lax.broadcasted_iota(jnp.int32, (n, n), 1)
    return ri, ci


def _each(fn, *seqs):
    return [fn(*args) for args in zip(*seqs)]


def _block_rows(x, split):
    low_half = lax.broadcasted_iota(jnp.int32, x.shape, 1) < split
    return jnp.concatenate([jnp.where(low_half, x, 0.0), jnp.where(low_half, 0.0, x)], axis=0)


def _neumann_inverses(lows, eye, n, expand):
    invs = [eye - low for low in lows]
    powers = list(lows)
    k = 2
    while k < n:
        powers = _each(lambda p: _mm(p, expand(p)), powers)
        yield
        invs = _each(lambda inv, p: inv + _mm(inv, expand(p)), invs, powers)
        yield
        k *= 2
    return invs


def _unit_lower_inverses(lows, n, width=1):
    assert width in (1, 2)
    expand = (lambda x: x) if width == 1 else (lambda x: _block_rows(x, n))
    ri = lax.broadcasted_iota(jnp.int32, (n, width * n), 0)
    ci = lax.broadcasted_iota(jnp.int32, (n, width * n), 1) & (n - 1)
    eye = (ri == ci).astype(F32)
    if n <= INV_BLOCK:
        invs = yield from _neumann_inverses(lows, eye, n, expand)
    else:
        shift = int(math.log2(INV_BLOCK))
        same_block = (ri >> shift) == (ci >> shift)
        diag_invs = yield from _neumann_inverses([jnp.where(same_block, low, 0.0) for low in lows], eye, INV_BLOCK,
                                                 expand)
        offs = _each(lambda d, low: _mm(d, expand(jnp.where(same_block, 0.0, low))), diag_invs, lows)
        yield
        off_invs = yield from _neumann_inverses(offs, eye, n // INV_BLOCK, expand)
        invs = _each(lambda o, d: _mm(o, expand(d)), off_invs, diag_invs)
        yield
    residuals = _each(lambda low, inv: (eye - inv) - _mm3(low, expand(inv)), lows, invs)
    yield
    refined = _each(lambda inv, res: inv + _mm(inv, expand(res)), invs, residuals)
    yield
    return refined


def _rmsnorm_rows(x, w):
    return x * lax.rsqrt(jnp.mean(x * x, -1, keepdims=True) + NORM_EPS) * w


def _even_in_kernel(x_ref, nw_ref, w_ref, ret_ref, rwkv_ref):
    xn = _rmsnorm_rows(x_ref[...], nw_ref[...]).astype(BF16)
    ret_ref[...] = jnp.dot(xn, w_ref[:, :RET_W], preferred_element_type=F32)
    rwkv_ref[...] = jnp.dot(xn, w_ref[:, RET_W:], preferred_element_type=F32)


def _odd_in_kernel(h_ref, nw_ref, win_ref, qkv_ref, z_ref, ba_ref):
    xn = _rmsnorm_rows(h_ref[...], nw_ref[...]).astype(BF16)
    qkv_ref[...] = jnp.dot(xn, win_ref[:, :GDN_QKV], preferred_element_type=F32)
    z_ref[...] = jnp.dot(xn, win_ref[:, GDN_QKV:GDN_QKV + GDN_VW], preferred_element_type=F32)
    ba_ref[...] = jnp.dot(xn, win_ref[:, GDN_QKV + GDN_VW:], preferred_element_type=F32)


def _row_spec(tile, width):
    return pl.BlockSpec((tile, width), lambda i: (i, 0))


def _full_spec(shape):
    return pl.BlockSpec(shape, lambda i: (0,) * len(shape), pipeline_mode=pl.Buffered(1))


def _token_call(kernel, n_tokens, row_inputs, full_inputs, out_widths, order, name, tile):
    tile = min(tile, n_tokens)
    assert n_tokens % tile == 0
    specs = {**{k: _row_spec(tile, v.shape[1]) for k, v in row_inputs.items()},
             **{k: _full_spec(v.shape) for k, v in full_inputs.items()}}
    arrays = {**row_inputs, **full_inputs}
    return pl.pallas_call(
        kernel,
        grid=(n_tokens // tile,),
        in_specs=[specs[k] for k in order],
        out_specs=[_row_spec(tile, w) for w in out_widths],
        out_shape=[jax.ShapeDtypeStruct((n_tokens, w), F32) for w in out_widths],
        compiler_params=pltpu.CompilerParams(dimension_semantics=("parallel",), vmem_limit_bytes=VMEM_LIMIT),
        name=name,
    )(*[arrays[k] for k in order])


class _Geometry(NamedTuple):
    chunk: int
    regions: int
    chunks: int
    joint: int

    @property
    def region_rows(self):
        return self.chunks * self.chunk

    def rows(self, region, chunk_index, lanes=slice(None)):
        return region, slice(chunk_index * self.chunk, (chunk_index + 1) * self.chunk), lanes


def _recurrent_call(body, geo, batch, seq, rows, tables, states, consts, out_widths, scratch_shapes, name):
    steps = seq // geo.region_rows
    assert batch % geo.regions == 0 and steps * geo.region_rows == seq
    row_spec = lambda w: pl.BlockSpec((geo.regions, geo.region_rows, w), lambda b, c: (b, c, 0))
    table_spec = lambda a: pl.BlockSpec((geo.region_rows, a.shape[1]), lambda b, c: (c, 0))
    state_spec = lambda a: pl.BlockSpec((None, geo.regions) + a.shape[2:], lambda b, c: (0, b) + (0,) * (a.ndim - 2))
    const_spec = lambda a: pl.BlockSpec(a.shape, lambda b, c: (0,) * a.ndim)
    outs = pl.pallas_call(
        functools.partial(body, geo=geo),
        grid=(batch // geo.regions, steps),
        in_specs=[row_spec(a.shape[1]) for a in rows] + [table_spec(a) for a in tables]
        + [state_spec(a) for a in states] + [const_spec(a) for a in consts],
        out_specs=[row_spec(w) for w in out_widths] + [state_spec(a) for a in states],
        out_shape=[jax.ShapeDtypeStruct((batch, seq, w), F32) for w in out_widths]
        + [jax.ShapeDtypeStruct(a.shape, F32) for a in states],
        scratch_shapes=scratch_shapes,
        compiler_params=pltpu.CompilerParams(dimension_semantics=("parallel", "arbitrary"),
                                             vmem_limit_bytes=VMEM_LIMIT),
        name=name,
    )(*[a.reshape(batch, seq, a.shape[1]) for a in rows], *tables, *states, *consts)
    return [o.reshape(batch * seq, o.shape[2]) for o in outs[:len(out_widths)]] + list(outs[len(out_widths):])


def _run_step(geo, prepare, finish):
    groups = [list(range(g, g + geo.joint)) for g in range(0, geo.regions, geo.joint)]
    items = [(group, ci) for group in groups for ci in range(geo.chunks)]

    def advance(gen, stages):
        for _ in range(stages):
            try:
                next(gen)
            except StopIteration as stop:
                return True, stop.value
        return False, None

    _, prepared = advance(prepare(*items[0]), 10 ** 6)
    for n, item in enumerate(items):
        finishing = finish(*item, prepared)
        preparing = prepare(*items[n + 1]) if n + 1 < len(items) else None
        finished, prepared_next = False, preparing is None
        prepared = None
        while not (finished and prepared_next):
            if not finished:
                finished, _ = advance(finishing, 1)
            if not prepared_next:
                prepared_next, prepared = advance(preparing, PREPARE_STAGES_PER_FINISH_STAGE)


def _ret_stages(p_ref, cos_ref, sin_ref, s0_ref, dec_ref, qs_ref, ks_ref, ss_ref, nw_ref, s_ref, geo):
    @pl.when(pl.program_id(1) == 0)
    def _():
        s_ref[...] = s0_ref[...]

    lane = lax.broadcasted_iota(jnp.int32, (geo.chunk, RET_Q), 1)
    even = (lane & 1) == 0
    heads = range(RET_HEADS)
    k_slices = [slice(h * RET_DK, (h + 1) * RET_DK) for h in heads]
    v_slices = [slice(h * RET_DV, (h + 1) * RET_DV) for h in heads]

    def rotary(x, cos, sin):
        partner = jnp.where(even, pltpu.roll(x, RET_Q - 1, 1), pltpu.roll(x, 1, 1))
        return x * cos + partner * sin

    def prepare(group, ci):
        q_l, k_l, qs_l, ks_l, v_l, dec_l, ss_l, post = [], [], [], [], [], [], [], []
        for reg in group:
            rows = geo.rows(reg, ci)
            p = p_ref[rows]
            cos, sin = cos_ref[rows[1], :], sin_ref[rows[1], :]
            q = rotary(p[:, :RET_Q], cos, sin)
            k = rotary(p[:, RET_Q:2 * RET_Q], cos, sin) * (RET_DK ** -0.5)
            q_state = q * qs_ref[...]
            k_state = k * ks_ref[...]
            for h in heads:
                q_l.append(q[:, k_slices[h]])
                k_l.append(k[:, k_slices[h]])
                qs_l.append(q_state[:, k_slices[h]])
                ks_l.append(k_state[:, k_slices[h]])
                v_l.append(p[:, 2 * RET_Q + h * RET_DV:2 * RET_Q + (h + 1) * RET_DV])
                dec_l.append(dec_ref[h])
                ss_l.append(ss_ref[h])
                post.append((rows, v_slices[h], p[:, 2 * RET_Q + RET_V + h * RET_DV:
                                                  2 * RET_Q + RET_V + (h + 1) * RET_DV]))
        scores = _each(lambda q_, k_, d: _mm_nt(q_, k_) * d, q_l, k_l, dec_l)
        yield
        intra = _each(_mm, scores, v_l)
        yield
        outer = _each(_mm_tn, ks_l, v_l)
        yield
        return qs_l, ss_l, post, intra, outer

    states = {}

    def finish(group, ci, prepared):
        qs_l, ss_l, post, intra, outer = prepared
        key = group[0]
        if ci == 0:
            states[key] = [s_ref[reg, h] for reg in group for h in heads]
        from_state = _each(_mm, qs_l, states[key])
        yield
        states[key] = _each(lambda s, ss, o: s * ss + o, states[key], ss_l, outer)
        o_l = _each(jnp.add, from_state, intra)
        ms_l = [jnp.mean(o * o, -1, keepdims=True) for o in o_l]
        if ci == geo.chunks - 1:
            for i, reg in enumerate(group):
                for h in heads:
                    s_ref[reg, h] = states[key][i * RET_HEADS + h]
        return [o * lax.rsqrt(ms + NORM_EPS) * nw_ref[:, vs_] * _silu(gate)
                for (rows, vs_, gate), o, ms in zip(post, o_l, ms_l)]

    return prepare, finish


def _ret_tables(pos, geo):
    chunk = geo.chunk
    half = RET_DK // 2
    inv = 1.0 / (RET_ROPE_BASE ** jnp.linspace(0.0, 1.0, half, dtype=F32))
    ang = pos.astype(F32)[:, None] * inv[None, :]
    cos = jnp.repeat(jnp.cos(ang), 2, axis=-1)
    sin = jnp.stack([-jnp.sin(ang), jnp.sin(ang)], -1).reshape(ang.shape[0], RET_DK)
    cos, sin = jnp.tile(cos, (1, RET_HEADS)), jnp.tile(sin, (1, RET_HEADS))
    lg = jnp.log1p(-jnp.exp2(-jnp.linspace(5.0, 12.0, RET_HEADS, dtype=F32)))
    ci = jnp.arange(chunk, dtype=F32)
    diff = ci[:, None] - ci[None, :]
    causal = diff >= 0
    decay = jnp.where(causal, jnp.exp(jnp.where(causal, diff, 0.0) * lg[:, None, None]), 0.0)
    q_scale = jnp.exp((ci + 1.0) * lg[:, None])
    k_scale = jnp.exp((chunk - 1.0 - ci) * lg[:, None])
    s_scale = jnp.exp(chunk * lg)
    widen = lambda t: jnp.repeat(t.T, RET_DK, axis=1)
    s_scale = jnp.broadcast_to(s_scale[:, None, None], (RET_HEADS, 1, RET_DV))
    return cos, sin, decay, widen(q_scale), widen(k_scale), s_scale


def _rwkv_stages(p_ref, s0_ref, sh0_ref, mu_ref, w0_ref, w2_ref, a0_ref, a2_ref, kk_ref, ka_ref, rk_ref,
                 lnw_ref, lnb_ref, s_ref, shout_ref, geo):
    chunk = geo.chunk
    pair_w = 2 * RWKV_N
    n_pairs = RWKV_HEADS // 2

    @pl.when(pl.program_id(1) == 0)
    def _():
        s_ref[...] = s0_ref[...]
        shout_ref[...] = sh0_ref[...]

    ri = lax.broadcasted_iota(jnp.int32, (chunk, 2 * chunk), 0)
    ci_ = lax.broadcasted_iota(jnp.int32, (chunk, 2 * chunk), 1) & (chunk - 1)
    tril = ri >= ci_
    strict = ri > ci_
    sq_r, sq_c = _square_masks(chunk)
    tril_one = sq_r >= sq_c
    first_row = lax.broadcasted_iota(jnp.int32, (chunk, SHIFT_W), 0) == 0
    head0 = lax.broadcasted_iota(jnp.int32, (chunk, pair_w), 1) < RWKV_N
    pr, pc = _square_masks(pair_w)
    pair_eye = pr == pc
    same_head = (pr < RWKV_N) == (pc < RWKV_N)
    pairs = range(n_pairs)
    lanes = [slice(p * pair_w, (p + 1) * pair_w) for p in pairs]
    per_pair = lambda t: [t[:, s] for s in lanes]
    blocks = lambda x: _block_rows(x, RWKV_N)

    def head_sums(x):
        first = jnp.sum(jnp.where(head0, x, 0.0), -1, keepdims=True)
        second = jnp.sum(jnp.where(head0, 0.0, x), -1, keepdims=True)
        return jnp.where(head0, first, second)

    def prepare(group, ci):
        r_p, k_p, v_p, kk_p, b_p, cum_p, ld_p, bonus_p, post = [], [], [], [], [], [], [], [], []
        for reg in group:
            rows = geo.rows(reg, ci)
            p = p_ref[rows]
            sh = p[:, :SHIFT_W]
            before = shout_ref[reg] if ci == 0 else p_ref[reg, rows[1].start - 1:rows[1].start, :SHIFT_W]
            prev = jnp.where(first_row, before, pltpu.roll(sh, 1, 0))
            if ci == geo.chunks - 1:
                shout_ref[reg] = sh[chunk - 1:chunk, :]
            xs = sh + (prev - sh) * mu_ref[...]
            r = xs[:, :RWKV_W]
            k_in = xs[:, RWKV_W:2 * RWKV_W]
            v = xs[:, 2 * RWKV_W:3 * RWKV_W]
            wd = xs[:, 3 * RWKV_W:3 * RWKV_W + RWKV_LORA]
            ad = xs[:, 3 * RWKV_W + RWKV_LORA:]
            w = -_softplus(-(w0_ref[...] + _mm3(jnp.tanh(wd), w2_ref[...]))) - 0.5
            log_decay = -jnp.exp(w)
            a = jax.nn.sigmoid(a0_ref[...] + _mm3(ad, a2_ref[...]))
            k = k_in * (1.0 + (a - 1.0) * ka_ref[...])
            kk = [x * lax.rsqrt(head_sums(x * x) + L2_EPS) for x in per_pair(k_in * kk_ref[...])]
            r_p += per_pair(r)
            k_p += per_pair(k)
            v_p += per_pair(v)
            kk_p += kk
            b_p += _each(jnp.multiply, kk, per_pair(a))
            cum_p += per_pair(_chunk_cumsum(tril_one, log_decay))
            ld_p += per_pair(log_decay)
            bonus_p += [head_sums(x) for x in per_pair(r * k * rk_ref[...])]
            post += [(rows, lanes[q], p[:, SHIFT_W + q * pair_w:SHIFT_W + (q + 1) * pair_w]) for q in pairs]
        yield
        cum_last = [c[chunk - 1:chunk, :] for c in cum_p]
        grow = [jnp.exp(-c) for c in cum_p]
        tail = _each(lambda cl, c: jnp.exp(cl - c), cum_last, cum_p)
        lhs = _each(lambda kk, r_, c, ld: jnp.concatenate([kk * jnp.exp(c - ld), r_ * jnp.exp(c)], axis=0),
                    kk_p, r_p, cum_p, ld_p)
        g_b = _each(lambda x, b, g: _mm_nt(x, blocks(b * g)), lhs, b_p, grow)
        yield
        inverting = _unit_lower_inverses([jnp.where(strict, g[:chunk], 0.0) for g in g_b], chunk, width=2)
        g_k = _each(lambda x, k_, g: _mm_nt(x, blocks(k_ * g)), lhs, k_p, grow)
        yield
        v_blocks = [blocks(v_) for v_ in v_p]
        from_v = _each(lambda g, vb: _mm(jnp.where(strict, g[:chunk], 0.0), vb), g_k, v_blocks)
        yield
        y_v = _each(lambda g, vb: _mm(jnp.where(tril, g[chunk:], 0.0), vb), g_k, v_blocks)
        yield
        m_rb = [jnp.where(tril, g[chunk:], 0.0) for g in g_b]
        k_tail_t = _each(lambda k_, b, t: jnp.concatenate([k_ * t, b * t], axis=0).T, k_p, b_p, tail)
        state_decay = [jnp.sum(jnp.where(pair_eye, jnp.exp(cl), 0.0), axis=1, keepdims=True) for cl in cum_last]
        inverses = yield from inverting
        return lhs, inverses, from_v, y_v, m_rb, k_tail_t, v_p, state_decay, bonus_p, post

    states = {}

    def load_state(reg, q):
        zero = jnp.zeros((RWKV_N, RWKV_N), F32)
        return jnp.concatenate([jnp.concatenate([s_ref[reg, 2 * q].T, zero], axis=1),
                                jnp.concatenate([zero, s_ref[reg, 2 * q + 1].T], axis=1)], axis=0)

    def finish(group, ci, prepared):
        lhs, inverses, from_v, y_v, m_rb, k_tail_t, v_p, state_decay, bonus_p, post = prepared
        key = group[0]
        if ci == 0:
            states[key] = [load_state(reg, q) for reg in group for q in pairs]
        from_state = _each(_mm, lhs, states[key])
        yield
        u = _each(lambda inv, fs, fv: _mm(inv, blocks(fs[:chunk] + fv)), inverses, from_state, from_v)
        yield
        outer = _each(lambda kt, v_, u_: _mm(kt, jnp.concatenate([v_, -u_], axis=0)), k_tail_t, v_p, u)
        yield
        y_u = _each(lambda m, u_: _mm(m, blocks(u_)), m_rb, u)
        yield
        states[key] = _each(lambda s, d, o: s * d + jnp.where(same_head, o, 0.0), states[key], state_decay, outer)
        y_l = _each(lambda fs, yv, yu: fs[chunk:] + yv - yu, from_state, y_v, y_u)
        mean_l = [head_sums(y) * (1.0 / RWKV_N) for y in y_l]
        cen_l = _each(jnp.subtract, y_l, mean_l)
        var_l = [head_sums(jnp.square(c)) * (1.0 / RWKV_N) for c in cen_l]
        gated = []
        for (rows, ls, gate), cen, var, bonus, v_ in zip(post, cen_l, var_l, bonus_p, v_p):
            y = cen * lax.rsqrt(var + RWKV_GN_EPS) * lnw_ref[:, ls] + lnb_ref[:, ls]
            gated.append((y + bonus * v_) * _silu(gate))
        if ci == geo.chunks - 1:
            for i, reg in enumerate(group):
                for q in pairs:
                    state = states[key][i * n_pairs + q]
                    s_ref[reg, 2 * q] = state[:RWKV_N, :RWKV_N].T
                    s_ref[reg, 2 * q + 1] = state[RWKV_N:, RWKV_N:].T
        return gated

    return prepare, finish


def _alternate(*generators):
    values = [None] * len(generators)
    live = list(range(len(generators)))
    while live:
        for n in list(live):
            try:
                next(generators[n])
            except StopIteration as stop:
                values[n] = stop.value
                live.remove(n)
        yield
    return values


N_RET_CONSTS = 5
N_RWKV_CONSTS = 10


def _even_mixers_kernel(p_ret_ref, p_rwkv_ref, h_ref, cos_ref, sin_ref, s0_ret_ref, s0_rwkv_ref, sh0_ref, *rest, geo):
    ret_consts, rest = rest[:N_RET_CONSTS], rest[N_RET_CONSTS:]
    rwkv_consts, rest = rest[:N_RWKV_CONSTS], rest[N_RWKV_CONSTS:]
    wout_ref, h1_ref, s_ret_ref, s_rwkv_ref, shout_ref = rest
    ret_prepare, ret_finish = _ret_stages(p_ret_ref, cos_ref, sin_ref, s0_ret_ref, *ret_consts, s_ret_ref, geo)
    rwkv_prepare, rwkv_finish = _rwkv_stages(p_rwkv_ref, s0_rwkv_ref, sh0_ref, *rwkv_consts, s_rwkv_ref, shout_ref,
                                             geo)
    n_pairs = RWKV_HEADS // 2

    def prepare(group, ci):
        return _alternate(rwkv_prepare(group, ci), ret_prepare(group, ci))

    def finish(group, ci, prepared):
        rwkv_out, ret_out = yield from _alternate(rwkv_finish(group, ci, prepared[0]),
                                                  ret_finish(group, ci, prepared[1]))
        mixed = jnp.concatenate([jnp.concatenate(ret_out[i * RET_HEADS:(i + 1) * RET_HEADS]
                                                 + rwkv_out[i * n_pairs:(i + 1) * n_pairs], axis=1)
                                 for i in range(len(group))], axis=0)
        out = jnp.dot(mixed.astype(BF16), wout_ref[...], preferred_element_type=F32)
        for i, reg in enumerate(group):
            rows = geo.rows(reg, ci)
            h1_ref[rows] = h_ref[rows] + out[i * geo.chunk:(i + 1) * geo.chunk]

    _run_step(geo, prepare, finish)


def _even_mixers(p_ret, p_rwkv, h0, pos, s_ret, s_rwkv, shift0, ret_norm_w, rwkv_params, w_out, batch, seq, geo):
    cos, sin, decay, q_scale, k_scale, s_scale = _ret_tables(pos, geo)
    ret_consts = [decay, q_scale, k_scale, s_scale, ret_norm_w]
    assert len(ret_consts) == N_RET_CONSTS and len(rwkv_params) == N_RWKV_CONSTS
    h1, ret_new, rwkv_new, shift = _recurrent_call(
        _even_mixers_kernel, geo, batch, seq, [p_ret, p_rwkv, h0], [cos, sin],
        [s_ret, s_rwkv, shift0.reshape(1, batch, 1, SHIFT_W)], ret_consts + list(rwkv_params) + [w_out],
        [D_MODEL], [], "even_mixers")
    return h1, ret_new, rwkv_new, shift.reshape(1, batch, SHIFT_W)


def _gdn_kernel(qkv_ref, z_ref, ba_ref, h_ref, s0_ref, c0_ref, cw_ref, alog_ref, dtb_ref, gnw_ref, wout_ref, fnw_ref,
                y_ref, s_ref, cout_ref, xbuf_ref, *, geo):
    chunk = geo.chunk
    taps = GDN_CONV - 1
    region_rows = geo.region_rows
    stride = region_rows + SUBLANES

    @pl.when(pl.program_id(1) == 0)
    def _():
        s_ref[...] = s0_ref[...]
        cout_ref[...] = c0_ref[...]

    ri = lax.broadcasted_iota(jnp.int32, (chunk, 2 * chunk), 0)
    lane = lax.broadcasted_iota(jnp.int32, (chunk, 2 * chunk), 1)
    ci_ = lane & (chunk - 1)
    first_half = lane < chunk
    tril = ri >= ci_
    strict = ri > ci_
    eye = ri == ci_
    sq_r, sq_c = _square_masks(chunk)
    tril_one = sq_r >= sq_c
    heads = range(GDN_HEADS)
    head_slice = lambda base, h: slice(base + h * GDN_DK, base + (h + 1) * GDN_DK)
    l2 = lambda x: x * lax.rsqrt(jnp.sum(x * x, -1, keepdims=True) + L2_EPS)

    for reg in range(geo.regions):
        base = reg * stride + SUBLANES
        xbuf_ref[base - taps:base, :] = cout_ref[reg]
        xbuf_ref[base:base + region_rows, :] = qkv_ref[reg]
        cout_ref[reg] = qkv_ref[reg, region_rows - taps:region_rows, :]

    def prepare(group, ci):
        q_h, k_h, v_h, beta_h, gc, post = [], [], [], [], [], []
        for reg in group:
            rows = geo.rows(reg, ci)
            at = reg * stride + SUBLANES + ci * chunk
            conv = xbuf_ref[at:at + chunk, :] * cw_ref[taps:taps + 1, :]
            for j in range(taps):
                conv = conv + xbuf_ref[at - taps + j:at - taps + j + chunk, :] * cw_ref[j:j + 1, :]
            act = _silu(conv)
            ba = ba_ref[rows]
            beta = jax.nn.sigmoid(ba[:, :GDN_HEADS])
            g = -jnp.exp(alog_ref[...]) * _softplus(ba[:, GDN_HEADS:] + dtb_ref[...])
            gcum = _chunk_cumsum(tril_one, g)
            q_h += [l2(act[:, head_slice(0, h)]) * (GDN_DK ** -0.5) for h in heads]
            k_h += [l2(act[:, head_slice(GDN_QK, h)]) for h in heads]
            v_h += [act[:, head_slice(2 * GDN_QK, h)] for h in heads]
            beta_h += [beta[:, h:h + 1] for h in heads]
            gc += [gcum[:, h:h + 1] for h in heads]
            post += [(rows, head_slice(0, h)) for h in heads]
        yield
        pair = lambda t: [jnp.concatenate(t[i:i + 2], axis=1) for i in range(0, len(t), 2)]
        kb = _each(jnp.multiply, k_h, beta_h)
        exp_gc = [jnp.exp(g_) for g_ in gc]
        g_last = [g_[chunk - 1:chunk, :] for g_ in gc]
        gc_col = [jnp.where(first_half, gc[i], gc[i + 1]) for i in range(0, len(gc), 2)]
        gc_row = [jnp.sum(jnp.where(eye, g_, 0.0), axis=0, keepdims=True) for g_ in gc_col]
        decay = _each(lambda c, r: jnp.where(tril, jnp.exp(jnp.where(tril, c - r, 0.0)), 0.0), gc_col, gc_row)
        k_blocks = [_block_rows(k_, GDN_DK) for k_ in pair(k_h)]
        lower = _each(lambda kb_, kbl, d: jnp.where(strict, _mm_nt(kb_, kbl) * d, 0.0), pair(kb), k_blocks, decay)
        yield
        attn = _each(lambda q_, kbl, d: _mm_nt(q_, kbl) * d, pair(q_h), k_blocks, decay)
        yield
        q_state = _each(jnp.multiply, q_h, exp_gc)
        k_tail_t = _each(lambda k_, gl, g_: (k_ * jnp.exp(gl - g_)).T, k_h, g_last, gc)
        inverses = yield from _unit_lower_inverses(lower, chunk, width=2)
        rhs = _each(lambda v_, b, kb_, e: jnp.concatenate([v_ * b, kb_ * e], axis=1), v_h, beta_h, kb, exp_gc)
        sol = _each(lambda inv, r_: _mm(inv, _block_rows(r_, GDN_DV + GDN_DK)), inverses, pair(rhs))
        yield
        return q_state, sol, attn, k_tail_t, [jnp.exp(gl) for gl in g_last], post

    states = {}

    def finish(group, ci, prepared):
        q_state, sol, attn, k_tail_t, state_decay, post = prepared
        key = group[0]
        if ci == 0:
            states[key] = [s_ref[reg, h] for reg in group for h in heads]
        width = GDN_DV + GDN_DK
        u_w = [s_[:, i * width:(i + 1) * width] for s_ in sol for i in range(2)]
        v_new = _each(lambda s_, state: s_[:, :GDN_DV] - _mm(s_[:, GDN_DV:], state), u_w, states[key])
        yield
        o_state = _each(_mm, q_state, states[key])
        yield
        outer = _each(_mm, k_tail_t, v_new)
        yield
        v_pairs = [jnp.concatenate(v_new[i:i + 2], axis=1) for i in range(0, len(v_new), 2)]
        o_pairs = _each(lambda a_, v_: _mm(a_, _block_rows(v_, GDN_DV)), attn, v_pairs)
        o_intra = [o_[:, i * GDN_DV:(i + 1) * GDN_DV] for o_ in o_pairs for i in range(2)]
        yield
        states[key] = _each(lambda s, d, o: s * d + o, states[key], state_decay, outer)
        o_l = _each(jnp.add, o_state, o_intra)
        ms_l = [jnp.mean(o * o, -1, keepdims=True) for o in o_l]
        gated = [o * lax.rsqrt(ms + NORM_EPS) * gnw_ref[...] * _silu(z_ref[rows[0], rows[1], os_])
                 for (rows, os_), o, ms in zip(post, o_l, ms_l)]
        yield
        mixed = jnp.concatenate([jnp.concatenate(gated[i:i + GDN_HEADS], axis=1)
                                 for i in range(0, len(gated), GDN_HEADS)], axis=0)
        out = jnp.dot(mixed.astype(BF16), wout_ref[...], preferred_element_type=F32)
        for i, reg in enumerate(group):
            rows = geo.rows(reg, ci)
            y_ref[rows] = _rmsnorm_rows(h_ref[rows] + out[i * chunk:(i + 1) * chunk], fnw_ref[...])
        if ci == geo.chunks - 1:
            for i, reg in enumerate(group):
                for h in heads:
                    s_ref[reg, h] = states[key][i * GDN_HEADS + h]

    _run_step(geo, prepare, finish)


def _gdn(qkv, z, ba, h1, s0, conv0, params, w_out, final_norm, batch, seq, geo):
    xbuf = pltpu.VMEM((geo.regions * (geo.region_rows + SUBLANES), GDN_QKV), F32)
    return _recurrent_call(_gdn_kernel, geo, batch, seq, [qkv, z, ba, h1], [], [s0, conv0],
                           list(params) + [w_out, final_norm], [D_MODEL], [xbuf], "gated_delta")


def _run_group(x, pos, s_ret, s_rwkv, s_shift, s_gdn, s_conv, geo, w):
    batch, seq, _ = x.shape
    n_tokens = batch * seq
    h0 = x.reshape(n_tokens, D_MODEL)
    p_ret, p_rwkv = _token_call(
        _even_in_kernel, n_tokens, {"x": h0}, {"nw": w["norm_e"], "w": w["w_in_e"]},
        [RET_W, RWKV_IN], ["x", "nw", "w"], "even_in", EVEN_IN_TILE)
    h1, ret_new, rwkv_new, shift_new = _even_mixers(
        p_ret, p_rwkv, h0, pos, s_ret, s_rwkv, s_shift, w["ret_norm_w"], w["rwkv_params"], w["w_out_e"],
        batch, seq, geo)
    qkv, z, ba = _token_call(
        _odd_in_kernel, n_tokens, {"h": h1}, {"nw": w["norm_o"], "win": w["w_in_o"]},
        [GDN_QKV, GDN_VW, 2 * GDN_HEADS], ["h", "nw", "win"], "odd_in", ODD_IN_TILE)
    y, gdn_new, conv_new = _gdn(qkv, z, ba, h1, s_gdn, s_conv, w["gdn_params"], w["w_out_o"], w["final_norm"],
                                batch, seq, geo)
    return y.reshape(batch, seq, D_MODEL), ret_new, rwkv_new, shift_new, gdn_new, conv_new


def kernel(x_prompt, x_sample, state_ret, state_rwkv, state_shift, state_gdn, state_conv, norm_e, w_in_e, rwkv_mu, rwkv_w0, rwkv_w2, rwkv_a0, rwkv_a2, rwkv_kk, rwkv_ka, rwkv_rk, rwkv_ln_w, rwkv_ln_b, ret_norm_w, w_out_e, norm_o, w_in_o, gdn_conv_w, gdn_a_log, gdn_dt_bias, gdn_norm_w, w_out_o, final_norm):
    assert state_ret.shape[0] == 1 and state_gdn.shape[0] == 1, "one even and one odd layer"
    row = lambda a: a.reshape(1, -1)
    w = {
        "norm_e": row(norm_e[0]), "w_in_e": w_in_e[0].astype(BF16), "ret_norm_w": row(ret_norm_w[0]),
        "rwkv_params": (row(rwkv_mu[0]), row(rwkv_w0[0]), rwkv_w2[0], row(rwkv_a0[0]), rwkv_a2[0], row(rwkv_kk[0]),
                        row(rwkv_ka[0]), row(rwkv_rk[0]), row(rwkv_ln_w[0]), row(rwkv_ln_b[0])),
        "w_out_e": w_out_e[0].astype(BF16), "norm_o": row(norm_o[0]),
        "w_in_o": w_in_o[0].astype(BF16),
        "gdn_params": (gdn_conv_w[0], row(gdn_a_log[0]), row(gdn_dt_bias[0]), row(gdn_norm_w[0])),
        "w_out_o": w_out_o[0].astype(BF16), "final_norm": row(final_norm),
    }
    batch, seq, _ = x_prompt.shape
    dec_batch, dec_seq, _ = x_sample.shape
    zeros = lambda s: jnp.zeros((1, batch) + s.shape[2:], F32)
    prompt = _run_group(x_prompt, jnp.arange(seq), zeros(state_ret), zeros(state_rwkv), zeros(state_shift),
                        zeros(state_gdn), zeros(state_conv),
                        _Geometry(math.gcd(seq, PROMPT_CHUNK), PROMPT_SEQS_PER_STEP, PROMPT_CHUNKS_PER_STEP,
                                  PROMPT_SEQS_PER_STEP), w)
    sample = _run_group(x_sample, PAST_LEN + jnp.arange(dec_seq), state_ret, state_rwkv, state_shift,
                        state_gdn, state_conv, _Geometry(dec_seq, SAMPLE_SEQS_PER_STEP, 1, SAMPLE_SEQS_JOINT), w)
    return (prompt[0], sample[0]) + prompt[1:] + sample[1:]
```

```python
import functools
import math
from typing import NamedTuple

import jax
import jax.numpy as jnp
from jax import lax
from jax.experimental import pallas as pl
from jax.experimental.pallas import tpu as pltpu

F32 = jnp.float32
BF16 = jnp.bfloat16

D_MODEL = 1024
PAST_LEN = 16384
RET_HEADS, RET_DK, RET_DV = 4, 64, 128
RET_ROPE_BASE = 10000.0
RET_Q = RET_HEADS * RET_DK
RET_V = RET_HEADS * RET_DV
RET_W = 2 * RET_Q + 2 * RET_V
RWKV_HEADS, RWKV_N = 8, 64
RWKV_W = RWKV_HEADS * RWKV_N
RWKV_LORA = 64
RWKV_GN_EPS = 64e-5
SHIFT_W = 3 * RWKV_W + 2 * RWKV_LORA
RWKV_IN = SHIFT_W + RWKV_W
GDN_HEADS, GDN_DK, GDN_DV, GDN_CONV = 8, 128, 128, 4
GDN_QK = GDN_HEADS * GDN_DK
GDN_VW = GDN_HEADS * GDN_DV
GDN_QKV = 2 * GDN_QK + GDN_VW
NORM_EPS = 1e-6
L2_EPS = 1e-12

SUBLANES = 8
INV_BLOCK = 16
VMEM_LIMIT = 56 * 1024 * 1024
PROMPT_CHUNK = 64
PROMPT_CHUNKS_PER_STEP = 2
PROMPT_SEQS_PER_STEP = 4
SAMPLE_SEQS_PER_STEP = 16
SAMPLE_SEQS_JOINT = 8
PREPARE_STAGES_PER_FINISH_STAGE = 3
EVEN_IN_TILE = 512
ODD_IN_TILE = 512
ODD_OUT_TILE = 1024


def _split_bf16(x):
    hi = x.astype(BF16)
    return hi, (x - hi.astype(F32)).astype(BF16)


def _dot(a, b, dims, mode):
    dot = lambda x, y: lax.dot_general(x, y, (dims, ((), ())), preferred_element_type=F32)
    if mode == "bf16":
        return dot(a.astype(BF16), b.astype(BF16))
    assert mode == "bf16x3", mode
    a_hi, a_lo = _split_bf16(a)
    b_hi, b_lo = _split_bf16(b)
    return (dot(a_lo, b_hi) + dot(a_hi, b_lo)) + dot(a_hi, b_hi)


def _mm(a, b, mode="bf16"):
    return _dot(a, b, ((1,), (0,)), mode)


def _mm_nt(a, b, mode="bf16"):
    return _dot(a, b, ((1,), (1,)), mode)


def _mm_tn(a, b, mode="bf16"):
    return _dot(a, b, ((0,), (0,)), mode)


def _mm3(a, b):
    return _mm(a, b, "bf16x3")


def _chunk_cumsum(tril, x):
    ones = tril.astype(BF16)
    hi = x.astype(BF16)
    rest = x - hi.astype(F32)
    mid = rest.astype(BF16)
    lo = (rest - mid.astype(F32)).astype(BF16)
    dot = lambda p: jnp.dot(ones, p, preferred_element_type=F32)
    return (dot(lo) + dot(mid)) + dot(hi)


def _silu(x):
    return x * jax.nn.sigmoid(x)


def _softplus(x):
    return jnp.maximum(x, 0.0) + jnp.log1p(jnp.exp(-jnp.abs(x)))


def _square_masks(n):
    ri = lax.broadcasted_iota(jnp.int32, (n, n), 0)
    ci = lax.broadcasted_iota(jnp.int32, (n, n), 1)
    return ri, ci


def _each(fn, *seqs):
    return [fn(*args) for args in zip(*seqs)]


def _block_rows(x, split):
    low_half = lax.broadcasted_iota(jnp.int32, x.shape, 1) < split
    return jnp.concatenate([jnp.where(low_half, x, 0.0), jnp.where(low_half, 0.0, x)], axis=0)


def _neumann_inverses(lows, eye, n, expand):
    invs = [eye - low for low in lows]
    powers = list(lows)
    k = 2
    while k < n:
        powers = _each(lambda p: _mm(p, expand(p)), powers)
        yield
        invs = _each(lambda inv, p: inv + _mm(inv, expand(p)), invs, powers)
        yield
        k *= 2
    return invs


def _unit_lower_inverses(lows, n, width=1):
    assert width in (1, 2)
    expand = (lambda x: x) if width == 1 else (lambda x: _block_rows(x, n))
    ri = lax.broadcasted_iota(jnp.int32, (n, width * n), 0)
    ci = lax.broadcasted_iota(jnp.int32, (n, width * n), 1) & (n - 1)
    eye = (ri == ci).astype(F32)
    if n <= INV_BLOCK:
        invs = yield from _neumann_inverses(lows, eye, n, expand)
    else:
        shift = int(math.log2(INV_BLOCK))
        same_block = (ri >> shift) == (ci >> shift)
        diag_invs = yield from _neumann_inverses([jnp.where(same_block, low, 0.0) for low in lows], eye, INV_BLOCK,
                                                 expand)
        offs = _each(lambda d, low: _mm(d, expand(jnp.where(same_block, 0.0, low))), diag_invs, lows)
        yield
        off_invs = yield from _neumann_inverses(offs, eye, n // INV_BLOCK, expand)
        invs = _each(lambda o, d: _mm(o, expand(d)), off_invs, diag_invs)
        yield
    residuals = _each(lambda low, inv: (eye - inv) - _mm3(low, expand(inv)), lows, invs)
    yield
    refined = _each(lambda inv, res: inv + _mm(inv, expand(res)), invs, residuals)
    yield
    return refined


def _rmsnorm_rows(x, w):
    return x * lax.rsqrt(jnp.mean(x * x, -1, keepdims=True) + NORM_EPS) * w


def _even_in_kernel(x_ref, nw_ref, w_ref, ret_ref, rwkv_ref):
    xn = _rmsnorm_rows(x_ref[...], nw_ref[...]).astype(BF16)
    ret_ref[...] = jnp.dot(xn, w_ref[:, :RET_W], preferred_element_type=F32)
    rwkv_ref[...] = jnp.dot(xn, w_ref[:, RET_W:], preferred_element_type=F32)


def _odd_in_kernel(h_ref, nw_ref, win_ref, qkv_ref, z_ref, ba_ref):
    xn = _rmsnorm_rows(h_ref[...], nw_ref[...]).astype(BF16)
    qkv_ref[...] = jnp.dot(xn, win_ref[:, :GDN_QKV], preferred_element_type=F32)
    z_ref[...] = jnp.dot(xn, win_ref[:, GDN_QKV:GDN_QKV + GDN_VW], preferred_element_type=F32)
    ba_ref[...] = jnp.dot(xn, win_ref[:, GDN_QKV + GDN_VW:], preferred_element_type=F32)


def _odd_out_final_kernel(h_ref, o_ref, wout_ref, nw_ref, y_ref):
    h2 = h_ref[...] + jnp.dot(o_ref[...].astype(BF16), wout_ref[...], preferred_element_type=F32)
    y_ref[...] = _rmsnorm_rows(h2, nw_ref[...])


def _row_spec(tile, width):
    return pl.BlockSpec((tile, width), lambda i: (i, 0))


def _full_spec(shape):
    return pl.BlockSpec(shape, lambda i: (0,) * len(shape), pipeline_mode=pl.Buffered(1))


def _token_call(kernel, n_tokens, row_inputs, full_inputs, out_widths, order, name, tile):
    tile = min(tile, n_tokens)
    assert n_tokens % tile == 0
    specs = {**{k: _row_spec(tile, v.shape[1]) for k, v in row_inputs.items()},
             **{k: _full_spec(v.shape) for k, v in full_inputs.items()}}
    arrays = {**row_inputs, **full_inputs}
    return pl.pallas_call(
        kernel,
        grid=(n_tokens // tile,),
        in_specs=[specs[k] for k in order],
        out_specs=[_row_spec(tile, w) for w in out_widths],
        out_shape=[jax.ShapeDtypeStruct((n_tokens, w), F32) for w in out_widths],
        compiler_params=pltpu.CompilerParams(dimension_semantics=("parallel",), vmem_limit_bytes=VMEM_LIMIT),
        name=name,
    )(*[arrays[k] for k in order])


class _Geometry(NamedTuple):
    chunk: int
    regions: int
    chunks: int
    joint: int

    @property
    def region_rows(self):
        return self.chunks * self.chunk

    def rows(self, region, chunk_index, lanes=slice(None)):
        return region, slice(chunk_index * self.chunk, (chunk_index + 1) * self.chunk), lanes


def _recurrent_call(body, geo, batch, seq, rows, tables, states, consts, out_widths, scratch_shapes, name):
    steps = seq // geo.region_rows
    assert batch % geo.regions == 0 and steps * geo.region_rows == seq
    row_spec = lambda w: pl.BlockSpec((geo.regions, geo.region_rows, w), lambda b, c: (b, c, 0))
    table_spec = lambda a: pl.BlockSpec((geo.region_rows, a.shape[1]), lambda b, c: (c, 0))
    state_spec = lambda a: pl.BlockSpec((None, geo.regions) + a.shape[2:], lambda b, c: (0, b) + (0,) * (a.ndim - 2))
    const_spec = lambda a: pl.BlockSpec(a.shape, lambda b, c: (0,) * a.ndim)
    outs = pl.pallas_call(
        functools.partial(body, geo=geo),
        grid=(batch // geo.regions, steps),
        in_specs=[row_spec(a.shape[1]) for a in rows] + [table_spec(a) for a in tables]
        + [state_spec(a) for a in states] + [const_spec(a) for a in consts],
        out_specs=[row_spec(w) for w in out_widths] + [state_spec(a) for a in states],
        out_shape=[jax.ShapeDtypeStruct((batch, seq, w), F32) for w in out_widths]
        + [jax.ShapeDtypeStruct(a.shape, F32) for a in states],
        scratch_shapes=scratch_shapes,
        compiler_params=pltpu.CompilerParams(dimension_semantics=("parallel", "arbitrary"),
                                             vmem_limit_bytes=VMEM_LIMIT),
        name=name,
    )(*[a.reshape(batch, seq, a.shape[1]) for a in rows], *tables, *states, *consts)
    return [o.reshape(batch * seq, o.shape[2]) for o in outs[:len(out_widths)]] + list(outs[len(out_widths):])


def _run_step(geo, prepare, finish):
    groups = [list(range(g, g + geo.joint)) for g in range(0, geo.regions, geo.joint)]
    items = [(group, ci) for group in groups for ci in range(geo.chunks)]

    def advance(gen, stages):
        for _ in range(stages):
            try:
                next(gen)
            except StopIteration as stop:
                return True, stop.value
        return False, None

    _, prepared = advance(prepare(*items[0]), 10 ** 6)
    for n, item in enumerate(items):
        finishing = finish(*item, prepared)
        preparing = prepare(*items[n + 1]) if n + 1 < len(items) else None
        finished, prepared_next = False, preparing is None
        prepared = None
        while not (finished and prepared_next):
            if not finished:
                finished, _ = advance(finishing, 1)
            if not prepared_next:
                prepared_next, prepared = advance(preparing, PREPARE_STAGES_PER_FINISH_STAGE)


def _ret_stages(p_ref, cos_ref, sin_ref, s0_ref, dec_ref, qs_ref, ks_ref, ss_ref, nw_ref, s_ref, geo):
    @pl.when(pl.program_id(1) == 0)
    def _():
        s_ref[...] = s0_ref[...]

    lane = lax.broadcasted_iota(jnp.int32, (geo.chunk, RET_Q), 1)
    even = (lane & 1) == 0
    heads = range(RET_HEADS)
    k_slices = [slice(h * RET_DK, (h + 1) * RET_DK) for h in heads]
    v_slices = [slice(h * RET_DV, (h + 1) * RET_DV) for h in heads]

    def rotary(x, cos, sin):
        partner = jnp.where(even, pltpu.roll(x, RET_Q - 1, 1), pltpu.roll(x, 1, 1))
        return x * cos + partner * sin

    def prepare(group, ci):
        q_l, k_l, qs_l, ks_l, v_l, dec_l, ss_l, post = [], [], [], [], [], [], [], []
        for reg in group:
            rows = geo.rows(reg, ci)
            p = p_ref[rows]
            cos, sin = cos_ref[rows[1], :], sin_ref[rows[1], :]
            q = rotary(p[:, :RET_Q], cos, sin)
            k = rotary(p[:, RET_Q:2 * RET_Q], cos, sin) * (RET_DK ** -0.5)
            q_state = q * qs_ref[...]
            k_state = k * ks_ref[...]
            for h in heads:
                q_l.append(q[:, k_slices[h]])
                k_l.append(k[:, k_slices[h]])
                qs_l.append(q_state[:, k_slices[h]])
                ks_l.append(k_state[:, k_slices[h]])
                v_l.append(p[:, 2 * RET_Q + h * RET_DV:2 * RET_Q + (h + 1) * RET_DV])
                dec_l.append(dec_ref[h])
                ss_l.append(ss_ref[h])
                post.append((rows, v_slices[h], p[:, 2 * RET_Q + RET_V + h * RET_DV:
                                                  2 * RET_Q + RET_V + (h + 1) * RET_DV]))
        scores = _each(lambda q_, k_, d: _mm_nt(q_, k_) * d, q_l, k_l, dec_l)
        yield
        intra = _each(_mm, scores, v_l)
        yield
        outer = _each(_mm_tn, ks_l, v_l)
        yield
        return qs_l, ss_l, post, intra, outer

    states = {}

    def finish(group, ci, prepared):
        qs_l, ss_l, post, intra, outer = prepared
        key = group[0]
        if ci == 0:
            states[key] = [s_ref[reg, h] for reg in group for h in heads]
        from_state = _each(_mm, qs_l, states[key])
        yield
        states[key] = _each(lambda s, ss, o: s * ss + o, states[key], ss_l, outer)
        o_l = _each(jnp.add, from_state, intra)
        ms_l = [jnp.mean(o * o, -1, keepdims=True) for o in o_l]
        if ci == geo.chunks - 1:
            for i, reg in enumerate(group):
                for h in heads:
                    s_ref[reg, h] = states[key][i * RET_HEADS + h]
        return [o * lax.rsqrt(ms + NORM_EPS) * nw_ref[:, vs_] * _silu(gate)
                for (rows, vs_, gate), o, ms in zip(post, o_l, ms_l)]

    return prepare, finish


def _ret_tables(pos, geo):
    chunk = geo.chunk
    half = RET_DK // 2
    inv = 1.0 / (RET_ROPE_BASE ** jnp.linspace(0.0, 1.0, half, dtype=F32))
    ang = pos.astype(F32)[:, None] * inv[None, :]
    cos = jnp.repeat(jnp.cos(ang), 2, axis=-1)
    sin = jnp.stack([-jnp.sin(ang), jnp.sin(ang)], -1).reshape(ang.shape[0], RET_DK)
    cos, sin = jnp.tile(cos, (1, RET_HEADS)), jnp.tile(sin, (1, RET_HEADS))
    lg = jnp.log1p(-jnp.exp2(-jnp.linspace(5.0, 12.0, RET_HEADS, dtype=F32)))
    ci = jnp.arange(chunk, dtype=F32)
    diff = ci[:, None] - ci[None, :]
    causal = diff >= 0
    decay = jnp.where(causal, jnp.exp(jnp.where(causal, diff, 0.0) * lg[:, None, None]), 0.0)
    q_scale = jnp.exp((ci + 1.0) * lg[:, None])
    k_scale = jnp.exp((chunk - 1.0 - ci) * lg[:, None])
    s_scale = jnp.exp(chunk * lg)
    widen = lambda t: jnp.repeat(t.T, RET_DK, axis=1)
    s_scale = jnp.broadcast_to(s_scale[:, None, None], (RET_HEADS, 1, RET_DV))
    return cos, sin, decay, widen(q_scale), widen(k_scale), s_scale


def _rwkv_stages(p_ref, s0_ref, sh0_ref, mu_ref, w0_ref, w2_ref, a0_ref, a2_ref, kk_ref, ka_ref, rk_ref,
                 lnw_ref, lnb_ref, s_ref, shout_ref, geo):
    chunk = geo.chunk
    pair_w = 2 * RWKV_N
    n_pairs = RWKV_HEADS // 2

    @pl.when(pl.program_id(1) == 0)
    def _():
        s_ref[...] = s0_ref[...]
        shout_ref[...] = sh0_ref[...]

    ri = lax.broadcasted_iota(jnp.int32, (chunk, 2 * chunk), 0)
    ci_ = lax.broadcasted_iota(jnp.int32, (chunk, 2 * chunk), 1) & (chunk - 1)
    tril = ri >= ci_
    strict = ri > ci_
    sq_r, sq_c = _square_masks(chunk)
    tril_one = sq_r >= sq_c
    first_row = lax.broadcasted_iota(jnp.int32, (chunk, SHIFT_W), 0) == 0
    head0 = lax.broadcasted_iota(jnp.int32, (chunk, pair_w), 1) < RWKV_N
    pr, pc = _square_masks(pair_w)
    pair_eye = pr == pc
    same_head = (pr < RWKV_N) == (pc < RWKV_N)
    pairs = range(n_pairs)
    lanes = [slice(p * pair_w, (p + 1) * pair_w) for p in pairs]
    per_pair = lambda t: [t[:, s] for s in lanes]
    blocks = lambda x: _block_rows(x, RWKV_N)

    def head_sums(x):
        first = jnp.sum(jnp.where(head0, x, 0.0), -1, keepdims=True)
        second = jnp.sum(jnp.where(head0, 0.0, x), -1, keepdims=True)
        return jnp.where(head0, first, second)

    def prepare(group, ci):
        r_p, k_p, v_p, kk_p, b_p, cum_p, ld_p, bonus_p, post = [], [], [], [], [], [], [], [], []
        for reg in group:
            rows = geo.rows(reg, ci)
            p = p_ref[rows]
            sh = p[:, :SHIFT_W]
            before = shout_ref[reg] if ci == 0 else p_ref[reg, rows[1].start - 1:rows[1].start, :SHIFT_W]
            prev = jnp.where(first_row, before, pltpu.roll(sh, 1, 0))
            if ci == geo.chunks - 1:
                shout_ref[reg] = sh[chunk - 1:chunk, :]
            xs = sh + (prev - sh) * mu_ref[...]
            r = xs[:, :RWKV_W]
            k_in = xs[:, RWKV_W:2 * RWKV_W]
            v = xs[:, 2 * RWKV_W:3 * RWKV_W]
            wd = xs[:, 3 * RWKV_W:3 * RWKV_W + RWKV_LORA]
            ad = xs[:, 3 * RWKV_W + RWKV_LORA:]
            w = -_softplus(-(w0_ref[...] + _mm3(jnp.tanh(wd), w2_ref[...]))) - 0.5
            log_decay = -jnp.exp(w)
            a = jax.nn.sigmoid(a0_ref[...] + _mm3(ad, a2_ref[...]))
            k = k_in * (1.0 + (a - 1.0) * ka_ref[...])
            kk = [x * lax.rsqrt(head_sums(x * x) + L2_EPS) for x in per_pair(k_in * kk_ref[...])]
            r_p += per_pair(r)
            k_p += per_pair(k)
            v_p += per_pair(v)
            kk_p += kk
            b_p += _each(jnp.multiply, kk, per_pair(a))
            cum_p += per_pair(_chunk_cumsum(tril_one, log_decay))
            ld_p += per_pair(log_decay)
            bonus_p += [head_sums(x) for x in per_pair(r * k * rk_ref[...])]
            post += [(rows, lanes[q], p[:, SHIFT_W + q * pair_w:SHIFT_W + (q + 1) * pair_w]) for q in pairs]
        yield
        cum_last = [c[chunk - 1:chunk, :] for c in cum_p]
        grow = [jnp.exp(-c) for c in cum_p]
        tail = _each(lambda cl, c: jnp.exp(cl - c), cum_last, cum_p)
        lhs = _each(lambda kk, r_, c, ld: jnp.concatenate([kk * jnp.exp(c - ld), r_ * jnp.exp(c)], axis=0),
                    kk_p, r_p, cum_p, ld_p)
        g_b = _each(lambda x, b, g: _mm_nt(x, blocks(b * g)), lhs, b_p, grow)
        yield
        inverting = _unit_lower_inverses([jnp.where(strict, g[:chunk], 0.0) for g in g_b], chunk, width=2)
        g_k = _each(lambda x, k_, g: _mm_nt(x, blocks(k_ * g)), lhs, k_p, grow)
        yield
        v_blocks = [blocks(v_) for v_ in v_p]
        from_v = _each(lambda g, vb: _mm(jnp.where(strict, g[:chunk], 0.0), vb), g_k, v_blocks)
        yield
        y_v = _each(lambda g, vb: _mm(jnp.where(tril, g[chunk:], 0.0), vb), g_k, v_blocks)
        yield
        m_rb = [jnp.where(tril, g[chunk:], 0.0) for g in g_b]
        k_tail_t = _each(lambda k_, b, t: jnp.concatenate([k_ * t, b * t], axis=0).T, k_p, b_p, tail)
        state_decay = [jnp.sum(jnp.where(pair_eye, jnp.exp(cl), 0.0), axis=1, keepdims=True) for cl in cum_last]
        inverses = yield from inverting
        return lhs, inverses, from_v, y_v, m_rb, k_tail_t, v_p, state_decay, bonus_p, post

    states = {}

    def load_state(reg, q):
        zero = jnp.zeros((RWKV_N, RWKV_N), F32)
        return jnp.concatenate([jnp.concatenate([s_ref[reg, 2 * q].T, zero], axis=1),
                                jnp.concatenate([zero, s_ref[reg, 2 * q + 1].T], axis=1)], axis=0)

    def finish(group, ci, prepared):
        lhs, inverses, from_v, y_v, m_rb, k_tail_t, v_p, state_decay, bonus_p, post = prepared
        key = group[0]
        if ci == 0:
            states[key] = [load_state(reg, q) for reg in group for q in pairs]
        from_state = _each(_mm, lhs, states[key])
        yield
        u = _each(lambda inv, fs, fv: _mm(inv, blocks(fs[:chunk] + fv)), inverses, from_state, from_v)
        yield
        outer = _each(lambda kt, v_, u_: _mm(kt, jnp.concatenate([v_, -u_], axis=0)), k_tail_t, v_p, u)
        yield
        y_u = _each(lambda m, u_: _mm(m, blocks(u_)), m_rb, u)
        yield
        states[key] = _each(lambda s, d, o: s * d + jnp.where(same_head, o, 0.0), states[key], state_decay, outer)
        y_l = _each(lambda fs, yv, yu: fs[chunk:] + yv - yu, from_state, y_v, y_u)
        mean_l = [head_sums(y) * (1.0 / RWKV_N) for y in y_l]
        cen_l = _each(jnp.subtract, y_l, mean_l)
        var_l = [head_sums(jnp.square(c)) * (1.0 / RWKV_N) for c in cen_l]
        gated = []
        for (rows, ls, gate), cen, var, bonus, v_ in zip(post, cen_l, var_l, bonus_p, v_p):
            y = cen * lax.rsqrt(var + RWKV_GN_EPS) * lnw_ref[:, ls] + lnb_ref[:, ls]
            gated.append((y + bonus * v_) * _silu(gate))
        if ci == geo.chunks - 1:
            for i, reg in enumerate(group):
                for q in pairs:
                    state = states[key][i * n_pairs + q]
                    s_ref[reg, 2 * q] = state[:RWKV_N, :RWKV_N].T
                    s_ref[reg, 2 * q + 1] = state[RWKV_N:, RWKV_N:].T
        return gated

    return prepare, finish


def _alternate(*generators):
    values = [None] * len(generators)
    live = list(range(len(generators)))
    while live:
        for n in list(live):
            try:
                next(generators[n])
            except StopIteration as stop:
                values[n] = stop.value
                live.remove(n)
        yield
    return values


N_RET_CONSTS = 5
N_RWKV_CONSTS = 10


def _even_mixers_kernel(p_ret_ref, p_rwkv_ref, h_ref, cos_ref, sin_ref, s0_ret_ref, s0_rwkv_ref, sh0_ref, *rest, geo):
    ret_consts, rest = rest[:N_RET_CONSTS], rest[N_RET_CONSTS:]
    rwkv_consts, rest = rest[:N_RWKV_CONSTS], rest[N_RWKV_CONSTS:]
    wout_ref, h1_ref, s_ret_ref, s_rwkv_ref, shout_ref = rest
    ret_prepare, ret_finish = _ret_stages(p_ret_ref, cos_ref, sin_ref, s0_ret_ref, *ret_consts, s_ret_ref, geo)
    rwkv_prepare, rwkv_finish = _rwkv_stages(p_rwkv_ref, s0_rwkv_ref, sh0_ref, *rwkv_consts, s_rwkv_ref, shout_ref,
                                             geo)
    n_pairs = RWKV_HEADS // 2

    def prepare(group, ci):
        return _alternate(rwkv_prepare(group, ci), ret_prepare(group, ci))

    def finish(group, ci, prepared):
        rwkv_out, ret_out = yield from _alternate(rwkv_finish(group, ci, prepared[0]),
                                                  ret_finish(group, ci, prepared[1]))
        mixed = jnp.concatenate([jnp.concatenate(ret_out[i * RET_HEADS:(i + 1) * RET_HEADS]
                                                 + rwkv_out[i * n_pairs:(i + 1) * n_pairs], axis=1)
                                 for i in range(len(group))], axis=0)
        out = jnp.dot(mixed.astype(BF16), wout_ref[...], preferred_element_type=F32)
        for i, reg in enumerate(group):
            rows = geo.rows(reg, ci)
            h1_ref[rows] = h_ref[rows] + out[i * geo.chunk:(i + 1) * geo.chunk]

    _run_step(geo, prepare, finish)


def _even_mixers(p_ret, p_rwkv, h0, pos, s_ret, s_rwkv, shift0, ret_norm_w, rwkv_params, w_out, batch, seq, geo):
    cos, sin, decay, q_scale, k_scale, s_scale = _ret_tables(pos, geo)
    ret_consts = [decay, q_scale, k_scale, s_scale, ret_norm_w]
    assert len(ret_consts) == N_RET_CONSTS and len(rwkv_params) == N_RWKV_CONSTS
    h1, ret_new, rwkv_new, shift = _recurrent_call(
        _even_mixers_kernel, geo, batch, seq, [p_ret, p_rwkv, h0], [cos, sin],
        [s_ret, s_rwkv, shift0.reshape(1, batch, 1, SHIFT_W)], ret_consts + list(rwkv_params) + [w_out],
        [D_MODEL], [], "even_mixers")
    return h1, ret_new, rwkv_new, shift.reshape(1, batch, SHIFT_W)


def _gdn_kernel(qkv_ref, z_ref, ba_ref, s0_ref, c0_ref, cw_ref, alog_ref, dtb_ref, gnw_ref,
                o_ref, s_ref, cout_ref, xbuf_ref, *, geo):
    chunk = geo.chunk
    taps = GDN_CONV - 1
    region_rows = geo.region_rows
    stride = region_rows + SUBLANES

    @pl.when(pl.program_id(1) == 0)
    def _():
        s_ref[...] = s0_ref[...]
        cout_ref[...] = c0_ref[...]

    ri = lax.broadcasted_iota(jnp.int32, (chunk, 2 * chunk), 0)
    lane = lax.broadcasted_iota(jnp.int32, (chunk, 2 * chunk), 1)
    ci_ = lane & (chunk - 1)
    first_half = lane < chunk
    tril = ri >= ci_
    strict = ri > ci_
    eye = ri == ci_
    sq_r, sq_c = _square_masks(chunk)
    tril_one = sq_r >= sq_c
    heads = range(GDN_HEADS)
    head_slice = lambda base, h: slice(base + h * GDN_DK, base + (h + 1) * GDN_DK)
    l2 = lambda x: x * lax.rsqrt(jnp.sum(x * x, -1, keepdims=True) + L2_EPS)

    for reg in range(geo.regions):
        base = reg * stride + SUBLANES
        xbuf_ref[base - taps:base, :] = cout_ref[reg]
        xbuf_ref[base:base + region_rows, :] = qkv_ref[reg]
        cout_ref[reg] = qkv_ref[reg, region_rows - taps:region_rows, :]

    def prepare(group, ci):
        q_h, k_h, v_h, beta_h, gc, post = [], [], [], [], [], []
        for reg in group:
            rows = geo.rows(reg, ci)
            at = reg * stride + SUBLANES + ci * chunk
            conv = xbuf_ref[at:at + chunk, :] * cw_ref[taps:taps + 1, :]
            for j in range(taps):
                conv = conv + xbuf_ref[at - taps + j:at - taps + j + chunk, :] * cw_ref[j:j + 1, :]
            act = _silu(conv)
            ba = ba_ref[rows]
            beta = jax.nn.sigmoid(ba[:, :GDN_HEADS])
            g = -jnp.exp(alog_ref[...]) * _softplus(ba[:, GDN_HEADS:] + dtb_ref[...])
            gcum = _chunk_cumsum(tril_one, g)
            q_h += [l2(act[:, head_slice(0, h)]) * (GDN_DK ** -0.5) for h in heads]
            k_h += [l2(act[:, head_slice(GDN_QK, h)]) for h in heads]
            v_h += [act[:, head_slice(2 * GDN_QK, h)] for h in heads]
            beta_h += [beta[:, h:h + 1] for h in heads]
            gc += [gcum[:, h:h + 1] for h in heads]
            post += [(rows, head_slice(0, h)) for h in heads]
        yield
        pair = lambda t: [jnp.concatenate(t[i:i + 2], axis=1) for i in range(0, len(t), 2)]
        kb = _each(jnp.multiply, k_h, beta_h)
        exp_gc = [jnp.exp(g_) for g_ in gc]
        g_last = [g_[chunk - 1:chunk, :] for g_ in gc]
        gc_col = [jnp.where(first_half, gc[i], gc[i + 1]) for i in range(0, len(gc), 2)]
        gc_row = [jnp.sum(jnp.where(eye, g_, 0.0), axis=0, keepdims=True) for g_ in gc_col]
        decay = _each(lambda c, r: jnp.where(tril, jnp.exp(jnp.where(tril, c - r, 0.0)), 0.0), gc_col, gc_row)
        k_blocks = [_block_rows(k_, GDN_DK) for k_ in pair(k_h)]
        lower = _each(lambda kb_, kbl, d: jnp.where(strict, _mm_nt(kb_, kbl) * d, 0.0), pair(kb), k_blocks, decay)
        yield
        attn = _each(lambda q_, kbl, d: _mm_nt(q_, kbl) * d, pair(q_h), k_blocks, decay)
        yield
        q_state = _each(jnp.multiply, q_h, exp_gc)
        k_tail_t = _each(lambda k_, gl, g_: (k_ * jnp.exp(gl - g_)).T, k_h, g_last, gc)
        inverses = yield from _unit_lower_inverses(lower, chunk, width=2)
        rhs = _each(lambda v_, b, kb_, e: jnp.concatenate([v_ * b, kb_ * e], axis=1), v_h, beta_h, kb, exp_gc)
        sol = _each(lambda inv, r_: _mm(inv, _block_rows(r_, GDN_DV + GDN_DK)), inverses, pair(rhs))
        yield
        return q_state, sol, attn, k_tail_t, [jnp.exp(gl) for gl in g_last], post

    states = {}

    def finish(group, ci, prepared):
        q_state, sol, attn, k_tail_t, state_decay, post = prepared
        key = group[0]
        if ci == 0:
            states[key] = [s_ref[reg, h] for reg in group for h in heads]
        width = GDN_DV + GDN_DK
        u_w = [s_[:, i * width:(i + 1) * width] for s_ in sol for i in range(2)]
        v_new = _each(lambda s_, state: s_[:, :GDN_DV] - _mm(s_[:, GDN_DV:], state), u_w, states[key])
        yield
        o_state = _each(_mm, q_state, states[key])
        yield
        outer = _each(_mm, k_tail_t, v_new)
        yield
        v_pairs = [jnp.concatenate(v_new[i:i + 2], axis=1) for i in range(0, len(v_new), 2)]
        o_pairs = _each(lambda a_, v_: _mm(a_, _block_rows(v_, GDN_DV)), attn, v_pairs)
        o_intra = [o_[:, i * GDN_DV:(i + 1) * GDN_DV] for o_ in o_pairs for i in range(2)]
        yield
        states[key] = _each(lambda s, d, o: s * d + o, states[key], state_decay, outer)
        o_l = _each(jnp.add, o_state, o_intra)
        ms_l = [jnp.mean(o * o, -1, keepdims=True) for o in o_l]
        for (rows, os_), o, ms in zip(post, o_l, ms_l):
            o_ref[rows[0], rows[1], os_] = (o * lax.rsqrt(ms + NORM_EPS) * gnw_ref[...]
                                            * _silu(z_ref[rows[0], rows[1], os_]))
        if ci == geo.chunks - 1:
            for i, reg in enumerate(group):
                for h in heads:
                    s_ref[reg, h] = states[key][i * GDN_HEADS + h]

    _run_step(geo, prepare, finish)


def _gdn(qkv, z, ba, s0, conv0, params, batch, seq, geo):
    xbuf = pltpu.VMEM((geo.regions * (geo.region_rows + SUBLANES), GDN_QKV), F32)
    return _recurrent_call(_gdn_kernel, geo, batch, seq, [qkv, z, ba], [], [s0, conv0], list(params),
                           [GDN_VW], [xbuf], "gated_delta")


def _run_group(x, pos, s_ret, s_rwkv, s_shift, s_gdn, s_conv, geo, w):
    batch, seq, _ = x.shape
    n_tokens = batch * seq
    h0 = x.reshape(n_tokens, D_MODEL)
    p_ret, p_rwkv = _token_call(
        _even_in_kernel, n_tokens, {"x": h0}, {"nw": w["norm_e"], "w": w["w_in_e"]},
        [RET_W, RWKV_IN], ["x", "nw", "w"], "even_in", EVEN_IN_TILE)
    h1, ret_new, rwkv_new, shift_new = _even_mixers(
        p_ret, p_rwkv, h0, pos, s_ret, s_rwkv, s_shift, w["ret_norm_w"], w["rwkv_params"], w["w_out_e"],
        batch, seq, geo)
    qkv, z, ba = _token_call(
        _odd_in_kernel, n_tokens, {"h": h1}, {"nw": w["norm_o"], "win": w["w_in_o"]},
        [GDN_QKV, GDN_VW, 2 * GDN_HEADS], ["h", "nw", "win"], "odd_in", ODD_IN_TILE)
    o_gdn, gdn_new, conv_new = _gdn(qkv, z, ba, s_gdn, s_conv, w["gdn_params"], batch, seq, geo)
    (y,) = _token_call(
        _odd_out_final_kernel, n_tokens, {"h": h1, "o": o_gdn}, {"wout": w["w_out_o"], "nw": w["final_norm"]},
        [D_MODEL], ["h", "o", "wout", "nw"], "odd_out_final", ODD_OUT_TILE)
    return y.reshape(batch, seq, D_MODEL), ret_new, rwkv_new, shift_new, gdn_new, conv_new


def kernel(x_prompt, x_sample, state_ret, state_rwkv, state_shift, state_gdn, state_conv, norm_e, w_in_e, rwkv_mu, rwkv_w0, rwkv_w2, rwkv_a0, rwkv_a2, rwkv_kk, rwkv_ka, rwkv_rk, rwkv_ln_w, rwkv_ln_b, ret_norm_w, w_out_e, norm_o, w_in_o, gdn_conv_w, gdn_a_log, gdn_dt_bias, gdn_norm_w, w_out_o, final_norm):
    assert state_ret.shape[0] == 1 and state_gdn.shape[0] == 1, "one even and one odd layer"
    row = lambda a: a.reshape(1, -1)
    w = {
        "norm_e": row(norm_e[0]), "w_in_e": w_in_e[0].astype(BF16), "ret_norm_w": row(ret_norm_w[0]),
        "rwkv_params": (row(rwkv_mu[0]), row(rwkv_w0[0]), rwkv_w2[0], row(rwkv_a0[0]), rwkv_a2[0], row(rwkv_kk[0]),
                        row(rwkv_ka[0]), row(rwkv_rk[0]), row(rwkv_ln_w[0]), row(rwkv_ln_b[0])),
        "w_out_e": w_out_e[0].astype(BF16), "norm_o": row(norm_o[0]),
        "w_in_o": w_in_o[0].astype(BF16),
        "gdn_params": (gdn_conv_w[0], row(gdn_a_log[0]), row(gdn_dt_bias[0]), row(gdn_norm_w[0])),
        "w_out_o": w_out_o[0].astype(BF16), "final_norm": row(final_norm),
    }
    batch, seq, _ = x_prompt.shape
    dec_batch, dec_seq, _ = x_sample.shape
    zeros = lambda s: jnp.zeros((1, batch) + s.shape[2:], F32)
    prompt = _run_group(x_prompt, jnp.arange(seq), zeros(state_ret), zeros(state_rwkv), zeros(state_shift),
                        zeros(state_gdn), zeros(state_conv),
                        _Geometry(math.gcd(seq, PROMPT_CHUNK), PROMPT_SEQS_PER_STEP, PROMPT_CHUNKS_PER_STEP,
                                  PROMPT_SEQS_PER_STEP), w)
    sample = _run_group(x_sample, PAST_LEN + jnp.arange(dec_seq), state_ret, state_rwkv, state_shift,
                        state_gdn, state_conv, _Geometry(dec_seq, SAMPLE_SEQS_PER_STEP, 1, SAMPLE_SEQS_JOINT), w)
    return (prompt[0], sample[0]) + prompt[1:] + sample[1:]
```

```python
import functools
import math
from typing import NamedTuple

import jax
import jax.numpy as jnp
from jax import lax
from jax.experimental import pallas as pl
from jax.experimental.pallas import tpu as pltpu

F32 = jnp.float32
BF16 = jnp.bfloat16

D_MODEL = 1024
PAST_LEN = 16384
RET_HEADS, RET_DK, RET_DV = 4, 64, 128
RET_ROPE_BASE = 10000.0
RET_Q = RET_HEADS * RET_DK
RET_V = RET_HEADS * RET_DV
RET_W = 2 * RET_Q + 2 * RET_V
RWKV_HEADS, RWKV_N = 8, 64
RWKV_W = RWKV_HEADS * RWKV_N
RWKV_LORA = 64
RWKV_GN_EPS = 64e-5
SHIFT_W = 3 * RWKV_W + 2 * RWKV_LORA
RWKV_IN = SHIFT_W + RWKV_W
GDN_HEADS, GDN_DK, GDN_DV, GDN_CONV = 8, 128, 128, 4
GDN_QK = GDN_HEADS * GDN_DK
GDN_VW = GDN_HEADS * GDN_DV
GDN_QKV = 2 * GDN_QK + GDN_VW
NORM_EPS = 1e-6
L2_EPS = 1e-12

SUBLANES = 8
INV_BLOCK = 8
VMEM_LIMIT = 56 * 1024 * 1024
PROMPT_CHUNK = 64
PROMPT_CHUNKS_PER_STEP = 2
PROMPT_SEQS_PER_STEP = 4
SAMPLE_SEQS_PER_STEP = 16
SAMPLE_SEQS_JOINT = 8
PREPARE_STAGES_PER_FINISH_STAGE = 3
EVEN_IN_TILE = 1024
ODD_IN_TILE = 1024
ODD_OUT_TILE = 1024


def _split_bf16(x):
    hi = x.astype(BF16)
    return hi, (x - hi.astype(F32)).astype(BF16)


def _dot(a, b, dims, mode):
    dot = lambda x, y: lax.dot_general(x, y, (dims, ((), ())), preferred_element_type=F32)
    if mode == "bf16":
        return dot(a.astype(BF16), b.astype(BF16))
    assert mode == "bf16x3", mode
    a_hi, a_lo = _split_bf16(a)
    b_hi, b_lo = _split_bf16(b)
    return (dot(a_lo, b_hi) + dot(a_hi, b_lo)) + dot(a_hi, b_hi)


def _mm(a, b, mode="bf16"):
    return _dot(a, b, ((1,), (0,)), mode)


def _mm_nt(a, b, mode="bf16"):
    return _dot(a, b, ((1,), (1,)), mode)


def _mm_tn(a, b, mode="bf16"):
    return _dot(a, b, ((0,), (0,)), mode)


def _mm3(a, b):
    return _mm(a, b, "bf16x3")


def _chunk_cumsum(tril, x):
    ones = tril.astype(BF16)
    hi = x.astype(BF16)
    rest = x - hi.astype(F32)
    mid = rest.astype(BF16)
    lo = (rest - mid.astype(F32)).astype(BF16)
    dot = lambda p: jnp.dot(ones, p, preferred_element_type=F32)
    return (dot(lo) + dot(mid)) + dot(hi)


def _silu(x):
    return x * jax.nn.sigmoid(x)


def _softplus(x):
    return jnp.maximum(x, 0.0) + jnp.log1p(jnp.exp(-jnp.abs(x)))


def _square_masks(n):
    ri = lax.broadcasted_iota(jnp.int32, (n, n), 0)
    ci = lax.broadcasted_iota(jnp.int32, (n, n), 1)
    return ri, ci


def _each(fn, *seqs):
    return [fn(*args) for args in zip(*seqs)]


def _block_rows(x, split):
    low_half = lax.broadcasted_iota(jnp.int32, x.shape, 1) < split
    return jnp.concatenate([jnp.where(low_half, x, 0.0), jnp.where(low_half, 0.0, x)], axis=0)


def _neumann_inverses(lows, eye, n, expand, mode):
    invs = [eye - low for low in lows]
    powers = list(lows)
    k = 2
    while k < n:
        powers = _each(lambda p: _mm(p, expand(p), mode), powers)
        yield
        invs = _each(lambda inv, p: inv + _mm(inv, expand(p), mode), invs, powers)
        yield
        k *= 2
    return invs


def _unit_lower_inverses(lows, n, width=1):
    assert width in (1, 2)
    expand = (lambda x: x) if width == 1 else (lambda x: _block_rows(x, n))
    ri = lax.broadcasted_iota(jnp.int32, (n, width * n), 0)
    ci = lax.broadcasted_iota(jnp.int32, (n, width * n), 1) & (n - 1)
    eye = (ri == ci).astype(F32)
    if n <= INV_BLOCK:
        invs = yield from _neumann_inverses(lows, eye, n, expand, "bf16x3")
    else:
        shift = int(math.log2(INV_BLOCK))
        same_block = (ri >> shift) == (ci >> shift)
        diag_invs = yield from _neumann_inverses([jnp.where(same_block, low, 0.0) for low in lows], eye, INV_BLOCK,
                                                 expand, "bf16x3")
        offs = _each(lambda d, low: _mm(d, expand(jnp.where(same_block, 0.0, low))), diag_invs, lows)
        yield
        off_invs = yield from _neumann_inverses(offs, eye, n // INV_BLOCK, expand, "bf16")
        invs = _each(lambda o, d: _mm(o, expand(d)), off_invs, diag_invs)
        yield
    residuals = _each(lambda low, inv: (eye - inv) - _mm3(low, expand(inv)), lows, invs)
    yield
    refined = _each(lambda inv, res: inv + _mm(inv, expand(res)), invs, residuals)
    yield
    return refined


def _rmsnorm_rows(x, w):
    return x * lax.rsqrt(jnp.mean(x * x, -1, keepdims=True) + NORM_EPS) * w


def _even_in_kernel(x_ref, nw_ref, w_ref, ret_ref, rwkv_ref):
    xn = _rmsnorm_rows(x_ref[...], nw_ref[...]).astype(BF16)
    ret_ref[...] = jnp.dot(xn, w_ref[:, :RET_W], preferred_element_type=F32)
    rwkv_ref[...] = jnp.dot(xn, w_ref[:, RET_W:], preferred_element_type=F32)


def _odd_in_kernel(h_ref, nw_ref, win_ref, qkv_ref, z_ref, ba_ref):
    xn = _rmsnorm_rows(h_ref[...], nw_ref[...]).astype(BF16)
    qkv_ref[...] = jnp.dot(xn, win_ref[:, :GDN_QKV], preferred_element_type=F32)
    z_ref[...] = jnp.dot(xn, win_ref[:, GDN_QKV:GDN_QKV + GDN_VW], preferred_element_type=F32)
    ba_ref[...] = jnp.dot(xn, win_ref[:, GDN_QKV + GDN_VW:], preferred_element_type=F32)


def _odd_out_final_kernel(h_ref, o_ref, wout_ref, nw_ref, y_ref):
    h2 = h_ref[...] + jnp.dot(o_ref[...].astype(BF16), wout_ref[...], preferred_element_type=F32)
    y_ref[...] = _rmsnorm_rows(h2, nw_ref[...])


def _row_spec(tile, width):
    return pl.BlockSpec((tile, width), lambda i: (i, 0))


def _full_spec(shape):
    return pl.BlockSpec(shape, lambda i: (0,) * len(shape), pipeline_mode=pl.Buffered(1))


def _token_call(kernel, n_tokens, row_inputs, full_inputs, out_widths, order, name, tile):
    tile = min(tile, n_tokens)
    assert n_tokens % tile == 0
    specs = {**{k: _row_spec(tile, v.shape[1]) for k, v in row_inputs.items()},
             **{k: _full_spec(v.shape) for k, v in full_inputs.items()}}
    arrays = {**row_inputs, **full_inputs}
    return pl.pallas_call(
        kernel,
        grid=(n_tokens // tile,),
        in_specs=[specs[k] for k in order],
        out_specs=[_row_spec(tile, w) for w in out_widths],
        out_shape=[jax.ShapeDtypeStruct((n_tokens, w), F32) for w in out_widths],
        compiler_params=pltpu.CompilerParams(dimension_semantics=("parallel",), vmem_limit_bytes=VMEM_LIMIT),
        name=name,
    )(*[arrays[k] for k in order])


class _Geometry(NamedTuple):
    chunk: int
    regions: int
    chunks: int
    joint: int

    @property
    def region_rows(self):
        return self.chunks * self.chunk

    def rows(self, region, chunk_index, lanes=slice(None)):
        return region, slice(chunk_index * self.chunk, (chunk_index + 1) * self.chunk), lanes


def _recurrent_call(body, geo, batch, seq, rows, tables, states, consts, out_widths, scratch_shapes, name):
    steps = seq // geo.region_rows
    assert batch % geo.regions == 0 and steps * geo.region_rows == seq
    row_spec = lambda w: pl.BlockSpec((geo.regions, geo.region_rows, w), lambda b, c: (b, c, 0))
    table_spec = lambda a: pl.BlockSpec((geo.region_rows, a.shape[1]), lambda b, c: (c, 0))
    state_spec = lambda a: pl.BlockSpec((None, geo.regions) + a.shape[2:], lambda b, c: (0, b) + (0,) * (a.ndim - 2))
    const_spec = lambda a: pl.BlockSpec(a.shape, lambda b, c: (0,) * a.ndim)
    outs = pl.pallas_call(
        functools.partial(body, geo=geo),
        grid=(batch // geo.regions, steps),
        in_specs=[row_spec(a.shape[1]) for a in rows] + [table_spec(a) for a in tables]
        + [state_spec(a) for a in states] + [const_spec(a) for a in consts],
        out_specs=[row_spec(w) for w in out_widths] + [state_spec(a) for a in states],
        out_shape=[jax.ShapeDtypeStruct((batch, seq, w), F32) for w in out_widths]
        + [jax.ShapeDtypeStruct(a.shape, F32) for a in states],
        scratch_shapes=scratch_shapes,
        compiler_params=pltpu.CompilerParams(dimension_semantics=("parallel", "arbitrary"),
                                             vmem_limit_bytes=VMEM_LIMIT),
        name=name,
    )(*[a.reshape(batch, seq, a.shape[1]) for a in rows], *tables, *states, *consts)
    return [o.reshape(batch * seq, o.shape[2]) for o in outs[:len(out_widths)]] + list(outs[len(out_widths):])


def _run_step(geo, prepare, finish):
    groups = [list(range(g, g + geo.joint)) for g in range(0, geo.regions, geo.joint)]
    items = [(group, ci) for group in groups for ci in range(geo.chunks)]

    def advance(gen, stages):
        for _ in range(stages):
            try:
                next(gen)
            except StopIteration as stop:
                return True, stop.value
        return False, None

    _, prepared = advance(prepare(*items[0]), 10 ** 6)
    for n, item in enumerate(items):
        finishing = finish(*item, prepared)
        preparing = prepare(*items[n + 1]) if n + 1 < len(items) else None
        finished, prepared_next = False, preparing is None
        prepared = None
        while not (finished and prepared_next):
            if not finished:
                finished, _ = advance(finishing, 1)
            if not prepared_next:
                prepared_next, prepared = advance(preparing, PREPARE_STAGES_PER_FINISH_STAGE)


def _ret_stages(p_ref, cos_ref, sin_ref, s0_ref, dec_ref, qs_ref, ks_ref, ss_ref, nw_ref, s_ref, geo):
    @pl.when(pl.program_id(1) == 0)
    def _():
        s_ref[...] = s0_ref[...]

    lane = lax.broadcasted_iota(jnp.int32, (geo.chunk, RET_Q), 1)
    even = (lane & 1) == 0
    heads = range(RET_HEADS)
    k_slices = [slice(h * RET_DK, (h + 1) * RET_DK) for h in heads]
    v_slices = [slice(h * RET_DV, (h + 1) * RET_DV) for h in heads]

    def rotary(x, cos, sin):
        partner = jnp.where(even, pltpu.roll(x, RET_Q - 1, 1), pltpu.roll(x, 1, 1))
        return x * cos + partner * sin

    def prepare(group, ci):
        q_l, k_l, qs_l, ks_l, v_l, dec_l, ss_l, post = [], [], [], [], [], [], [], []
        for reg in group:
            rows = geo.rows(reg, ci)
            p = p_ref[rows]
            cos, sin = cos_ref[rows[1], :], sin_ref[rows[1], :]
            q = rotary(p[:, :RET_Q], cos, sin)
            k = rotary(p[:, RET_Q:2 * RET_Q], cos, sin) * (RET_DK ** -0.5)
            q_state = q * qs_ref[...]
            k_state = k * ks_ref[...]
            for h in heads:
                q_l.append(q[:, k_slices[h]])
                k_l.append(k[:, k_slices[h]])
                qs_l.append(q_state[:, k_slices[h]])
                ks_l.append(k_state[:, k_slices[h]])
                v_l.append(p[:, 2 * RET_Q + h * RET_DV:2 * RET_Q + (h + 1) * RET_DV])
                dec_l.append(dec_ref[h])
                ss_l.append(ss_ref[h])
                post.append((rows, v_slices[h], p[:, 2 * RET_Q + RET_V + h * RET_DV:
                                                  2 * RET_Q + RET_V + (h + 1) * RET_DV]))
        scores = _each(lambda q_, k_, d: _mm_nt(q_, k_) * d, q_l, k_l, dec_l)
        yield
        intra = _each(_mm, scores, v_l)
        yield
        outer = _each(_mm_tn, ks_l, v_l)
        yield
        return qs_l, ss_l, post, intra, outer

    states = {}

    def finish(group, ci, prepared):
        qs_l, ss_l, post, intra, outer = prepared
        key = group[0]
        if ci == 0:
            states[key] = [s_ref[reg, h] for reg in group for h in heads]
        from_state = _each(_mm, qs_l, states[key])
        yield
        states[key] = _each(lambda s, ss, o: s * ss + o, states[key], ss_l, outer)
        o_l = _each(jnp.add, from_state, intra)
        ms_l = [jnp.mean(o * o, -1, keepdims=True) for o in o_l]
        if ci == geo.chunks - 1:
            for i, reg in enumerate(group):
                for h in heads:
                    s_ref[reg, h] = states[key][i * RET_HEADS + h]
        return [o * lax.rsqrt(ms + NORM_EPS) * nw_ref[:, vs_] * _silu(gate)
                for (rows, vs_, gate), o, ms in zip(post, o_l, ms_l)]

    return prepare, finish


def _ret_tables(pos, geo):
    chunk = geo.chunk
    half = RET_DK // 2
    inv = 1.0 / (RET_ROPE_BASE ** jnp.linspace(0.0, 1.0, half, dtype=F32))
    ang = pos.astype(F32)[:, None] * inv[None, :]
    cos = jnp.repeat(jnp.cos(ang), 2, axis=-1)
    sin = jnp.stack([-jnp.sin(ang), jnp.sin(ang)], -1).reshape(ang.shape[0], RET_DK)
    cos, sin = jnp.tile(cos, (1, RET_HEADS)), jnp.tile(sin, (1, RET_HEADS))
    lg = jnp.log1p(-jnp.exp2(-jnp.linspace(5.0, 12.0, RET_HEADS, dtype=F32)))
    ci = jnp.arange(chunk, dtype=F32)
    diff = ci[:, None] - ci[None, :]
    causal = diff >= 0
    decay = jnp.where(causal, jnp.exp(jnp.where(causal, diff, 0.0) * lg[:, None, None]), 0.0)
    q_scale = jnp.exp((ci + 1.0) * lg[:, None])
    k_scale = jnp.exp((chunk - 1.0 - ci) * lg[:, None])
    s_scale = jnp.exp(chunk * lg)
    widen = lambda t: jnp.repeat(t.T, RET_DK, axis=1)
    s_scale = jnp.broadcast_to(s_scale[:, None, None], (RET_HEADS, 1, RET_DV))
    return cos, sin, decay, widen(q_scale), widen(k_scale), s_scale


def _rwkv_stages(p_ref, s0_ref, sh0_ref, mu_ref, w0_ref, w2_ref, a0_ref, a2_ref, kk_ref, ka_ref, rk_ref,
                 lnw_ref, lnb_ref, s_ref, shout_ref, geo):
    chunk = geo.chunk
    pair_w = 2 * RWKV_N
    n_pairs = RWKV_HEADS // 2

    @pl.when(pl.program_id(1) == 0)
    def _():
        s_ref[...] = s0_ref[...]
        shout_ref[...] = sh0_ref[...]

    ri = lax.broadcasted_iota(jnp.int32, (chunk, 2 * chunk), 0)
    ci_ = lax.broadcasted_iota(jnp.int32, (chunk, 2 * chunk), 1) & (chunk - 1)
    tril = ri >= ci_
    strict = ri > ci_
    sq_r, sq_c = _square_masks(chunk)
    tril_one = sq_r >= sq_c
    first_row = lax.broadcasted_iota(jnp.int32, (chunk, SHIFT_W), 0) == 0
    head0 = lax.broadcasted_iota(jnp.int32, (chunk, pair_w), 1) < RWKV_N
    pr, pc = _square_masks(pair_w)
    pair_eye = pr == pc
    same_head = (pr < RWKV_N) == (pc < RWKV_N)
    pairs = range(n_pairs)
    lanes = [slice(p * pair_w, (p + 1) * pair_w) for p in pairs]
    per_pair = lambda t: [t[:, s] for s in lanes]
    blocks = lambda x: _block_rows(x, RWKV_N)

    def head_sums(x):
        first = jnp.sum(jnp.where(head0, x, 0.0), -1, keepdims=True)
        second = jnp.sum(jnp.where(head0, 0.0, x), -1, keepdims=True)
        return jnp.where(head0, first, second)

    def prepare(group, ci):
        r_p, k_p, v_p, kk_p, b_p, cum_p, ld_p, bonus_p, post = [], [], [], [], [], [], [], [], []
        for reg in group:
            rows = geo.rows(reg, ci)
            p = p_ref[rows]
            sh = p[:, :SHIFT_W]
            before = shout_ref[reg] if ci == 0 else p_ref[reg, rows[1].start - 1:rows[1].start, :SHIFT_W]
            prev = jnp.where(first_row, before, pltpu.roll(sh, 1, 0))
            if ci == geo.chunks - 1:
                shout_ref[reg] = sh[chunk - 1:chunk, :]
            xs = sh + (prev - sh) * mu_ref[...]
            r = xs[:, :RWKV_W]
            k_in = xs[:, RWKV_W:2 * RWKV_W]
            v = xs[:, 2 * RWKV_W:3 * RWKV_W]
            wd = xs[:, 3 * RWKV_W:3 * RWKV_W + RWKV_LORA]
            ad = xs[:, 3 * RWKV_W + RWKV_LORA:]
            w = -_softplus(-(w0_ref[...] + _mm3(jnp.tanh(wd), w2_ref[...]))) - 0.5
            log_decay = -jnp.exp(w)
            a = jax.nn.sigmoid(a0_ref[...] + _mm3(ad, a2_ref[...]))
            k = k_in * (1.0 + (a - 1.0) * ka_ref[...])
            kk = [x * lax.rsqrt(head_sums(x * x) + L2_EPS) for x in per_pair(k_in * kk_ref[...])]
            r_p += per_pair(r)
            k_p += per_pair(k)
            v_p += per_pair(v)
            kk_p += kk
            b_p += _each(jnp.multiply, kk, per_pair(a))
            cum_p += per_pair(_chunk_cumsum(tril_one, log_decay))
            ld_p += per_pair(log_decay)
            bonus_p += [head_sums(x) for x in per_pair(r * k * rk_ref[...])]
            post += [(rows, lanes[q], p[:, SHIFT_W + q * pair_w:SHIFT_W + (q + 1) * pair_w]) for q in pairs]
        yield
        cum_last = [c[chunk - 1:chunk, :] for c in cum_p]
        grow = [jnp.exp(-c) for c in cum_p]
        tail = _each(lambda cl, c: jnp.exp(cl - c), cum_last, cum_p)
        lhs = _each(lambda kk, r_, c, ld: jnp.concatenate([kk * jnp.exp(c - ld), r_ * jnp.exp(c)], axis=0),
                    kk_p, r_p, cum_p, ld_p)
        g_b = _each(lambda x, b, g: _mm_nt(x, blocks(b * g)), lhs, b_p, grow)
        yield
        inverting = _unit_lower_inverses([jnp.where(strict, g[:chunk], 0.0) for g in g_b], chunk, width=2)
        g_k = _each(lambda x, k_, g: _mm_nt(x, blocks(k_ * g)), lhs, k_p, grow)
        yield
        v_blocks = [blocks(v_) for v_ in v_p]
        from_v = _each(lambda g, vb: _mm(jnp.where(strict, g[:chunk], 0.0), vb), g_k, v_blocks)
        yield
        y_v = _each(lambda g, vb: _mm(jnp.where(tril, g[chunk:], 0.0), vb), g_k, v_blocks)
        yield
        m_rb = [jnp.where(tril, g[chunk:], 0.0) for g in g_b]
        k_tail_t = _each(lambda k_, b, t: jnp.concatenate([k_ * t, b * t], axis=0).T, k_p, b_p, tail)
        state_decay = [jnp.sum(jnp.where(pair_eye, jnp.exp(cl), 0.0), axis=1, keepdims=True) for cl in cum_last]
        inverses = yield from inverting
        return lhs, inverses, from_v, y_v, m_rb, k_tail_t, v_p, state_decay, bonus_p, post

    states = {}

    def load_state(reg, q):
        zero = jnp.zeros((RWKV_N, RWKV_N), F32)
        return jnp.concatenate([jnp.concatenate([s_ref[reg, 2 * q].T, zero], axis=1),
                                jnp.concatenate([zero, s_ref[reg, 2 * q + 1].T], axis=1)], axis=0)

    def finish(group, ci, prepared):
        lhs, inverses, from_v, y_v, m_rb, k_tail_t, v_p, state_decay, bonus_p, post = prepared
        key = group[0]
        if ci == 0:
            states[key] = [load_state(reg, q) for reg in group for q in pairs]
        from_state = _each(_mm, lhs, states[key])
        yield
        u = _each(lambda inv, fs, fv: _mm(inv, blocks(fs[:chunk] + fv)), inverses, from_state, from_v)
        yield
        outer = _each(lambda kt, v_, u_: _mm(kt, jnp.concatenate([v_, -u_], axis=0)), k_tail_t, v_p, u)
        yield
        y_u = _each(lambda m, u_: _mm(m, blocks(u_)), m_rb, u)
        yield
        states[key] = _each(lambda s, d, o: s * d + jnp.where(same_head, o, 0.0), states[key], state_decay, outer)
        y_l = _each(lambda fs, yv, yu: fs[chunk:] + yv - yu, from_state, y_v, y_u)
        mean_l = [head_sums(y) * (1.0 / RWKV_N) for y in y_l]
        cen_l = _each(jnp.subtract, y_l, mean_l)
        var_l = [head_sums(jnp.square(c)) * (1.0 / RWKV_N) for c in cen_l]
        gated = []
        for (rows, ls, gate), cen, var, bonus, v_ in zip(post, cen_l, var_l, bonus_p, v_p):
            y = cen * lax.rsqrt(var + RWKV_GN_EPS) * lnw_ref[:, ls] + lnb_ref[:, ls]
            gated.append((y + bonus * v_) * _silu(gate))
        if ci == geo.chunks - 1:
            for i, reg in enumerate(group):
                for q in pairs:
                    state = states[key][i * n_pairs + q]
                    s_ref[reg, 2 * q] = state[:RWKV_N, :RWKV_N].T
                    s_ref[reg, 2 * q + 1] = state[RWKV_N:, RWKV_N:].T
        return gated

    return prepare, finish


def _alternate(*generators):
    values = [None] * len(generators)
    live = list(range(len(generators)))
    while live:
        for n in list(live):
            try:
                next(generators[n])
            except StopIteration as stop:
                values[n] = stop.value
                live.remove(n)
        yield
    return values


N_RET_CONSTS = 5
N_RWKV_CONSTS = 10


def _even_mixers_kernel(p_ret_ref, p_rwkv_ref, h_ref, cos_ref, sin_ref, s0_ret_ref, s0_rwkv_ref, sh0_ref, *rest, geo):
    ret_consts, rest = rest[:N_RET_CONSTS], rest[N_RET_CONSTS:]
    rwkv_consts, rest = rest[:N_RWKV_CONSTS], rest[N_RWKV_CONSTS:]
    wout_ref, h1_ref, s_ret_ref, s_rwkv_ref, shout_ref = rest
    ret_prepare, ret_finish = _ret_stages(p_ret_ref, cos_ref, sin_ref, s0_ret_ref, *ret_consts, s_ret_ref, geo)
    rwkv_prepare, rwkv_finish = _rwkv_stages(p_rwkv_ref, s0_rwkv_ref, sh0_ref, *rwkv_consts, s_rwkv_ref, shout_ref,
                                             geo)
    n_pairs = RWKV_HEADS // 2

    def prepare(group, ci):
        return _alternate(rwkv_prepare(group, ci), ret_prepare(group, ci))

    def finish(group, ci, prepared):
        rwkv_out, ret_out = yield from _alternate(rwkv_finish(group, ci, prepared[0]),
                                                  ret_finish(group, ci, prepared[1]))
        mixed = jnp.concatenate([jnp.concatenate(ret_out[i * RET_HEADS:(i + 1) * RET_HEADS]
                                                 + rwkv_out[i * n_pairs:(i + 1) * n_pairs], axis=1)
                                 for i in range(len(group))], axis=0)
        out = jnp.dot(mixed.astype(BF16), wout_ref[...], preferred_element_type=F32)
        for i, reg in enumerate(group):
            rows = geo.rows(reg, ci)
            h1_ref[rows] = h_ref[rows] + out[i * geo.chunk:(i + 1) * geo.chunk]

    _run_step(geo, prepare, finish)


def _even_mixers(p_ret, p_rwkv, h0, pos, s_ret, s_rwkv, shift0, ret_norm_w, rwkv_params, w_out, batch, seq, geo):
    cos, sin, decay, q_scale, k_scale, s_scale = _ret_tables(pos, geo)
    ret_consts = [decay, q_scale, k_scale, s_scale, ret_norm_w]
    assert len(ret_consts) == N_RET_CONSTS and len(rwkv_params) == N_RWKV_CONSTS
    h1, ret_new, rwkv_new, shift = _recurrent_call(
        _even_mixers_kernel, geo, batch, seq, [p_ret, p_rwkv, h0], [cos, sin],
        [s_ret, s_rwkv, shift0.reshape(1, batch, 1, SHIFT_W)], ret_consts + list(rwkv_params) + [w_out],
        [D_MODEL], [], "even_mixers")
    return h1, ret_new, rwkv_new, shift.reshape(1, batch, SHIFT_W)


def _gdn_kernel(qkv_ref, z_ref, ba_ref, s0_ref, c0_ref, cw_ref, alog_ref, dtb_ref, gnw_ref,
                o_ref, s_ref, cout_ref, xbuf_ref, *, geo):
    chunk = geo.chunk
    taps = GDN_CONV - 1
    region_rows = geo.region_rows
    stride = region_rows + SUBLANES

    @pl.when(pl.program_id(1) == 0)
    def _():
        s_ref[...] = s0_ref[...]
        cout_ref[...] = c0_ref[...]

    ri = lax.broadcasted_iota(jnp.int32, (chunk, 2 * chunk), 0)
    lane = lax.broadcasted_iota(jnp.int32, (chunk, 2 * chunk), 1)
    ci_ = lane & (chunk - 1)
    first_half = lane < chunk
    tril = ri >= ci_
    strict = ri > ci_
    eye = ri == ci_
    sq_r, sq_c = _square_masks(chunk)
    tril_one = sq_r >= sq_c
    heads = range(GDN_HEADS)
    head_slice = lambda base, h: slice(base + h * GDN_DK, base + (h + 1) * GDN_DK)
    l2 = lambda x: x * lax.rsqrt(jnp.sum(x * x, -1, keepdims=True) + L2_EPS)

    for reg in range(geo.regions):
        base = reg * stride + SUBLANES
        xbuf_ref[base - taps:base, :] = cout_ref[reg]
        xbuf_ref[base:base + region_rows, :] = qkv_ref[reg]
        cout_ref[reg] = qkv_ref[reg, region_rows - taps:region_rows, :]

    def prepare(group, ci):
        q_h, k_h, v_h, beta_h, gc, post = [], [], [], [], [], []
        for reg in group:
            rows = geo.rows(reg, ci)
            at = reg * stride + SUBLANES + ci * chunk
            conv = xbuf_ref[at:at + chunk, :] * cw_ref[taps:taps + 1, :]
            for j in range(taps):
                conv = conv + xbuf_ref[at - taps + j:at - taps + j + chunk, :] * cw_ref[j:j + 1, :]
            act = _silu(conv)
            ba = ba_ref[rows]
            beta = jax.nn.sigmoid(ba[:, :GDN_HEADS])
            g = -jnp.exp(alog_ref[...]) * _softplus(ba[:, GDN_HEADS:] + dtb_ref[...])
            gcum = _chunk_cumsum(tril_one, g)
            q_h += [l2(act[:, head_slice(0, h)]) * (GDN_DK ** -0.5) for h in heads]
            k_h += [l2(act[:, head_slice(GDN_QK, h)]) for h in heads]
            v_h += [act[:, head_slice(2 * GDN_QK, h)] for h in heads]
            beta_h += [beta[:, h:h + 1] for h in heads]
            gc += [gcum[:, h:h + 1] for h in heads]
            post += [(rows, head_slice(0, h)) for h in heads]
        yield
        pair = lambda t: [jnp.concatenate(t[i:i + 2], axis=1) for i in range(0, len(t), 2)]
        kb = _each(jnp.multiply, k_h, beta_h)
        exp_gc = [jnp.exp(g_) for g_ in gc]
        g_last = [g_[chunk - 1:chunk, :] for g_ in gc]
        gc_col = [jnp.where(first_half, gc[i], gc[i + 1]) for i in range(0, len(gc), 2)]
        gc_row = [jnp.sum(jnp.where(eye, g_, 0.0), axis=0, keepdims=True) for g_ in gc_col]
        decay = _each(lambda c, r: jnp.where(tril, jnp.exp(jnp.where(tril, c - r, 0.0)), 0.0), gc_col, gc_row)
        k_blocks = [_block_rows(k_, GDN_DK) for k_ in pair(k_h)]
        lower = _each(lambda kb_, kbl, d: jnp.where(strict, _mm_nt(kb_, kbl) * d, 0.0), pair(kb), k_blocks, decay)
        yield
        attn = _each(lambda q_, kbl, d: _mm_nt(q_, kbl) * d, pair(q_h), k_blocks, decay)
        yield
        q_state = _each(jnp.multiply, q_h, exp_gc)
        k_tail_t = _each(lambda k_, gl, g_: (k_ * jnp.exp(gl - g_)).T, k_h, g_last, gc)
        inverses = yield from _unit_lower_inverses(lower, chunk, width=2)
        rhs = _each(lambda v_, b, kb_, e: jnp.concatenate([v_ * b, kb_ * e], axis=1), v_h, beta_h, kb, exp_gc)
        sol = _each(lambda inv, r_: _mm(inv, _block_rows(r_, GDN_DV + GDN_DK)), inverses, pair(rhs))
        yield
        return q_state, sol, attn, k_tail_t, [jnp.exp(gl) for gl in g_last], post

    states = {}

    def finish(group, ci, prepared):
        q_state, sol, attn, k_tail_t, state_decay, post = prepared
        key = group[0]
        if ci == 0:
            states[key] = [s_ref[reg, h] for reg in group for h in heads]
        width = GDN_DV + GDN_DK
        u_w = [s_[:, i * width:(i + 1) * width] for s_ in sol for i in range(2)]
        v_new = _each(lambda s_, state: s_[:, :GDN_DV] - _mm(s_[:, GDN_DV:], state), u_w, states[key])
        yield
        o_state = _each(_mm, q_state, states[key])
        yield
        outer = _each(_mm, k_tail_t, v_new)
        yield
        v_pairs = [jnp.concatenate(v_new[i:i + 2], axis=1) for i in range(0, len(v_new), 2)]
        o_pairs = _each(lambda a_, v_: _mm(a_, _block_rows(v_, GDN_DV)), attn, v_pairs)
        o_intra = [o_[:, i * GDN_DV:(i + 1) * GDN_DV] for o_ in o_pairs for i in range(2)]
        yield
        states[key] = _each(lambda s, d, o: s * d + o, states[key], state_decay, outer)
        o_l = _each(jnp.add, o_state, o_intra)
        ms_l = [jnp.mean(o * o, -1, keepdims=True) for o in o_l]
        for (rows, os_), o, ms in zip(post, o_l, ms_l):
            o_ref[rows[0], rows[1], os_] = (o * lax.rsqrt(ms + NORM_EPS) * gnw_ref[...]
                                            * _silu(z_ref[rows[0], rows[1], os_]))
        if ci == geo.chunks - 1:
            for i, reg in enumerate(group):
                for h in heads:
                    s_ref[reg, h] = states[key][i * GDN_HEADS + h]

    _run_step(geo, prepare, finish)


def _gdn(qkv, z, ba, s0, conv0, params, batch, seq, geo):
    xbuf = pltpu.VMEM((geo.regions * (geo.region_rows + SUBLANES), GDN_QKV), F32)
    return _recurrent_call(_gdn_kernel, geo, batch, seq, [qkv, z, ba], [], [s0, conv0], list(params),
                           [GDN_VW], [xbuf], "gated_delta")


def _run_group(x, pos, s_ret, s_rwkv, s_shift, s_gdn, s_conv, geo, w):
    batch, seq, _ = x.shape
    n_tokens = batch * seq
    h0 = x.reshape(n_tokens, D_MODEL)
    p_ret, p_rwkv = _token_call(
        _even_in_kernel, n_tokens, {"x": h0}, {"nw": w["norm_e"], "w": w["w_in_e"]},
        [RET_W, RWKV_IN], ["x", "nw", "w"], "even_in", EVEN_IN_TILE)
    h1, ret_new, rwkv_new, shift_new = _even_mixers(
        p_ret, p_rwkv, h0, pos, s_ret, s_rwkv, s_shift, w["ret_norm_w"], w["rwkv_params"], w["w_out_e"],
        batch, seq, geo)
    qkv, z, ba = _token_call(
        _odd_in_kernel, n_tokens, {"h": h1}, {"nw": w["norm_o"], "win": w["w_in_o"]},
        [GDN_QKV, GDN_VW, 2 * GDN_HEADS], ["h", "nw", "win"], "odd_in", ODD_IN_TILE)
    o_gdn, gdn_new, conv_new = _gdn(qkv, z, ba, s_gdn, s_conv, w["gdn_params"], batch, seq, geo)
    (y,) = _token_call(
        _odd_out_final_kernel, n_tokens, {"h": h1, "o": o_gdn}, {"wout": w["w_out_o"], "nw": w["final_norm"]},
        [D_MODEL], ["h", "o", "wout", "nw"], "odd_out_final", ODD_OUT_TILE)
    return y.reshape(batch, seq, D_MODEL), ret_new, rwkv_new, shift_new, gdn_new, conv_new


def kernel(x_prompt, x_sample, state_ret, state_rwkv, state_shift, state_gdn, state_conv, norm_e, w_in_e, rwkv_mu, rwkv_w0, rwkv_w2, rwkv_a0, rwkv_a2, rwkv_kk, rwkv_ka, rwkv_rk, rwkv_ln_w, rwkv_ln_b, ret_norm_w, w_out_e, norm_o, w_in_o, gdn_conv_w, gdn_a_log, gdn_dt_bias, gdn_norm_w, w_out_o, final_norm):
    assert state_ret.shape[0] == 1 and state_gdn.shape[0] == 1, "one even and one odd layer"
    row = lambda a: a.reshape(1, -1)
    w = {
        "norm_e": row(norm_e[0]), "w_in_e": w_in_e[0].astype(BF16), "ret_norm_w": row(ret_norm_w[0]),
        "rwkv_params": (row(rwkv_mu[0]), row(rwkv_w0[0]), rwkv_w2[0], row(rwkv_a0[0]), rwkv_a2[0], row(rwkv_kk[0]),
                        row(rwkv_ka[0]), row(rwkv_rk[0]), row(rwkv_ln_w[0]), row(rwkv_ln_b[0])),
        "w_out_e": w_out_e[0].astype(BF16), "norm_o": row(norm_o[0]),
        "w_in_o": w_in_o[0].astype(BF16),
        "gdn_params": (gdn_conv_w[0], row(gdn_a_log[0]), row(gdn_dt_bias[0]), row(gdn_norm_w[0])),
        "w_out_o": w_out_o[0].astype(BF16), "final_norm": row(final_norm),
    }
    batch, seq, _ = x_prompt.shape
    dec_batch, dec_seq, _ = x_sample.shape
    zeros = lambda s: jnp.zeros((1, batch) + s.shape[2:], F32)
    prompt = _run_group(x_prompt, jnp.arange(seq), zeros(state_ret), zeros(state_rwkv), zeros(state_shift),
                        zeros(state_gdn), zeros(state_conv),
                        _Geometry(math.gcd(seq, PROMPT_CHUNK), PROMPT_SEQS_PER_STEP, PROMPT_CHUNKS_PER_STEP,
                                  PROMPT_SEQS_PER_STEP), w)
    sample = _run_group(x_sample, PAST_LEN + jnp.arange(dec_seq), state_ret, state_rwkv, state_shift,
                        state_gdn, state_conv, _Geometry(dec_seq, SAMPLE_SEQS_PER_STEP, 1, SAMPLE_SEQS_JOINT), w)
    return (prompt[0], sample[0]) + prompt[1:] + sample[1:]
```

```python
import functools
import math
from typing import NamedTuple

import jax
import jax.numpy as jnp
from jax import lax
from jax.experimental import pallas as pl
from jax.experimental.pallas import tpu as pltpu

F32 = jnp.float32
BF16 = jnp.bfloat16

D_MODEL = 1024
PAST_LEN = 16384
RET_HEADS, RET_DK, RET_DV = 4, 64, 128
RET_ROPE_BASE = 10000.0
RET_Q = RET_HEADS * RET_DK
RET_V = RET_HEADS * RET_DV
RET_W = 2 * RET_Q + 2 * RET_V
RWKV_HEADS, RWKV_N = 8, 64
RWKV_W = RWKV_HEADS * RWKV_N
RWKV_LORA = 64
RWKV_GN_EPS = 64e-5
SHIFT_W = 3 * RWKV_W + 2 * RWKV_LORA
RWKV_IN = SHIFT_W + RWKV_W
GDN_HEADS, GDN_DK, GDN_DV, GDN_CONV = 8, 128, 128, 4
GDN_QK = GDN_HEADS * GDN_DK
GDN_VW = GDN_HEADS * GDN_DV
GDN_QKV = 2 * GDN_QK + GDN_VW
NORM_EPS = 1e-6
L2_EPS = 1e-12

SUBLANES = 8
INV_BLOCK = 8
VMEM_LIMIT = 56 * 1024 * 1024
PROMPT_CHUNK = 64
PROMPT_CHUNKS_PER_STEP = 2
PROMPT_SEQS_PER_STEP = 4
SAMPLE_SEQS_PER_STEP = 16
SAMPLE_SEQS_JOINT = 8
PREPARE_STAGES_PER_FINISH_STAGE = 3
EVEN_IN_TILE = 1024
ODD_IN_TILE = 1024
ODD_OUT_TILE = 1024


def _split_bf16(x):
    hi = x.astype(BF16)
    return hi, (x - hi.astype(F32)).astype(BF16)


def _dot(a, b, dims, mode):
    dot = lambda x, y: lax.dot_general(x, y, (dims, ((), ())), preferred_element_type=F32)
    if mode == "bf16":
        return dot(a.astype(BF16), b.astype(BF16))
    assert mode == "bf16x3", mode
    a_hi, a_lo = _split_bf16(a)
    b_hi, b_lo = _split_bf16(b)
    return (dot(a_lo, b_hi) + dot(a_hi, b_lo)) + dot(a_hi, b_hi)


def _mm(a, b, mode="bf16"):
    return _dot(a, b, ((1,), (0,)), mode)


def _mm_nt(a, b, mode="bf16"):
    return _dot(a, b, ((1,), (1,)), mode)


def _mm_tn(a, b, mode="bf16"):
    return _dot(a, b, ((0,), (0,)), mode)


def _mm3(a, b):
    return _mm(a, b, "bf16x3")


def _chunk_cumsum(tril, x):
    ones = tril.astype(BF16)
    hi = x.astype(BF16)
    rest = x - hi.astype(F32)
    mid = rest.astype(BF16)
    lo = (rest - mid.astype(F32)).astype(BF16)
    dot = lambda p: jnp.dot(ones, p, preferred_element_type=F32)
    return (dot(lo) + dot(mid)) + dot(hi)


def _silu(x):
    return x * jax.nn.sigmoid(x)


def _softplus(x):
    return jnp.maximum(x, 0.0) + jnp.log1p(jnp.exp(-jnp.abs(x)))


def _square_masks(n):
    ri = lax.broadcasted_iota(jnp.int32, (n, n), 0)
    ci = lax.broadcasted_iota(jnp.int32, (n, n), 1)
    return ri, ci


def _each(fn, *seqs):
    return [fn(*args) for args in zip(*seqs)]


def _block_rows(x, split):
    low_half = lax.broadcasted_iota(jnp.int32, x.shape, 1) < split
    return jnp.concatenate([jnp.where(low_half, x, 0.0), jnp.where(low_half, 0.0, x)], axis=0)


def _neumann_inverses(lows, eye, n, expand):
    invs = [eye - low for low in lows]
    powers = list(lows)
    k = 2
    while k < n:
        powers = _each(lambda p: _mm(p, expand(p)), powers)
        yield
        invs = _each(lambda inv, p: inv + _mm(inv, expand(p)), invs, powers)
        yield
        k *= 2
    return invs


def _unit_lower_inverses(lows, n, width=1):
    assert width in (1, 2)
    expand = (lambda x: x) if width == 1 else (lambda x: _block_rows(x, n))
    ri = lax.broadcasted_iota(jnp.int32, (n, width * n), 0)
    ci = lax.broadcasted_iota(jnp.int32, (n, width * n), 1) & (n - 1)
    eye = (ri == ci).astype(F32)
    if n <= INV_BLOCK:
        invs = yield from _neumann_inverses(lows, eye, n, expand)
    else:
        shift = int(math.log2(INV_BLOCK))
        same_block = (ri >> shift) == (ci >> shift)
        diag_invs = yield from _neumann_inverses([jnp.where(same_block, low, 0.0) for low in lows], eye, INV_BLOCK,
                                                 expand)
        offs = _each(lambda d, low: _mm(d, expand(jnp.where(same_block, 0.0, low))), diag_invs, lows)
        yield
        off_invs = yield from _neumann_inverses(offs, eye, n // INV_BLOCK, expand)
        invs = _each(lambda o, d: _mm(o, expand(d)), off_invs, diag_invs)
        yield
    residuals = _each(lambda low, inv: (eye - inv) - _mm3(low, expand(inv)), lows, invs)
    yield
    refined = _each(lambda inv, res: inv + _mm(inv, expand(res)), invs, residuals)
    yield
    return refined


def _rmsnorm_rows(x, w):
    return x * lax.rsqrt(jnp.mean(x * x, -1, keepdims=True) + NORM_EPS) * w


def _even_in_kernel(x_ref, nw_ref, w_ref, ret_ref, rwkv_ref):
    xn = _rmsnorm_rows(x_ref[...], nw_ref[...]).astype(BF16)
    ret_ref[...] = jnp.dot(xn, w_ref[:, :RET_W], preferred_element_type=F32)
    rwkv_ref[...] = jnp.dot(xn, w_ref[:, RET_W:], preferred_element_type=F32)


def _odd_in_kernel(h_ref, nw_ref, win_ref, qkv_ref, z_ref, ba_ref):
    xn = _rmsnorm_rows(h_ref[...], nw_ref[...]).astype(BF16)
    qkv_ref[...] = jnp.dot(xn, win_ref[:, :GDN_QKV], preferred_element_type=F32)
    z_ref[...] = jnp.dot(xn, win_ref[:, GDN_QKV:GDN_QKV + GDN_VW], preferred_element_type=F32)
    ba_ref[...] = jnp.dot(xn, win_ref[:, GDN_QKV + GDN_VW:], preferred_element_type=F32)


def _odd_out_final_kernel(h_ref, o_ref, wout_ref, nw_ref, y_ref):
    h2 = h_ref[...] + jnp.dot(o_ref[...].astype(BF16), wout_ref[...], preferred_element_type=F32)
    y_ref[...] = _rmsnorm_rows(h2, nw_ref[...])


def _row_spec(tile, width):
    return pl.BlockSpec((tile, width), lambda i: (i, 0))


def _full_spec(shape):
    return pl.BlockSpec(shape, lambda i: (0,) * len(shape), pipeline_mode=pl.Buffered(1))


def _token_call(kernel, n_tokens, row_inputs, full_inputs, out_widths, order, name, tile):
    tile = min(tile, n_tokens)
    assert n_tokens % tile == 0
    specs = {**{k: _row_spec(tile, v.shape[1]) for k, v in row_inputs.items()},
             **{k: _full_spec(v.shape) for k, v in full_inputs.items()}}
    arrays = {**row_inputs, **full_inputs}
    return pl.pallas_call(
        kernel,
        grid=(n_tokens // tile,),
        in_specs=[specs[k] for k in order],
        out_specs=[_row_spec(tile, w) for w in out_widths],
        out_shape=[jax.ShapeDtypeStruct((n_tokens, w), F32) for w in out_widths],
        compiler_params=pltpu.CompilerParams(dimension_semantics=("parallel",), vmem_limit_bytes=VMEM_LIMIT),
        name=name,
    )(*[arrays[k] for k in order])


class _Geometry(NamedTuple):
    chunk: int
    regions: int
    chunks: int
    joint: int

    @property
    def region_rows(self):
        return self.chunks * self.chunk

    def rows(self, region, chunk_index, lanes=slice(None)):
        return region, slice(chunk_index * self.chunk, (chunk_index + 1) * self.chunk), lanes


def _recurrent_call(body, geo, batch, seq, rows, tables, states, consts, out_widths, scratch_shapes, name):
    steps = seq // geo.region_rows
    assert batch % geo.regions == 0 and steps * geo.region_rows == seq
    row_spec = lambda w: pl.BlockSpec((geo.regions, geo.region_rows, w), lambda b, c: (b, c, 0))
    table_spec = lambda a: pl.BlockSpec((geo.region_rows, a.shape[1]), lambda b, c: (c, 0))
    state_spec = lambda a: pl.BlockSpec((None, geo.regions) + a.shape[2:], lambda b, c: (0, b) + (0,) * (a.ndim - 2))
    const_spec = lambda a: pl.BlockSpec(a.shape, lambda b, c: (0,) * a.ndim)
    outs = pl.pallas_call(
        functools.partial(body, geo=geo),
        grid=(batch // geo.regions, steps),
        in_specs=[row_spec(a.shape[1]) for a in rows] + [table_spec(a) for a in tables]
        + [state_spec(a) for a in states] + [const_spec(a) for a in consts],
        out_specs=[row_spec(w) for w in out_widths] + [state_spec(a) for a in states],
        out_shape=[jax.ShapeDtypeStruct((batch, seq, w), F32) for w in out_widths]
        + [jax.ShapeDtypeStruct(a.shape, F32) for a in states],
        scratch_shapes=scratch_shapes,
        compiler_params=pltpu.CompilerParams(dimension_semantics=("parallel", "arbitrary"),
                                             vmem_limit_bytes=VMEM_LIMIT),
        name=name,
    )(*[a.reshape(batch, seq, a.shape[1]) for a in rows], *tables, *states, *consts)
    return [o.reshape(batch * seq, o.shape[2]) for o in outs[:len(out_widths)]] + list(outs[len(out_widths):])


def _run_step(geo, prepare, finish):
    groups = [list(range(g, g + geo.joint)) for g in range(0, geo.regions, geo.joint)]
    items = [(group, ci) for group in groups for ci in range(geo.chunks)]

    def advance(gen, stages):
        for _ in range(stages):
            try:
                next(gen)
            except StopIteration as stop:
                return True, stop.value
        return False, None

    _, prepared = advance(prepare(*items[0]), 10 ** 6)
    for n, item in enumerate(items):
        finishing = finish(*item, prepared)
        preparing = prepare(*items[n + 1]) if n + 1 < len(items) else None
        finished, prepared_next = False, preparing is None
        prepared = None
        while not (finished and prepared_next):
            if not finished:
                finished, _ = advance(finishing, 1)
            if not prepared_next:
                prepared_next, prepared = advance(preparing, PREPARE_STAGES_PER_FINISH_STAGE)


def _ret_stages(p_ref, cos_ref, sin_ref, s0_ref, dec_ref, qs_ref, ks_ref, ss_ref, nw_ref, s_ref, geo):
    @pl.when(pl.program_id(1) == 0)
    def _():
        s_ref[...] = s0_ref[...]

    lane = lax.broadcasted_iota(jnp.int32, (geo.chunk, RET_Q), 1)
    even = (lane & 1) == 0
    heads = range(RET_HEADS)
    k_slices = [slice(h * RET_DK, (h + 1) * RET_DK) for h in heads]
    v_slices = [slice(h * RET_DV, (h + 1) * RET_DV) for h in heads]

    def rotary(x, cos, sin):
        partner = jnp.where(even, pltpu.roll(x, RET_Q - 1, 1), pltpu.roll(x, 1, 1))
        return x * cos + partner * sin

    def prepare(group, ci):
        q_l, k_l, qs_l, ks_l, v_l, dec_l, ss_l, post = [], [], [], [], [], [], [], []
        for reg in group:
            rows = geo.rows(reg, ci)
            p = p_ref[rows]
            cos, sin = cos_ref[rows[1], :], sin_ref[rows[1], :]
            q = rotary(p[:, :RET_Q], cos, sin)
            k = rotary(p[:, RET_Q:2 * RET_Q], cos, sin) * (RET_DK ** -0.5)
            q_state = q * qs_ref[...]
            k_state = k * ks_ref[...]
            for h in heads:
                q_l.append(q[:, k_slices[h]])
                k_l.append(k[:, k_slices[h]])
                qs_l.append(q_state[:, k_slices[h]])
                ks_l.append(k_state[:, k_slices[h]])
                v_l.append(p[:, 2 * RET_Q + h * RET_DV:2 * RET_Q + (h + 1) * RET_DV])
                dec_l.append(dec_ref[h])
                ss_l.append(ss_ref[h])
                post.append((rows, v_slices[h], p[:, 2 * RET_Q + RET_V + h * RET_DV:
                                                  2 * RET_Q + RET_V + (h + 1) * RET_DV]))
        scores = _each(lambda q_, k_, d: _mm_nt(q_, k_) * d, q_l, k_l, dec_l)
        yield
        intra = _each(_mm, scores, v_l)
        yield
        outer = _each(_mm_tn, ks_l, v_l)
        yield
        return qs_l, ss_l, post, intra, outer

    states = {}

    def finish(group, ci, prepared):
        qs_l, ss_l, post, intra, outer = prepared
        key = group[0]
        if ci == 0:
            states[key] = [s_ref[reg, h] for reg in group for h in heads]
        from_state = _each(_mm, qs_l, states[key])
        yield
        states[key] = _each(lambda s, ss, o: s * ss + o, states[key], ss_l, outer)
        o_l = _each(jnp.add, from_state, intra)
        ms_l = [jnp.mean(o * o, -1, keepdims=True) for o in o_l]
        if ci == geo.chunks - 1:
            for i, reg in enumerate(group):
                for h in heads:
                    s_ref[reg, h] = states[key][i * RET_HEADS + h]
        return [o * lax.rsqrt(ms + NORM_EPS) * nw_ref[:, vs_] * _silu(gate)
                for (rows, vs_, gate), o, ms in zip(post, o_l, ms_l)]

    return prepare, finish


def _ret_tables(pos, geo):
    chunk = geo.chunk
    half = RET_DK // 2
    inv = 1.0 / (RET_ROPE_BASE ** jnp.linspace(0.0, 1.0, half, dtype=F32))
    ang = pos.astype(F32)[:, None] * inv[None, :]
    cos = jnp.repeat(jnp.cos(ang), 2, axis=-1)
    sin = jnp.stack([-jnp.sin(ang), jnp.sin(ang)], -1).reshape(ang.shape[0], RET_DK)
    cos, sin = jnp.tile(cos, (1, RET_HEADS)), jnp.tile(sin, (1, RET_HEADS))
    lg = jnp.log1p(-jnp.exp2(-jnp.linspace(5.0, 12.0, RET_HEADS, dtype=F32)))
    ci = jnp.arange(chunk, dtype=F32)
    diff = ci[:, None] - ci[None, :]
    causal = diff >= 0
    decay = jnp.where(causal, jnp.exp(jnp.where(causal, diff, 0.0) * lg[:, None, None]), 0.0)
    q_scale = jnp.exp((ci + 1.0) * lg[:, None])
    k_scale = jnp.exp((chunk - 1.0 - ci) * lg[:, None])
    s_scale = jnp.exp(chunk * lg)
    widen = lambda t: jnp.repeat(t.T, RET_DK, axis=1)
    s_scale = jnp.broadcast_to(s_scale[:, None, None], (RET_HEADS, 1, RET_DV))
    return cos, sin, decay, widen(q_scale), widen(k_scale), s_scale


def _rwkv_stages(p_ref, s0_ref, sh0_ref, mu_ref, w0_ref, w2_ref, a0_ref, a2_ref, kk_ref, ka_ref, rk_ref,
                 lnw_ref, lnb_ref, s_ref, shout_ref, geo):
    chunk = geo.chunk
    pair_w = 2 * RWKV_N
    n_pairs = RWKV_HEADS // 2

    @pl.when(pl.program_id(1) == 0)
    def _():
        s_ref[...] = s0_ref[...]
        shout_ref[...] = sh0_ref[...]

    ri = lax.broadcasted_iota(jnp.int32, (chunk, 2 * chunk), 0)
    ci_ = lax.broadcasted_iota(jnp.int32, (chunk, 2 * chunk), 1) & (chunk - 1)
    tril = ri >= ci_
    strict = ri > ci_
    sq_r, sq_c = _square_masks(chunk)
    tril_one = sq_r >= sq_c
    first_row = lax.broadcasted_iota(jnp.int32, (chunk, SHIFT_W), 0) == 0
    head0 = lax.broadcasted_iota(jnp.int32, (chunk, pair_w), 1) < RWKV_N
    pr, pc = _square_masks(pair_w)
    pair_eye = pr == pc
    same_head = (pr < RWKV_N) == (pc < RWKV_N)
    pairs = range(n_pairs)
    lanes = [slice(p * pair_w, (p + 1) * pair_w) for p in pairs]
    per_pair = lambda t: [t[:, s] for s in lanes]
    blocks = lambda x: _block_rows(x, RWKV_N)

    def head_sums(x):
        first = jnp.sum(jnp.where(head0, x, 0.0), -1, keepdims=True)
        second = jnp.sum(jnp.where(head0, 0.0, x), -1, keepdims=True)
        return jnp.where(head0, first, second)

    def prepare(group, ci):
        r_p, k_p, v_p, kk_p, b_p, cum_p, ld_p, bonus_p, post = [], [], [], [], [], [], [], [], []
        for reg in group:
            rows = geo.rows(reg, ci)
            p = p_ref[rows]
            sh = p[:, :SHIFT_W]
            before = shout_ref[reg] if ci == 0 else p_ref[reg, rows[1].start - 1:rows[1].start, :SHIFT_W]
            prev = jnp.where(first_row, before, pltpu.roll(sh, 1, 0))
            if ci == geo.chunks - 1:
                shout_ref[reg] = sh[chunk - 1:chunk, :]
            xs = sh + (prev - sh) * mu_ref[...]
            r = xs[:, :RWKV_W]
            k_in = xs[:, RWKV_W:2 * RWKV_W]
            v = xs[:, 2 * RWKV_W:3 * RWKV_W]
            wd = xs[:, 3 * RWKV_W:3 * RWKV_W + RWKV_LORA]
            ad = xs[:, 3 * RWKV_W + RWKV_LORA:]
            w = -_softplus(-(w0_ref[...] + _mm3(jnp.tanh(wd), w2_ref[...]))) - 0.5
            log_decay = -jnp.exp(w)
            a = jax.nn.sigmoid(a0_ref[...] + _mm3(ad, a2_ref[...]))
            k = k_in * (1.0 + (a - 1.0) * ka_ref[...])
            kk = [x * lax.rsqrt(head_sums(x * x) + L2_EPS) for x in per_pair(k_in * kk_ref[...])]
            r_p += per_pair(r)
            k_p += per_pair(k)
            v_p += per_pair(v)
            kk_p += kk
            b_p += _each(jnp.multiply, kk, per_pair(a))
            cum_p += per_pair(_chunk_cumsum(tril_one, log_decay))
            ld_p += per_pair(log_decay)
            bonus_p += [head_sums(x) for x in per_pair(r * k * rk_ref[...])]
            post += [(rows, lanes[q], p[:, SHIFT_W + q * pair_w:SHIFT_W + (q + 1) * pair_w]) for q in pairs]
        yield
        cum_last = [c[chunk - 1:chunk, :] for c in cum_p]
        grow = [jnp.exp(-c) for c in cum_p]
        tail = _each(lambda cl, c: jnp.exp(cl - c), cum_last, cum_p)
        lhs = _each(lambda kk, r_, c, ld: jnp.concatenate([kk * jnp.exp(c - ld), r_ * jnp.exp(c)], axis=0),
                    kk_p, r_p, cum_p, ld_p)
        g_b = _each(lambda x, b, g: _mm_nt(x, blocks(b * g)), lhs, b_p, grow)
        yield
        inverting = _unit_lower_inverses([jnp.where(strict, g[:chunk], 0.0) for g in g_b], chunk, width=2)
        g_k = _each(lambda x, k_, g: _mm_nt(x, blocks(k_ * g)), lhs, k_p, grow)
        yield
        v_blocks = [blocks(v_) for v_ in v_p]
        from_v = _each(lambda g, vb: _mm(jnp.where(strict, g[:chunk], 0.0), vb), g_k, v_blocks)
        yield
        y_v = _each(lambda g, vb: _mm(jnp.where(tril, g[chunk:], 0.0), vb), g_k, v_blocks)
        yield
        m_rb = [jnp.where(tril, g[chunk:], 0.0) for g in g_b]
        k_tail_t = _each(lambda k_, b, t: jnp.concatenate([k_ * t, b * t], axis=0).T, k_p, b_p, tail)
        state_decay = [jnp.sum(jnp.where(pair_eye, jnp.exp(cl), 0.0), axis=1, keepdims=True) for cl in cum_last]
        inverses = yield from inverting
        return lhs, inverses, from_v, y_v, m_rb, k_tail_t, v_p, state_decay, bonus_p, post

    states = {}

    def load_state(reg, q):
        zero = jnp.zeros((RWKV_N, RWKV_N), F32)
        return jnp.concatenate([jnp.concatenate([s_ref[reg, 2 * q].T, zero], axis=1),
                                jnp.concatenate([zero, s_ref[reg, 2 * q + 1].T], axis=1)], axis=0)

    def finish(group, ci, prepared):
        lhs, inverses, from_v, y_v, m_rb, k_tail_t, v_p, state_decay, bonus_p, post = prepared
        key = group[0]
        if ci == 0:
            states[key] = [load_state(reg, q) for reg in group for q in pairs]
        from_state = _each(_mm, lhs, states[key])
        yield
        u = _each(lambda inv, fs, fv: _mm(inv, blocks(fs[:chunk] + fv)), inverses, from_state, from_v)
        yield
        outer = _each(lambda kt, v_, u_: _mm(kt, jnp.concatenate([v_, -u_], axis=0)), k_tail_t, v_p, u)
        yield
        y_u = _each(lambda m, u_: _mm(m, blocks(u_)), m_rb, u)
        yield
        states[key] = _each(lambda s, d, o: s * d + jnp.where(same_head, o, 0.0), states[key], state_decay, outer)
        y_l = _each(lambda fs, yv, yu: fs[chunk:] + yv - yu, from_state, y_v, y_u)
        mean_l = [head_sums(y) * (1.0 / RWKV_N) for y in y_l]
        cen_l = _each(jnp.subtract, y_l, mean_l)
        var_l = [head_sums(jnp.square(c)) * (1.0 / RWKV_N) for c in cen_l]
        gated = []
        for (rows, ls, gate), cen, var, bonus, v_ in zip(post, cen_l, var_l, bonus_p, v_p):
            y = cen * lax.rsqrt(var + RWKV_GN_EPS) * lnw_ref[:, ls] + lnb_ref[:, ls]
            gated.append((y + bonus * v_) * _silu(gate))
        if ci == geo.chunks - 1:
            for i, reg in enumerate(group):
                for q in pairs:
                    state = states[key][i * n_pairs + q]
                    s_ref[reg, 2 * q] = state[:RWKV_N, :RWKV_N].T
                    s_ref[reg, 2 * q + 1] = state[RWKV_N:, RWKV_N:].T
        return gated

    return prepare, finish


def _alternate(*generators):
    values = [None] * len(generators)
    live = list(range(len(generators)))
    while live:
        for n in list(live):
            try:
                next(generators[n])
            except StopIteration as stop:
                values[n] = stop.value
                live.remove(n)
        yield
    return values


N_RET_CONSTS = 5
N_RWKV_CONSTS = 10


def _even_mixers_kernel(p_ret_ref, p_rwkv_ref, h_ref, cos_ref, sin_ref, s0_ret_ref, s0_rwkv_ref, sh0_ref, *rest, geo):
    ret_consts, rest = rest[:N_RET_CONSTS], rest[N_RET_CONSTS:]
    rwkv_consts, rest = rest[:N_RWKV_CONSTS], rest[N_RWKV_CONSTS:]
    wout_ref, h1_ref, s_ret_ref, s_rwkv_ref, shout_ref = rest
    ret_prepare, ret_finish = _ret_stages(p_ret_ref, cos_ref, sin_ref, s0_ret_ref, *ret_consts, s_ret_ref, geo)
    rwkv_prepare, rwkv_finish = _rwkv_stages(p_rwkv_ref, s0_rwkv_ref, sh0_ref, *rwkv_consts, s_rwkv_ref, shout_ref,
                                             geo)
    n_pairs = RWKV_HEADS // 2

    def prepare(group, ci):
        return _alternate(rwkv_prepare(group, ci), ret_prepare(group, ci))

    def finish(group, ci, prepared):
        rwkv_out, ret_out = yield from _alternate(rwkv_finish(group, ci, prepared[0]),
                                                  ret_finish(group, ci, prepared[1]))
        mixed = jnp.concatenate([jnp.concatenate(ret_out[i * RET_HEADS:(i + 1) * RET_HEADS]
                                                 + rwkv_out[i * n_pairs:(i + 1) * n_pairs], axis=1)
                                 for i in range(len(group))], axis=0)
        out = jnp.dot(mixed.astype(BF16), wout_ref[...], preferred_element_type=F32)
        for i, reg in enumerate(group):
            rows = geo.rows(reg, ci)
            h1_ref[rows] = h_ref[rows] + out[i * geo.chunk:(i + 1) * geo.chunk]

    _run_step(geo, prepare, finish)


def _even_mixers(p_ret, p_rwkv, h0, pos, s_ret, s_rwkv, shift0, ret_norm_w, rwkv_params, w_out, batch, seq, geo):
    cos, sin, decay, q_scale, k_scale, s_scale = _ret_tables(pos, geo)
    ret_consts = [decay, q_scale, k_scale, s_scale, ret_norm_w]
    assert len(ret_consts) == N_RET_CONSTS and len(rwkv_params) == N_RWKV_CONSTS
    h1, ret_new, rwkv_new, shift = _recurrent_call(
        _even_mixers_kernel, geo, batch, seq, [p_ret, p_rwkv, h0], [cos, sin],
        [s_ret, s_rwkv, shift0.reshape(1, batch, 1, SHIFT_W)], ret_consts + list(rwkv_params) + [w_out],
        [D_MODEL], [], "even_mixers")
    return h1, ret_new, rwkv_new, shift.reshape(1, batch, SHIFT_W)


def _gdn_kernel(qkv_ref, z_ref, ba_ref, s0_ref, c0_ref, cw_ref, alog_ref, dtb_ref, gnw_ref,
                o_ref, s_ref, cout_ref, xbuf_ref, *, geo):
    chunk = geo.chunk
    taps = GDN_CONV - 1
    region_rows = geo.region_rows
    stride = region_rows + SUBLANES

    @pl.when(pl.program_id(1) == 0)
    def _():
        s_ref[...] = s0_ref[...]
        cout_ref[...] = c0_ref[...]

    ri = lax.broadcasted_iota(jnp.int32, (chunk, 2 * chunk), 0)
    lane = lax.broadcasted_iota(jnp.int32, (chunk, 2 * chunk), 1)
    ci_ = lane & (chunk - 1)
    first_half = lane < chunk
    tril = ri >= ci_
    strict = ri > ci_
    eye = ri == ci_
    sq_r, sq_c = _square_masks(chunk)
    tril_one = sq_r >= sq_c
    heads = range(GDN_HEADS)
    head_slice = lambda base, h: slice(base + h * GDN_DK, base + (h + 1) * GDN_DK)
    l2 = lambda x: x * lax.rsqrt(jnp.sum(x * x, -1, keepdims=True) + L2_EPS)

    for reg in range(geo.regions):
        base = reg * stride + SUBLANES
        xbuf_ref[base - taps:base, :] = cout_ref[reg]
        xbuf_ref[base:base + region_rows, :] = qkv_ref[reg]
        cout_ref[reg] = qkv_ref[reg, region_rows - taps:region_rows, :]

    def prepare(group, ci):
        q_h, k_h, v_h, beta_h, gc, post = [], [], [], [], [], []
        for reg in group:
            rows = geo.rows(reg, ci)
            at = reg * stride + SUBLANES + ci * chunk
            conv = xbuf_ref[at:at + chunk, :] * cw_ref[taps:taps + 1, :]
            for j in range(taps):
                conv = conv + xbuf_ref[at - taps + j:at - taps + j + chunk, :] * cw_ref[j:j + 1, :]
            act = _silu(conv)
            ba = ba_ref[rows]
            beta = jax.nn.sigmoid(ba[:, :GDN_HEADS])
            g = -jnp.exp(alog_ref[...]) * _softplus(ba[:, GDN_HEADS:] + dtb_ref[...])
            gcum = _chunk_cumsum(tril_one, g)
            q_h += [l2(act[:, head_slice(0, h)]) * (GDN_DK ** -0.5) for h in heads]
            k_h += [l2(act[:, head_slice(GDN_QK, h)]) for h in heads]
            v_h += [act[:, head_slice(2 * GDN_QK, h)] for h in heads]
            beta_h += [beta[:, h:h + 1] for h in heads]
            gc += [gcum[:, h:h + 1] for h in heads]
            post += [(rows, head_slice(0, h)) for h in heads]
        yield
        pair = lambda t: [jnp.concatenate(t[i:i + 2], axis=1) for i in range(0, len(t), 2)]
        kb = _each(jnp.multiply, k_h, beta_h)
        exp_gc = [jnp.exp(g_) for g_ in gc]
        g_last = [g_[chunk - 1:chunk, :] for g_ in gc]
        gc_col = [jnp.where(first_half, gc[i], gc[i + 1]) for i in range(0, len(gc), 2)]
        gc_row = [jnp.sum(jnp.where(eye, g_, 0.0), axis=0, keepdims=True) for g_ in gc_col]
        decay = _each(lambda c, r: jnp.where(tril, jnp.exp(jnp.where(tril, c - r, 0.0)), 0.0), gc_col, gc_row)
        k_blocks = [_block_rows(k_, GDN_DK) for k_ in pair(k_h)]
        lower = _each(lambda kb_, kbl, d: jnp.where(strict, _mm_nt(kb_, kbl) * d, 0.0), pair(kb), k_blocks, decay)
        yield
        attn = _each(lambda q_, kbl, d: _mm_nt(q_, kbl) * d, pair(q_h), k_blocks, decay)
        yield
        q_state = _each(jnp.multiply, q_h, exp_gc)
        k_tail_t = _each(lambda k_, gl, g_: (k_ * jnp.exp(gl - g_)).T, k_h, g_last, gc)
        inverses = yield from _unit_lower_inverses(lower, chunk, width=2)
        rhs = _each(lambda v_, b, kb_, e: jnp.concatenate([v_ * b, kb_ * e], axis=1), v_h, beta_h, kb, exp_gc)
        sol = _each(lambda inv, r_: _mm(inv, _block_rows(r_, GDN_DV + GDN_DK)), inverses, pair(rhs))
        yield
        return q_state, sol, attn, k_tail_t, [jnp.exp(gl) for gl in g_last], post

    states = {}

    def finish(group, ci, prepared):
        q_state, sol, attn, k_tail_t, state_decay, post = prepared
        key = group[0]
        if ci == 0:
            states[key] = [s_ref[reg, h] for reg in group for h in heads]
        width = GDN_DV + GDN_DK
        u_w = [s_[:, i * width:(i + 1) * width] for s_ in sol for i in range(2)]
        v_new = _each(lambda s_, state: s_[:, :GDN_DV] - _mm(s_[:, GDN_DV:], state), u_w, states[key])
        yield
        o_state = _each(_mm, q_state, states[key])
        yield
        outer = _each(_mm, k_tail_t, v_new)
        yield
        v_pairs = [jnp.concatenate(v_new[i:i + 2], axis=1) for i in range(0, len(v_new), 2)]
        o_pairs = _each(lambda a_, v_: _mm(a_, _block_rows(v_, GDN_DV)), attn, v_pairs)
        o_intra = [o_[:, i * GDN_DV:(i + 1) * GDN_DV] for o_ in o_pairs for i in range(2)]
        yield
        states[key] = _each(lambda s, d, o: s * d + o, states[key], state_decay, outer)
        o_l = _each(jnp.add, o_state, o_intra)
        ms_l = [jnp.mean(o * o, -1, keepdims=True) for o in o_l]
        for (rows, os_), o, ms in zip(post, o_l, ms_l):
            o_ref[rows[0], rows[1], os_] = (o * lax.rsqrt(ms + NORM_EPS) * gnw_ref[...]
                                            * _silu(z_ref[rows[0], rows[1], os_]))
        if ci == geo.chunks - 1:
            for i, reg in enumerate(group):
                for h in heads:
                    s_ref[reg, h] = states[key][i * GDN_HEADS + h]

    _run_step(geo, prepare, finish)


def _gdn(qkv, z, ba, s0, conv0, params, batch, seq, geo):
    xbuf = pltpu.VMEM((geo.regions * (geo.region_rows + SUBLANES), GDN_QKV), F32)
    return _recurrent_call(_gdn_kernel, geo, batch, seq, [qkv, z, ba], [], [s0, conv0], list(params),
                           [GDN_VW], [xbuf], "gated_delta")


def _run_group(x, pos, s_ret, s_rwkv, s_shift, s_gdn, s_conv, geo, w):
    batch, seq, _ = x.shape
    n_tokens = batch * seq
    h0 = x.reshape(n_tokens, D_MODEL)
    p_ret, p_rwkv = _token_call(
        _even_in_kernel, n_tokens, {"x": h0}, {"nw": w["norm_e"], "w": w["w_in_e"]},
        [RET_W, RWKV_IN], ["x", "nw", "w"], "even_in", EVEN_IN_TILE)
    h1, ret_new, rwkv_new, shift_new = _even_mixers(
        p_ret, p_rwkv, h0, pos, s_ret, s_rwkv, s_shift, w["ret_norm_w"], w["rwkv_params"], w["w_out_e"],
        batch, seq, geo)
    qkv, z, ba = _token_call(
        _odd_in_kernel, n_tokens, {"h": h1}, {"nw": w["norm_o"], "win": w["w_in_o"]},
        [GDN_QKV, GDN_VW, 2 * GDN_HEADS], ["h", "nw", "win"], "odd_in", ODD_IN_TILE)
    o_gdn, gdn_new, conv_new = _gdn(qkv, z, ba, s_gdn, s_conv, w["gdn_params"], batch, seq, geo)
    (y,) = _token_call(
        _odd_out_final_kernel, n_tokens, {"h": h1, "o": o_gdn}, {"wout": w["w_out_o"], "nw": w["final_norm"]},
        [D_MODEL], ["h", "o", "wout", "nw"], "odd_out_final", ODD_OUT_TILE)
    return y.reshape(batch, seq, D_MODEL), ret_new, rwkv_new, shift_new, gdn_new, conv_new


def kernel(x_prompt, x_sample, state_ret, state_rwkv, state_shift, state_gdn, state_conv, norm_e, w_in_e, rwkv_mu, rwkv_w0, rwkv_w2, rwkv_a0, rwkv_a2, rwkv_kk, rwkv_ka, rwkv_rk, rwkv_ln_w, rwkv_ln_b, ret_norm_w, w_out_e, norm_o, w_in_o, gdn_conv_w, gdn_a_log, gdn_dt_bias, gdn_norm_w, w_out_o, final_norm):
    assert state_ret.shape[0] == 1 and state_gdn.shape[0] == 1, "one even and one odd layer"
    row = lambda a: a.reshape(1, -1)
    w = {
        "norm_e": row(norm_e[0]), "w_in_e": w_in_e[0].astype(BF16), "ret_norm_w": row(ret_norm_w[0]),
        "rwkv_params": (row(rwkv_mu[0]), row(rwkv_w0[0]), rwkv_w2[0], row(rwkv_a0[0]), rwkv_a2[0], row(rwkv_kk[0]),
                        row(rwkv_ka[0]), row(rwkv_rk[0]), row(rwkv_ln_w[0]), row(rwkv_ln_b[0])),
        "w_out_e": w_out_e[0].astype(BF16), "norm_o": row(norm_o[0]),
        "w_in_o": w_in_o[0].astype(BF16),
        "gdn_params": (gdn_conv_w[0], row(gdn_a_log[0]), row(gdn_dt_bias[0]), row(gdn_norm_w[0])),
        "w_out_o": w_out_o[0].astype(BF16), "final_norm": row(final_norm),
    }
    batch, seq, _ = x_prompt.shape
    dec_batch, dec_seq, _ = x_sample.shape
    zeros = lambda s: jnp.zeros((1, batch) + s.shape[2:], F32)
    prompt = _run_group(x_prompt, jnp.arange(seq), zeros(state_ret), zeros(state_rwkv), zeros(state_shift),
                        zeros(state_gdn), zeros(state_conv),
                        _Geometry(math.gcd(seq, PROMPT_CHUNK), PROMPT_SEQS_PER_STEP, PROMPT_CHUNKS_PER_STEP,
                                  PROMPT_SEQS_PER_STEP), w)
    sample = _run_group(x_sample, PAST_LEN + jnp.arange(dec_seq), state_ret, state_rwkv, state_shift,
                        state_gdn, state_conv, _Geometry(dec_seq, SAMPLE_SEQS_PER_STEP, 1, SAMPLE_SEQS_JOINT), w)
    return (prompt[0], sample[0]) + prompt[1:] + sample[1:]
```

```python
import functools
import math
from typing import NamedTuple

import jax
import jax.numpy as jnp
from jax import lax
from jax.experimental import pallas as pl
from jax.experimental.pallas import tpu as pltpu

F32 = jnp.float32
BF16 = jnp.bfloat16

D_MODEL = 1024
PAST_LEN = 16384
RET_HEADS, RET_DK, RET_DV = 4, 64, 128
RET_ROPE_BASE = 10000.0
RET_Q = RET_HEADS * RET_DK
RET_V = RET_HEADS * RET_DV
RET_W = 2 * RET_Q + 2 * RET_V
RWKV_HEADS, RWKV_N = 8, 64
RWKV_W = RWKV_HEADS * RWKV_N
RWKV_LORA = 64
RWKV_GN_EPS = 64e-5
SHIFT_W = 3 * RWKV_W + 2 * RWKV_LORA
RWKV_IN = SHIFT_W + RWKV_W
GDN_HEADS, GDN_DK, GDN_DV, GDN_CONV = 8, 128, 128, 4
GDN_QK = GDN_HEADS * GDN_DK
GDN_VW = GDN_HEADS * GDN_DV
GDN_QKV = 2 * GDN_QK + GDN_VW
NORM_EPS = 1e-6
L2_EPS = 1e-12

SUBLANES = 8
INV_BLOCK = 8
VMEM_LIMIT = 56 * 1024 * 1024
PROMPT_CHUNK = 64
PROMPT_CHUNKS_PER_STEP = 2
PROMPT_SEQS_PER_STEP = 4
SAMPLE_SEQS_PER_STEP = 16
SAMPLE_SEQS_JOINT = 8
PREPARE_STAGES_PER_FINISH_STAGE = 3
EVEN_IN_TILE = 1024
ODD_IN_TILE = 1024
ODD_OUT_TILE = 1024


def _split_bf16(x):
    hi = x.astype(BF16)
    return hi, (x - hi.astype(F32)).astype(BF16)


def _dot(a, b, dims, mode):
    dot = lambda x, y: lax.dot_general(x, y, (dims, ((), ())), preferred_element_type=F32)
    if mode == "bf16":
        return dot(a.astype(BF16), b.astype(BF16))
    assert mode == "bf16x3", mode
    a_hi, a_lo = _split_bf16(a)
    b_hi, b_lo = _split_bf16(b)
    return (dot(a_lo, b_hi) + dot(a_hi, b_lo)) + dot(a_hi, b_hi)


def _mm(a, b, mode="bf16"):
    return _dot(a, b, ((1,), (0,)), mode)


def _mm_nt(a, b, mode="bf16"):
    return _dot(a, b, ((1,), (1,)), mode)


def _mm_tn(a, b, mode="bf16"):
    return _dot(a, b, ((0,), (0,)), mode)


def _mm3(a, b):
    return _mm(a, b, "bf16x3")


def _chunk_cumsum(tril, x):
    ones = tril.astype(BF16)
    hi = x.astype(BF16)
    rest = x - hi.astype(F32)
    mid = rest.astype(BF16)
    lo = (rest - mid.astype(F32)).astype(BF16)
    dot = lambda p: jnp.dot(ones, p, preferred_element_type=F32)
    return (dot(lo) + dot(mid)) + dot(hi)


def _silu(x):
    return x * jax.nn.sigmoid(x)


def _softplus(x):
    return jnp.maximum(x, 0.0) + jnp.log1p(jnp.exp(-jnp.abs(x)))


def _square_masks(n):
    ri = lax.broadcasted_iota(jnp.int32, (n, n), 0)
    ci = lax.broadcasted_iota(jnp.int32, (n, n), 1)
    return ri, ci


def _each(fn, *seqs):
    return [fn(*args) for args in zip(*seqs)]


def _block_rows(x, split):
    low_half = lax.broadcasted_iota(jnp.int32, x.shape, 1) < split
    return jnp.concatenate([jnp.where(low_half, x, 0.0), jnp.where(low_half, 0.0, x)], axis=0)


def _neumann_inverses(lows, eye, n, expand):
    invs = [eye - low for low in lows]
    powers = list(lows)
    k = 2
    while k < n:
        powers = _each(lambda p: _mm(p, expand(p)), powers)
        yield
        invs = _each(lambda inv, p: inv + _mm(inv, expand(p)), invs, powers)
        yield
        k *= 2
    return invs


def _unit_lower_inverses(lows, n, width=1):
    assert width in (1, 2)
    expand = (lambda x: x) if width == 1 else (lambda x: _block_rows(x, n))
    ri = lax.broadcasted_iota(jnp.int32, (n, width * n), 0)
    ci = lax.broadcasted_iota(jnp.int32, (n, width * n), 1) & (n - 1)
    eye = (ri == ci).astype(F32)
    if n <= INV_BLOCK:
        invs = yield from _neumann_inverses(lows, eye, n, expand)
    else:
        shift = int(math.log2(INV_BLOCK))
        same_block = (ri >> shift) == (ci >> shift)
        diag_invs = yield from _neumann_inverses([jnp.where(same_block, low, 0.0) for low in lows], eye, INV_BLOCK,
                                                 expand)
        offs = _each(lambda d, low: _mm(d, expand(jnp.where(same_block, 0.0, low))), diag_invs, lows)
        yield
        off_invs = yield from _neumann_inverses(offs, eye, n // INV_BLOCK, expand)
        invs = _each(lambda o, d: _mm(o, expand(d)), off_invs, diag_invs)
        yield
    residuals = _each(lambda low, inv: (eye - inv) - _mm3(low, expand(inv)), lows, invs)
    yield
    refined = _each(lambda inv, res: inv + _mm(inv, expand(res)), invs, residuals)
    yield
    return refined


def _rmsnorm_rows(x, w):
    return x * lax.rsqrt(jnp.mean(x * x, -1, keepdims=True) + NORM_EPS) * w


def _even_in_kernel(x_ref, nw_ref, w_ref, ret_ref, rwkv_ref):
    xn = _rmsnorm_rows(x_ref[...], nw_ref[...]).astype(BF16)
    project = lambda lo, hi: jnp.dot(xn, w_ref[:, lo:hi], preferred_element_type=F32)
    ret_gate = 2 * RET_Q + RET_V
    ret_ref[:, :ret_gate] = project(0, ret_gate)
    ret_ref[:, ret_gate:] = _silu(project(ret_gate, RET_W))
    rwkv_ref[:, :SHIFT_W] = project(RET_W, RET_W + SHIFT_W)
    rwkv_ref[:, SHIFT_W:] = _silu(project(RET_W + SHIFT_W, RET_W + RWKV_IN))


def _odd_in_kernel(h_ref, nw_ref, win_ref, qkv_ref, z_ref, ba_ref):
    xn = _rmsnorm_rows(h_ref[...], nw_ref[...]).astype(BF16)
    qkv_ref[...] = jnp.dot(xn, win_ref[:, :GDN_QKV], preferred_element_type=F32)
    z_ref[...] = _silu(jnp.dot(xn, win_ref[:, GDN_QKV:GDN_QKV + GDN_VW], preferred_element_type=F32))
    ba_ref[...] = jnp.dot(xn, win_ref[:, GDN_QKV + GDN_VW:], preferred_element_type=F32)


def _odd_out_final_kernel(h_ref, o_ref, wout_ref, nw_ref, y_ref):
    h2 = h_ref[...] + jnp.dot(o_ref[...].astype(BF16), wout_ref[...], preferred_element_type=F32)
    y_ref[...] = _rmsnorm_rows(h2, nw_ref[...])


def _row_spec(tile, width):
    return pl.BlockSpec((tile, width), lambda i: (i, 0))


def _full_spec(shape):
    return pl.BlockSpec(shape, lambda i: (0,) * len(shape), pipeline_mode=pl.Buffered(1))


def _token_call(kernel, n_tokens, row_inputs, full_inputs, out_widths, order, name, tile):
    tile = min(tile, n_tokens)
    assert n_tokens % tile == 0
    specs = {**{k: _row_spec(tile, v.shape[1]) for k, v in row_inputs.items()},
             **{k: _full_spec(v.shape) for k, v in full_inputs.items()}}
    arrays = {**row_inputs, **full_inputs}
    return pl.pallas_call(
        kernel,
        grid=(n_tokens // tile,),
        in_specs=[specs[k] for k in order],
        out_specs=[_row_spec(tile, w) for w in out_widths],
        out_shape=[jax.ShapeDtypeStruct((n_tokens, w), F32) for w in out_widths],
        compiler_params=pltpu.CompilerParams(dimension_semantics=("parallel",), vmem_limit_bytes=VMEM_LIMIT),
        name=name,
    )(*[arrays[k] for k in order])


class _Geometry(NamedTuple):
    chunk: int
    regions: int
    chunks: int
    joint: int

    @property
    def region_rows(self):
        return self.chunks * self.chunk

    def rows(self, region, chunk_index, lanes=slice(None)):
        return region, slice(chunk_index * self.chunk, (chunk_index + 1) * self.chunk), lanes


def _recurrent_call(body, geo, batch, seq, rows, tables, states, consts, out_widths, scratch_shapes, name):
    steps = seq // geo.region_rows
    assert batch % geo.regions == 0 and steps * geo.region_rows == seq
    row_spec = lambda w: pl.BlockSpec((geo.regions, geo.region_rows, w), lambda b, c: (b, c, 0))
    table_spec = lambda a: pl.BlockSpec((geo.region_rows, a.shape[1]), lambda b, c: (c, 0))
    state_spec = lambda a: pl.BlockSpec((None, geo.regions) + a.shape[2:], lambda b, c: (0, b) + (0,) * (a.ndim - 2))
    const_spec = lambda a: pl.BlockSpec(a.shape, lambda b, c: (0,) * a.ndim)
    outs = pl.pallas_call(
        functools.partial(body, geo=geo),
        grid=(batch // geo.regions, steps),
        in_specs=[row_spec(a.shape[1]) for a in rows] + [table_spec(a) for a in tables]
        + [state_spec(a) for a in states] + [const_spec(a) for a in consts],
        out_specs=[row_spec(w) for w in out_widths] + [state_spec(a) for a in states],
        out_shape=[jax.ShapeDtypeStruct((batch, seq, w), F32) for w in out_widths]
        + [jax.ShapeDtypeStruct(a.shape, F32) for a in states],
        scratch_shapes=scratch_shapes,
        compiler_params=pltpu.CompilerParams(dimension_semantics=("parallel", "arbitrary"),
                                             vmem_limit_bytes=VMEM_LIMIT),
        name=name,
    )(*[a.reshape(batch, seq, a.shape[1]) for a in rows], *tables, *states, *consts)
    return [o.reshape(batch * seq, o.shape[2]) for o in outs[:len(out_widths)]] + list(outs[len(out_widths):])


def _run_step(geo, prepare, finish):
    groups = [list(range(g, g + geo.joint)) for g in range(0, geo.regions, geo.joint)]
    items = [(group, ci) for group in groups for ci in range(geo.chunks)]

    def advance(gen, stages):
        for _ in range(stages):
            try:
                next(gen)
            except StopIteration as stop:
                return True, stop.value
        return False, None

    _, prepared = advance(prepare(*items[0]), 10 ** 6)
    for n, item in enumerate(items):
        finishing = finish(*item, prepared)
        preparing = prepare(*items[n + 1]) if n + 1 < len(items) else None
        finished, prepared_next = False, preparing is None
        prepared = None
        while not (finished and prepared_next):
            if not finished:
                finished, _ = advance(finishing, 1)
            if not prepared_next:
                prepared_next, prepared = advance(preparing, PREPARE_STAGES_PER_FINISH_STAGE)


def _ret_stages(p_ref, cos_ref, sin_ref, s0_ref, dec_ref, qs_ref, ks_ref, ss_ref, nw_ref, s_ref, geo):
    @pl.when(pl.program_id(1) == 0)
    def _():
        s_ref[...] = s0_ref[...]

    lane = lax.broadcasted_iota(jnp.int32, (geo.chunk, RET_Q), 1)
    even = (lane & 1) == 0
    heads = range(RET_HEADS)
    k_slices = [slice(h * RET_DK, (h + 1) * RET_DK) for h in heads]
    v_slices = [slice(h * RET_DV, (h + 1) * RET_DV) for h in heads]

    def rotary(x, cos, sin):
        partner = jnp.where(even, pltpu.roll(x, RET_Q - 1, 1), pltpu.roll(x, 1, 1))
        return x * cos + partner * sin

    def prepare(group, ci):
        q_l, k_l, qs_l, ks_l, v_l, dec_l, ss_l, post = [], [], [], [], [], [], [], []
        for reg in group:
            rows = geo.rows(reg, ci)
            p = p_ref[rows]
            cos, sin = cos_ref[rows[1], :], sin_ref[rows[1], :]
            q = rotary(p[:, :RET_Q], cos, sin)
            k = rotary(p[:, RET_Q:2 * RET_Q], cos, sin) * (RET_DK ** -0.5)
            q_state = q * qs_ref[...]
            k_state = k * ks_ref[...]
            for h in heads:
                q_l.append(q[:, k_slices[h]])
                k_l.append(k[:, k_slices[h]])
                qs_l.append(q_state[:, k_slices[h]])
                ks_l.append(k_state[:, k_slices[h]])
                v_l.append(p[:, 2 * RET_Q + h * RET_DV:2 * RET_Q + (h + 1) * RET_DV])
                dec_l.append(dec_ref[h])
                ss_l.append(ss_ref[h])
                post.append((rows, v_slices[h], p[:, 2 * RET_Q + RET_V + h * RET_DV:
                                                  2 * RET_Q + RET_V + (h + 1) * RET_DV]))
        scores = _each(lambda q_, k_, d: _mm_nt(q_, k_) * d, q_l, k_l, dec_l)
        yield
        intra = _each(_mm, scores, v_l)
        yield
        outer = _each(_mm_tn, ks_l, v_l)
        yield
        return qs_l, ss_l, post, intra, outer

    states = {}

    def finish(group, ci, prepared):
        qs_l, ss_l, post, intra, outer = prepared
        key = group[0]
        if ci == 0:
            states[key] = [s_ref[reg, h] for reg in group for h in heads]
        from_state = _each(_mm, qs_l, states[key])
        yield
        states[key] = _each(lambda s, ss, o: s * ss + o, states[key], ss_l, outer)
        o_l = _each(jnp.add, from_state, intra)
        ms_l = [jnp.mean(o * o, -1, keepdims=True) for o in o_l]
        if ci == geo.chunks - 1:
            for i, reg in enumerate(group):
                for h in heads:
                    s_ref[reg, h] = states[key][i * RET_HEADS + h]
        return [o * lax.rsqrt(ms + NORM_EPS) * nw_ref[:, vs_] * gate
                for (rows, vs_, gate), o, ms in zip(post, o_l, ms_l)]

    return prepare, finish


def _ret_tables(pos, geo):
    chunk = geo.chunk
    half = RET_DK // 2
    inv = 1.0 / (RET_ROPE_BASE ** jnp.linspace(0.0, 1.0, half, dtype=F32))
    ang = pos.astype(F32)[:, None] * inv[None, :]
    cos = jnp.repeat(jnp.cos(ang), 2, axis=-1)
    sin = jnp.stack([-jnp.sin(ang), jnp.sin(ang)], -1).reshape(ang.shape[0], RET_DK)
    cos, sin = jnp.tile(cos, (1, RET_HEADS)), jnp.tile(sin, (1, RET_HEADS))
    lg = jnp.log1p(-jnp.exp2(-jnp.linspace(5.0, 12.0, RET_HEADS, dtype=F32)))
    ci = jnp.arange(chunk, dtype=F32)
    diff = ci[:, None] - ci[None, :]
    causal = diff >= 0
    decay = jnp.where(causal, jnp.exp(jnp.where(causal, diff, 0.0) * lg[:, None, None]), 0.0)
    q_scale = jnp.exp((ci + 1.0) * lg[:, None])
    k_scale = jnp.exp((chunk - 1.0 - ci) * lg[:, None])
    s_scale = jnp.exp(chunk * lg)
    widen = lambda t: jnp.repeat(t.T, RET_DK, axis=1)
    s_scale = jnp.broadcast_to(s_scale[:, None, None], (RET_HEADS, 1, RET_DV))
    return cos, sin, decay, widen(q_scale), widen(k_scale), s_scale


def _rwkv_stages(p_ref, s0_ref, sh0_ref, mu_ref, w0_ref, w2_ref, a0_ref, a2_ref, kk_ref, ka_ref, rk_ref,
                 lnw_ref, lnb_ref, s_ref, shout_ref, geo):
    chunk = geo.chunk
    pair_w = 2 * RWKV_N
    n_pairs = RWKV_HEADS // 2

    @pl.when(pl.program_id(1) == 0)
    def _():
        s_ref[...] = s0_ref[...]
        shout_ref[...] = sh0_ref[...]

    ri = lax.broadcasted_iota(jnp.int32, (chunk, 2 * chunk), 0)
    ci_ = lax.broadcasted_iota(jnp.int32, (chunk, 2 * chunk), 1) & (chunk - 1)
    tril = ri >= ci_
    strict = ri > ci_
    sq_r, sq_c = _square_masks(chunk)
    tril_one = sq_r >= sq_c
    first_row = lax.broadcasted_iota(jnp.int32, (chunk, SHIFT_W), 0) == 0
    head0 = lax.broadcasted_iota(jnp.int32, (chunk, pair_w), 1) < RWKV_N
    pr, pc = _square_masks(pair_w)
    pair_eye = pr == pc
    same_head = (pr < RWKV_N) == (pc < RWKV_N)
    pairs = range(n_pairs)
    lanes = [slice(p * pair_w, (p + 1) * pair_w) for p in pairs]
    per_pair = lambda t: [t[:, s] for s in lanes]
    blocks = lambda x: _block_rows(x, RWKV_N)

    def head_sums(x):
        first = jnp.sum(jnp.where(head0, x, 0.0), -1, keepdims=True)
        second = jnp.sum(jnp.where(head0, 0.0, x), -1, keepdims=True)
        return jnp.where(head0, first, second)

    def prepare(group, ci):
        r_p, k_p, v_p, kk_p, b_p, cum_p, ld_p, bonus_p, post = [], [], [], [], [], [], [], [], []
        for reg in group:
            rows = geo.rows(reg, ci)
            p = p_ref[rows]
            sh = p[:, :SHIFT_W]
            before = shout_ref[reg] if ci == 0 else p_ref[reg, rows[1].start - 1:rows[1].start, :SHIFT_W]
            prev = jnp.where(first_row, before, pltpu.roll(sh, 1, 0))
            if ci == geo.chunks - 1:
                shout_ref[reg] = sh[chunk - 1:chunk, :]
            xs = sh + (prev - sh) * mu_ref[...]
            r = xs[:, :RWKV_W]
            k_in = xs[:, RWKV_W:2 * RWKV_W]
            v = xs[:, 2 * RWKV_W:3 * RWKV_W]
            wd = xs[:, 3 * RWKV_W:3 * RWKV_W + RWKV_LORA]
            ad = xs[:, 3 * RWKV_W + RWKV_LORA:]
            w = -_softplus(-(w0_ref[...] + _mm3(jnp.tanh(wd), w2_ref[...]))) - 0.5
            log_decay = -jnp.exp(w)
            a = jax.nn.sigmoid(a0_ref[...] + _mm3(ad, a2_ref[...]))
            k = k_in * (1.0 + (a - 1.0) * ka_ref[...])
            kk = [x * lax.rsqrt(head_sums(x * x) + L2_EPS) for x in per_pair(k_in * kk_ref[...])]
            r_p += per_pair(r)
            k_p += per_pair(k)
            v_p += per_pair(v)
            kk_p += kk
            b_p += _each(jnp.multiply, kk, per_pair(a))
            cum_p += per_pair(_chunk_cumsum(tril_one, log_decay))
            ld_p += per_pair(log_decay)
            bonus_p += [head_sums(x) for x in per_pair(r * k * rk_ref[...])]
            post += [(rows, lanes[q], p[:, SHIFT_W + q * pair_w:SHIFT_W + (q + 1) * pair_w]) for q in pairs]
        yield
        cum_last = [c[chunk - 1:chunk, :] for c in cum_p]
        grow = [jnp.exp(-c) for c in cum_p]
        tail = _each(lambda cl, c: jnp.exp(cl - c), cum_last, cum_p)
        lhs = _each(lambda kk, r_, c, ld: jnp.concatenate([kk * jnp.exp(c - ld), r_ * jnp.exp(c)], axis=0),
                    kk_p, r_p, cum_p, ld_p)
        g_b = _each(lambda x, b, g: _mm_nt(x, blocks(b * g)), lhs, b_p, grow)
        yield
        inverting = _unit_lower_inverses([jnp.where(strict, g[:chunk], 0.0) for g in g_b], chunk, width=2)
        g_k = _each(lambda x, k_, g: _mm_nt(x, blocks(k_ * g)), lhs, k_p, grow)
        yield
        v_blocks = [blocks(v_) for v_ in v_p]
        from_v = _each(lambda g, vb: _mm(jnp.where(strict, g[:chunk], 0.0), vb), g_k, v_blocks)
        yield
        y_v = _each(lambda g, vb: _mm(jnp.where(tril, g[chunk:], 0.0), vb), g_k, v_blocks)
        yield
        m_rb = [jnp.where(tril, g[chunk:], 0.0) for g in g_b]
        k_tail_t = _each(lambda k_, b, t: jnp.concatenate([k_ * t, b * t], axis=0).T, k_p, b_p, tail)
        state_decay = [jnp.sum(jnp.where(pair_eye, jnp.exp(cl), 0.0), axis=1, keepdims=True) for cl in cum_last]
        inverses = yield from inverting
        return lhs, inverses, from_v, y_v, m_rb, k_tail_t, v_p, state_decay, bonus_p, post

    states = {}

    def load_state(reg, q):
        zero = jnp.zeros((RWKV_N, RWKV_N), F32)
        return jnp.concatenate([jnp.concatenate([s_ref[reg, 2 * q].T, zero], axis=1),
                                jnp.concatenate([zero, s_ref[reg, 2 * q + 1].T], axis=1)], axis=0)

    def finish(group, ci, prepared):
        lhs, inverses, from_v, y_v, m_rb, k_tail_t, v_p, state_decay, bonus_p, post = prepared
        key = group[0]
        if ci == 0:
            states[key] = [load_state(reg, q) for reg in group for q in pairs]
        from_state = _each(_mm, lhs, states[key])
        yield
        u = _each(lambda inv, fs, fv: _mm(inv, blocks(fs[:chunk] + fv)), inverses, from_state, from_v)
        yield
        outer = _each(lambda kt, v_, u_: _mm(kt, jnp.concatenate([v_, -u_], axis=0)), k_tail_t, v_p, u)
        yield
        y_u = _each(lambda m, u_: _mm(m, blocks(u_)), m_rb, u)
        yield
        states[key] = _each(lambda s, d, o: s * d + jnp.where(same_head, o, 0.0), states[key], state_decay, outer)
        y_l = _each(lambda fs, yv, yu: fs[chunk:] + yv - yu, from_state, y_v, y_u)
        mean_l = [head_sums(y) * (1.0 / RWKV_N) for y in y_l]
        cen_l = _each(jnp.subtract, y_l, mean_l)
        var_l = [head_sums(jnp.square(c)) * (1.0 / RWKV_N) for c in cen_l]
        gated = []
        for (rows, ls, gate), cen, var, bonus, v_ in zip(post, cen_l, var_l, bonus_p, v_p):
            y = cen * lax.rsqrt(var + RWKV_GN_EPS) * lnw_ref[:, ls] + lnb_ref[:, ls]
            gated.append((y + bonus * v_) * gate)
        if ci == geo.chunks - 1:
            for i, reg in enumerate(group):
                for q in pairs:
                    state = states[key][i * n_pairs + q]
                    s_ref[reg, 2 * q] = state[:RWKV_N, :RWKV_N].T
                    s_ref[reg, 2 * q + 1] = state[RWKV_N:, RWKV_N:].T
        return gated

    return prepare, finish


def _alternate(*generators):
    values = [None] * len(generators)
    live = list(range(len(generators)))
    while live:
        for n in list(live):
            try:
                next(generators[n])
            except StopIteration as stop:
                values[n] = stop.value
                live.remove(n)
        yield
    return values


N_RET_CONSTS = 5
N_RWKV_CONSTS = 10


def _even_mixers_kernel(p_ret_ref, p_rwkv_ref, h_ref, cos_ref, sin_ref, s0_ret_ref, s0_rwkv_ref, sh0_ref, *rest, geo):
    ret_consts, rest = rest[:N_RET_CONSTS], rest[N_RET_CONSTS:]
    rwkv_consts, rest = rest[:N_RWKV_CONSTS], rest[N_RWKV_CONSTS:]
    wout_ref, h1_ref, s_ret_ref, s_rwkv_ref, shout_ref = rest
    ret_prepare, ret_finish = _ret_stages(p_ret_ref, cos_ref, sin_ref, s0_ret_ref, *ret_consts, s_ret_ref, geo)
    rwkv_prepare, rwkv_finish = _rwkv_stages(p_rwkv_ref, s0_rwkv_ref, sh0_ref, *rwkv_consts, s_rwkv_ref, shout_ref,
                                             geo)
    n_pairs = RWKV_HEADS // 2

    def prepare(group, ci):
        return _alternate(rwkv_prepare(group, ci), ret_prepare(group, ci))

    def finish(group, ci, prepared):
        rwkv_out, ret_out = yield from _alternate(rwkv_finish(group, ci, prepared[0]),
                                                  ret_finish(group, ci, prepared[1]))
        mixed = jnp.concatenate([jnp.concatenate(ret_out[i * RET_HEADS:(i + 1) * RET_HEADS]
                                                 + rwkv_out[i * n_pairs:(i + 1) * n_pairs], axis=1)
                                 for i in range(len(group))], axis=0)
        out = jnp.dot(mixed.astype(BF16), wout_ref[...], preferred_element_type=F32)
        for i, reg in enumerate(group):
            rows = geo.rows(reg, ci)
            h1_ref[rows] = h_ref[rows] + out[i * geo.chunk:(i + 1) * geo.chunk]

    _run_step(geo, prepare, finish)


def _even_mixers(p_ret, p_rwkv, h0, pos, s_ret, s_rwkv, shift0, ret_norm_w, rwkv_params, w_out, batch, seq, geo):
    cos, sin, decay, q_scale, k_scale, s_scale = _ret_tables(pos, geo)
    ret_consts = [decay, q_scale, k_scale, s_scale, ret_norm_w]
    assert len(ret_consts) == N_RET_CONSTS and len(rwkv_params) == N_RWKV_CONSTS
    h1, ret_new, rwkv_new, shift = _recurrent_call(
        _even_mixers_kernel, geo, batch, seq, [p_ret, p_rwkv, h0], [cos, sin],
        [s_ret, s_rwkv, shift0.reshape(1, batch, 1, SHIFT_W)], ret_consts + list(rwkv_params) + [w_out],
        [D_MODEL], [], "even_mixers")
    return h1, ret_new, rwkv_new, shift.reshape(1, batch, SHIFT_W)


def _gdn_kernel(qkv_ref, z_ref, ba_ref, s0_ref, c0_ref, cw_ref, alog_ref, dtb_ref, gnw_ref,
                o_ref, s_ref, cout_ref, xbuf_ref, *, geo):
    chunk = geo.chunk
    taps = GDN_CONV - 1
    region_rows = geo.region_rows
    stride = region_rows + SUBLANES

    @pl.when(pl.program_id(1) == 0)
    def _():
        s_ref[...] = s0_ref[...]
        cout_ref[...] = c0_ref[...]

    ri = lax.broadcasted_iota(jnp.int32, (chunk, 2 * chunk), 0)
    lane = lax.broadcasted_iota(jnp.int32, (chunk, 2 * chunk), 1)
    ci_ = lane & (chunk - 1)
    first_half = lane < chunk
    tril = ri >= ci_
    strict = ri > ci_
    eye = ri == ci_
    sq_r, sq_c = _square_masks(chunk)
    tril_one = sq_r >= sq_c
    heads = range(GDN_HEADS)
    head_slice = lambda base, h: slice(base + h * GDN_DK, base + (h + 1) * GDN_DK)
    l2 = lambda x: x * lax.rsqrt(jnp.sum(x * x, -1, keepdims=True) + L2_EPS)

    for reg in range(geo.regions):
        base = reg * stride + SUBLANES
        xbuf_ref[base - taps:base, :] = cout_ref[reg]
        xbuf_ref[base:base + region_rows, :] = qkv_ref[reg]
        cout_ref[reg] = qkv_ref[reg, region_rows - taps:region_rows, :]

    def prepare(group, ci):
        q_h, k_h, v_h, beta_h, gc, post = [], [], [], [], [], []
        for reg in group:
            rows = geo.rows(reg, ci)
            at = reg * stride + SUBLANES + ci * chunk
            conv = xbuf_ref[at:at + chunk, :] * cw_ref[taps:taps + 1, :]
            for j in range(taps):
                conv = conv + xbuf_ref[at - taps + j:at - taps + j + chunk, :] * cw_ref[j:j + 1, :]
            act = _silu(conv)
            ba = ba_ref[rows]
            beta = jax.nn.sigmoid(ba[:, :GDN_HEADS])
            g = -jnp.exp(alog_ref[...]) * _softplus(ba[:, GDN_HEADS:] + dtb_ref[...])
            gcum = _chunk_cumsum(tril_one, g)
            q_h += [l2(act[:, head_slice(0, h)]) * (GDN_DK ** -0.5) for h in heads]
            k_h += [l2(act[:, head_slice(GDN_QK, h)]) for h in heads]
            v_h += [act[:, head_slice(2 * GDN_QK, h)] for h in heads]
            beta_h += [beta[:, h:h + 1] for h in heads]
            gc += [gcum[:, h:h + 1] for h in heads]
            post += [(rows, head_slice(0, h)) for h in heads]
        yield
        pair = lambda t: [jnp.concatenate(t[i:i + 2], axis=1) for i in range(0, len(t), 2)]
        kb = _each(jnp.multiply, k_h, beta_h)
        exp_gc = [jnp.exp(g_) for g_ in gc]
        g_last = [g_[chunk - 1:chunk, :] for g_ in gc]
        gc_col = [jnp.where(first_half, gc[i], gc[i + 1]) for i in range(0, len(gc), 2)]
        gc_row = [jnp.sum(jnp.where(eye, g_, 0.0), axis=0, keepdims=True) for g_ in gc_col]
        decay = _each(lambda c, r: jnp.where(tril, jnp.exp(jnp.where(tril, c - r, 0.0)), 0.0), gc_col, gc_row)
        k_blocks = [_block_rows(k_, GDN_DK) for k_ in pair(k_h)]
        lower = _each(lambda kb_, kbl, d: jnp.where(strict, _mm_nt(kb_, kbl) * d, 0.0), pair(kb), k_blocks, decay)
        yield
        attn = _each(lambda q_, kbl, d: _mm_nt(q_, kbl) * d, pair(q_h), k_blocks, decay)
        yield
        q_state = _each(jnp.multiply, q_h, exp_gc)
        k_tail_t = _each(lambda k_, gl, g_: (k_ * jnp.exp(gl - g_)).T, k_h, g_last, gc)
        inverses = yield from _unit_lower_inverses(lower, chunk, width=2)
        rhs = _each(lambda v_, b, kb_, e: jnp.concatenate([v_ * b, kb_ * e], axis=1), v_h, beta_h, kb, exp_gc)
        sol = _each(lambda inv, r_: _mm(inv, _block_rows(r_, GDN_DV + GDN_DK)), inverses, pair(rhs))
        yield
        return q_state, sol, attn, k_tail_t, [jnp.exp(gl) for gl in g_last], post

    states = {}

    def finish(group, ci, prepared):
        q_state, sol, attn, k_tail_t, state_decay, post = prepared
        key = group[0]
        if ci == 0:
            states[key] = [s_ref[reg, h] for reg in group for h in heads]
        width = GDN_DV + GDN_DK
        u_w = [s_[:, i * width:(i + 1) * width] for s_ in sol for i in range(2)]
        v_new = _each(lambda s_, state: s_[:, :GDN_DV] - _mm(s_[:, GDN_DV:], state), u_w, states[key])
        yield
        o_state = _each(_mm, q_state, states[key])
        yield
        outer = _each(_mm, k_tail_t, v_new)
        yield
        v_pairs = [jnp.concatenate(v_new[i:i + 2], axis=1) for i in range(0, len(v_new), 2)]
        o_pairs = _each(lambda a_, v_: _mm(a_, _block_rows(v_, GDN_DV)), attn, v_pairs)
        o_intra = [o_[:, i * GDN_DV:(i + 1) * GDN_DV] for o_ in o_pairs for i in range(2)]
        yield
        states[key] = _each(lambda s, d, o: s * d + o, states[key], state_decay, outer)
        o_l = _each(jnp.add, o_state, o_intra)
        ms_l = [jnp.mean(o * o, -1, keepdims=True) for o in o_l]
        for (rows, os_), o, ms in zip(post, o_l, ms_l):
            o_ref[rows[0], rows[1], os_] = (o * lax.rsqrt(ms + NORM_EPS) * gnw_ref[...]
                                            * z_ref[rows[0], rows[1], os_])
        if ci == geo.chunks - 1:
            for i, reg in enumerate(group):
                for h in heads:
                    s_ref[reg, h] = states[key][i * GDN_HEADS + h]

    _run_step(geo, prepare, finish)


def _gdn(qkv, z, ba, s0, conv0, params, batch, seq, geo):
    xbuf = pltpu.VMEM((geo.regions * (geo.region_rows + SUBLANES), GDN_QKV), F32)
    return _recurrent_call(_gdn_kernel, geo, batch, seq, [qkv, z, ba], [], [s0, conv0], list(params),
                           [GDN_VW], [xbuf], "gated_delta")


def _run_group(x, pos, s_ret, s_rwkv, s_shift, s_gdn, s_conv, geo, w):
    batch, seq, _ = x.shape
    n_tokens = batch * seq
    h0 = x.reshape(n_tokens, D_MODEL)
    p_ret, p_rwkv = _token_call(
        _even_in_kernel, n_tokens, {"x": h0}, {"nw": w["norm_e"], "w": w["w_in_e"]},
        [RET_W, RWKV_IN], ["x", "nw", "w"], "even_in", EVEN_IN_TILE)
    h1, ret_new, rwkv_new, shift_new = _even_mixers(
        p_ret, p_rwkv, h0, pos, s_ret, s_rwkv, s_shift, w["ret_norm_w"], w["rwkv_params"], w["w_out_e"],
        batch, seq, geo)
    qkv, z, ba = _token_call(
        _odd_in_kernel, n_tokens, {"h": h1}, {"nw": w["norm_o"], "win": w["w_in_o"]},
        [GDN_QKV, GDN_VW, 2 * GDN_HEADS], ["h", "nw", "win"], "odd_in", ODD_IN_TILE)
    o_gdn, gdn_new, conv_new = _gdn(qkv, z, ba, s_gdn, s_conv, w["gdn_params"], batch, seq, geo)
    (y,) = _token_call(
        _odd_out_final_kernel, n_tokens, {"h": h1, "o": o_gdn}, {"wout": w["w_out_o"], "nw": w["final_norm"]},
        [D_MODEL], ["h", "o", "wout", "nw"], "odd_out_final", ODD_OUT_TILE)
    return y.reshape(batch, seq, D_MODEL), ret_new, rwkv_new, shift_new, gdn_new, conv_new


def kernel(x_prompt, x_sample, state_ret, state_rwkv, state_shift, state_gdn, state_conv, norm_e, w_in_e, rwkv_mu, rwkv_w0, rwkv_w2, rwkv_a0, rwkv_a2, rwkv_kk, rwkv_ka, rwkv_rk, rwkv_ln_w, rwkv_ln_b, ret_norm_w, w_out_e, norm_o, w_in_o, gdn_conv_w, gdn_a_log, gdn_dt_bias, gdn_norm_w, w_out_o, final_norm):
    assert state_ret.shape[0] == 1 and state_gdn.shape[0] == 1, "one even and one odd layer"
    row = lambda a: a.reshape(1, -1)
    w = {
        "norm_e": row(norm_e[0]), "w_in_e": w_in_e[0].astype(BF16), "ret_norm_w": row(ret_norm_w[0]),
        "rwkv_params": (row(rwkv_mu[0]), row(rwkv_w0[0]), rwkv_w2[0], row(rwkv_a0[0]), rwkv_a2[0], row(rwkv_kk[0]),
                        row(rwkv_ka[0]), row(rwkv_rk[0]), row(rwkv_ln_w[0]), row(rwkv_ln_b[0])),
        "w_out_e": w_out_e[0].astype(BF16), "norm_o": row(norm_o[0]),
        "w_in_o": w_in_o[0].astype(BF16),
        "gdn_params": (gdn_conv_w[0], row(gdn_a_log[0]), row(gdn_dt_bias[0]), row(gdn_norm_w[0])),
        "w_out_o": w_out_o[0].astype(BF16), "final_norm": row(final_norm),
    }
    batch, seq, _ = x_prompt.shape
    dec_batch, dec_seq, _ = x_sample.shape
    zeros = lambda s: jnp.zeros((1, batch) + s.shape[2:], F32)
    prompt = _run_group(x_prompt, jnp.arange(seq), zeros(state_ret), zeros(state_rwkv), zeros(state_shift),
                        zeros(state_gdn), zeros(state_conv),
                        _Geometry(math.gcd(seq, PROMPT_CHUNK), PROMPT_SEQS_PER_STEP, PROMPT_CHUNKS_PER_STEP,
                                  PROMPT_SEQS_PER_STEP), w)
    sample = _run_group(x_sample, PAST_LEN + jnp.arange(dec_seq), state_ret, state_rwkv, state_shift,
                        state_gdn, state_conv, _Geometry(dec_seq, SAMPLE_SEQS_PER_STEP, 1, SAMPLE_SEQS_JOINT), w)
    return (prompt[0], sample[0]) + prompt[1:] + sample[1:]
```

```python
import functools
import math
from typing import NamedTuple

import jax
import jax.numpy as jnp
from jax import lax
from jax.experimental import pallas as pl
from jax.experimental.pallas import tpu as pltpu

F32 = jnp.float32
BF16 = jnp.bfloat16

D_MODEL = 1024
PAST_LEN = 16384
RET_HEADS, RET_DK, RET_DV = 4, 64, 128
RET_ROPE_BASE = 10000.0
RET_Q = RET_HEADS * RET_DK
RET_V = RET_HEADS * RET_DV
RET_W = 2 * RET_Q + 2 * RET_V
RWKV_HEADS, RWKV_N = 8, 64
RWKV_W = RWKV_HEADS * RWKV_N
RWKV_LORA = 64
RWKV_GN_EPS = 64e-5
SHIFT_W = 3 * RWKV_W + 2 * RWKV_LORA
RWKV_IN = SHIFT_W + RWKV_W
GDN_HEADS, GDN_DK, GDN_DV, GDN_CONV = 8, 128, 128, 4
GDN_QK = GDN_HEADS * GDN_DK
GDN_VW = GDN_HEADS * GDN_DV
GDN_QKV = 2 * GDN_QK + GDN_VW
NORM_EPS = 1e-6
L2_EPS = 1e-12

SUBLANES = 8
INV_BLOCK = 8
VMEM_LIMIT = 56 * 1024 * 1024
PROMPT_CHUNK = 64
PROMPT_CHUNKS_PER_STEP = 2
PROMPT_SEQS_PER_STEP = 4
SAMPLE_SEQS_PER_STEP = 16
SAMPLE_SEQS_JOINT = 8
PREPARE_STAGES_PER_FINISH_STAGE = 3
EVEN_IN_TILE = 1024
ODD_IN_TILE = 1024
ODD_OUT_TILE = 1024


def _split_bf16(x):
    hi = x.astype(BF16)
    return hi, (x - hi.astype(F32)).astype(BF16)


def _dot(a, b, dims, mode):
    dot = lambda x, y: lax.dot_general(x, y, (dims, ((), ())), preferred_element_type=F32)
    if mode == "bf16":
        return dot(a.astype(BF16), b.astype(BF16))
    assert mode == "bf16x3", mode
    a_hi, a_lo = _split_bf16(a)
    b_hi, b_lo = _split_bf16(b)
    return (dot(a_lo, b_hi) + dot(a_hi, b_lo)) + dot(a_hi, b_hi)


def _mm(a, b, mode="bf16"):
    return _dot(a, b, ((1,), (0,)), mode)


def _mm_nt(a, b, mode="bf16"):
    return _dot(a, b, ((1,), (1,)), mode)


def _mm_tn(a, b, mode="bf16"):
    return _dot(a, b, ((0,), (0,)), mode)


def _mm3(a, b):
    return _mm(a, b, "bf16x3")


def _chunk_cumsum(tril, x):
    ones = tril.astype(BF16)
    hi = x.astype(BF16)
    rest = x - hi.astype(F32)
    mid = rest.astype(BF16)
    lo = (rest - mid.astype(F32)).astype(BF16)
    dot = lambda p: jnp.dot(ones, p, preferred_element_type=F32)
    return (dot(lo) + dot(mid)) + dot(hi)


def _silu(x):
    return x * jax.nn.sigmoid(x)


def _softplus(x):
    return jnp.maximum(x, 0.0) + jnp.log1p(jnp.exp(-jnp.abs(x)))


def _square_masks(n):
    ri = lax.broadcasted_iota(jnp.int32, (n, n), 0)
    ci = lax.broadcasted_iota(jnp.int32, (n, n), 1)
    return ri, ci


def _each(fn, *seqs):
    return [fn(*args) for args in zip(*seqs)]


def _block_rows(x, split):
    low_half = lax.broadcasted_iota(jnp.int32, x.shape, 1) < split
    return jnp.concatenate([jnp.where(low_half, x, 0.0), jnp.where(low_half, 0.0, x)], axis=0)


def _neumann_inverses(lows, eye, n, expand):
    invs = [eye - low for low in lows]
    powers = list(lows)
    k = 2
    while k < n:
        powers = _each(lambda p: _mm(p, expand(p)), powers)
        yield
        invs = _each(lambda inv, p: inv + _mm(inv, expand(p)), invs, powers)
        yield
        k *= 2
    return invs


def _unit_lower_inverses(lows, n, width=1):
    assert width in (1, 2)
    expand = (lambda x: x) if width == 1 else (lambda x: _block_rows(x, n))
    ri = lax.broadcasted_iota(jnp.int32, (n, width * n), 0)
    ci = lax.broadcasted_iota(jnp.int32, (n, width * n), 1) & (n - 1)
    eye = (ri == ci).astype(F32)
    if n <= INV_BLOCK:
        invs = yield from _neumann_inverses(lows, eye, n, expand)
    else:
        shift = int(math.log2(INV_BLOCK))
        same_block = (ri >> shift) == (ci >> shift)
        diag_invs = yield from _neumann_inverses([jnp.where(same_block, low, 0.0) for low in lows], eye, INV_BLOCK,
                                                 expand)
        offs = _each(lambda d, low: _mm(d, expand(jnp.where(same_block, 0.0, low))), diag_invs, lows)
        yield
        off_invs = yield from _neumann_inverses(offs, eye, n // INV_BLOCK, expand)
        invs = _each(lambda o, d: _mm(o, expand(d)), off_invs, diag_invs)
        yield
    residuals = _each(lambda low, inv: (eye - inv) - _mm3(low, expand(inv)), lows, invs)
    yield
    refined = _each(lambda inv, res: inv + _mm(inv, expand(res)), invs, residuals)
    yield
    return refined


def _rmsnorm_rows(x, w):
    return x * lax.rsqrt(jnp.mean(x * x, -1, keepdims=True) + NORM_EPS) * w


def _even_in_kernel(x_ref, nw_ref, w_ref, ret_ref, rwkv_ref):
    xn = _rmsnorm_rows(x_ref[...], nw_ref[...]).astype(BF16)
    project = lambda lo, hi: jnp.dot(xn, w_ref[:, lo:hi], preferred_element_type=F32)
    ret_gate = 2 * RET_Q + RET_V
    ret_ref[:, :ret_gate] = project(0, ret_gate)
    ret_ref[:, ret_gate:] = _silu(project(ret_gate, RET_W))
    rwkv_ref[:, :SHIFT_W] = project(RET_W, RET_W + SHIFT_W)
    rwkv_ref[:, SHIFT_W:] = _silu(project(RET_W + SHIFT_W, RET_W + RWKV_IN))


def _odd_in_kernel(h_ref, nw_ref, win_ref, qkv_ref, z_ref, ba_ref):
    xn = _rmsnorm_rows(h_ref[...], nw_ref[...]).astype(BF16)
    qkv_ref[...] = jnp.dot(xn, win_ref[:, :GDN_QKV], preferred_element_type=F32)
    z_ref[...] = _silu(jnp.dot(xn, win_ref[:, GDN_QKV:GDN_QKV + GDN_VW], preferred_element_type=F32))
    ba_ref[...] = jnp.dot(xn, win_ref[:, GDN_QKV + GDN_VW:], preferred_element_type=F32)


def _odd_out_final_kernel(h_ref, o_ref, wout_ref, nw_ref, y_ref):
    h2 = h_ref[...] + jnp.dot(o_ref[...].astype(BF16), wout_ref[...], preferred_element_type=F32)
    y_ref[...] = _rmsnorm_rows(h2, nw_ref[...])


def _row_spec(tile, width):
    return pl.BlockSpec((tile, width), lambda i: (i, 0))


def _full_spec(shape):
    return pl.BlockSpec(shape, lambda i: (0,) * len(shape), pipeline_mode=pl.Buffered(1))


def _token_call(kernel, n_tokens, row_inputs, full_inputs, out_widths, order, name, tile):
    tile = min(tile, n_tokens)
    assert n_tokens % tile == 0
    specs = {**{k: _row_spec(tile, v.shape[1]) for k, v in row_inputs.items()},
             **{k: _full_spec(v.shape) for k, v in full_inputs.items()}}
    arrays = {**row_inputs, **full_inputs}
    return pl.pallas_call(
        kernel,
        grid=(n_tokens // tile,),
        in_specs=[specs[k] for k in order],
        out_specs=[_row_spec(tile, w) for w in out_widths],
        out_shape=[jax.ShapeDtypeStruct((n_tokens, w), F32) for w in out_widths],
        compiler_params=pltpu.CompilerParams(dimension_semantics=("parallel",), vmem_limit_bytes=VMEM_LIMIT),
        name=name,
    )(*[arrays[k] for k in order])


class _Geometry(NamedTuple):
    chunk: int
    regions: int
    chunks: int
    joint: int

    @property
    def region_rows(self):
        return self.chunks * self.chunk

    def rows(self, region, chunk_index, lanes=slice(None)):
        return region, slice(chunk_index * self.chunk, (chunk_index + 1) * self.chunk), lanes


def _recurrent_call(body, geo, batch, seq, rows, tables, states, consts, out_widths, scratch_shapes, name):
    steps = seq // geo.region_rows
    assert batch % geo.regions == 0 and steps * geo.region_rows == seq
    row_spec = lambda w: pl.BlockSpec((geo.regions, geo.region_rows, w), lambda b, c: (b, c, 0))
    table_spec = lambda a: pl.BlockSpec((geo.region_rows, a.shape[1]), lambda b, c: (c, 0))
    state_spec = lambda a: pl.BlockSpec((None, geo.regions) + a.shape[2:], lambda b, c: (0, b) + (0,) * (a.ndim - 2))
    const_spec = lambda a: pl.BlockSpec(a.shape, lambda b, c: (0,) * a.ndim)
    outs = pl.pallas_call(
        functools.partial(body, geo=geo),
        grid=(batch // geo.regions, steps),
        in_specs=[row_spec(a.shape[1]) for a in rows] + [table_spec(a) for a in tables]
        + [state_spec(a) for a in states] + [const_spec(a) for a in consts],
        out_specs=[row_spec(w) for w in out_widths] + [state_spec(a) for a in states],
        out_shape=[jax.ShapeDtypeStruct((batch, seq, w), F32) for w in out_widths]
        + [jax.ShapeDtypeStruct(a.shape, F32) for a in states],
        scratch_shapes=scratch_shapes,
        compiler_params=pltpu.CompilerParams(dimension_semantics=("parallel", "arbitrary"),
                                             vmem_limit_bytes=VMEM_LIMIT),
        name=name,
    )(*[a.reshape(batch, seq, a.shape[1]) for a in rows], *tables, *states, *consts)
    return [o.reshape(batch * seq, o.shape[2]) for o in outs[:len(out_widths)]] + list(outs[len(out_widths):])


def _run_step(geo, prepare, finish):
    groups = [list(range(g, g + geo.joint)) for g in range(0, geo.regions, geo.joint)]
    items = [(group, ci) for group in groups for ci in range(geo.chunks)]

    def advance(gen, stages):
        for _ in range(stages):
            try:
                next(gen)
            except StopIteration as stop:
                return True, stop.value
        return False, None

    _, prepared = advance(prepare(*items[0]), 10 ** 6)
    for n, item in enumerate(items):
        finishing = finish(*item, prepared)
        preparing = prepare(*items[n + 1]) if n + 1 < len(items) else None
        finished, prepared_next = False, preparing is None
        prepared = None
        while not (finished and prepared_next):
            if not finished:
                finished, _ = advance(finishing, 1)
            if not prepared_next:
                prepared_next, prepared = advance(preparing, PREPARE_STAGES_PER_FINISH_STAGE)


def _ret_stages(p_ref, cos_ref, sin_ref, s0_ref, dec_ref, qs_ref, ks_ref, ss_ref, nw_ref, s_ref, geo):
    @pl.when(pl.program_id(1) == 0)
    def _():
        s_ref[...] = s0_ref[...]

    lane = lax.broadcasted_iota(jnp.int32, (geo.chunk, RET_Q), 1)
    even = (lane & 1) == 0
    heads = range(RET_HEADS)
    k_slices = [slice(h * RET_DK, (h + 1) * RET_DK) for h in heads]
    v_slices = [slice(h * RET_DV, (h + 1) * RET_DV) for h in heads]

    def rotary(x, cos, sin):
        partner = jnp.where(even, pltpu.roll(x, RET_Q - 1, 1), pltpu.roll(x, 1, 1))
        return x * cos + partner * sin

    def prepare(group, ci):
        q_l, k_l, qs_l, ks_l, v_l, dec_l, ss_l, post = [], [], [], [], [], [], [], []
        for reg in group:
            rows = geo.rows(reg, ci)
            p = p_ref[rows]
            cos, sin = cos_ref[rows[1], :], sin_ref[rows[1], :]
            q = rotary(p[:, :RET_Q], cos, sin)
            k = rotary(p[:, RET_Q:2 * RET_Q], cos, sin) * (RET_DK ** -0.5)
            q_state = q * qs_ref[...]
            k_state = k * ks_ref[...]
            for h in heads:
                q_l.append(q[:, k_slices[h]])
                k_l.append(k[:, k_slices[h]])
                qs_l.append(q_state[:, k_slices[h]])
                ks_l.append(k_state[:, k_slices[h]])
                v_l.append(p[:, 2 * RET_Q + h * RET_DV:2 * RET_Q + (h + 1) * RET_DV])
                dec_l.append(dec_ref[h])
                ss_l.append(ss_ref[h])
                post.append((rows, v_slices[h], p[:, 2 * RET_Q + RET_V + h * RET_DV:
                                                  2 * RET_Q + RET_V + (h + 1) * RET_DV]))
        scores = _each(lambda q_, k_, d: _mm_nt(q_, k_) * d, q_l, k_l, dec_l)
        yield
        intra = _each(_mm, scores, v_l)
        yield
        outer = _each(_mm_tn, ks_l, v_l)
        yield
        return qs_l, ss_l, post, intra, outer

    states = {}

    def finish(group, ci, prepared):
        qs_l, ss_l, post, intra, outer = prepared
        key = group[0]
        if ci == 0:
            states[key] = [s_ref[reg, h] for reg in group for h in heads]
        from_state = _each(_mm, qs_l, states[key])
        yield
        states[key] = _each(lambda s, ss, o: s * ss + o, states[key], ss_l, outer)
        o_l = _each(jnp.add, from_state, intra)
        ms_l = [jnp.mean(o * o, -1, keepdims=True) for o in o_l]
        if ci == geo.chunks - 1:
            for i, reg in enumerate(group):
                for h in heads:
                    s_ref[reg, h] = states[key][i * RET_HEADS + h]
        return [o * lax.rsqrt(ms + NORM_EPS) * nw_ref[:, vs_] * gate
                for (rows, vs_, gate), o, ms in zip(post, o_l, ms_l)]

    return prepare, finish


def _ret_tables(pos, geo):
    chunk = geo.chunk
    half = RET_DK // 2
    inv = 1.0 / (RET_ROPE_BASE ** jnp.linspace(0.0, 1.0, half, dtype=F32))
    ang = pos.astype(F32)[:, None] * inv[None, :]
    cos = jnp.repeat(jnp.cos(ang), 2, axis=-1)
    sin = jnp.stack([-jnp.sin(ang), jnp.sin(ang)], -1).reshape(ang.shape[0], RET_DK)
    cos, sin = jnp.tile(cos, (1, RET_HEADS)), jnp.tile(sin, (1, RET_HEADS))
    lg = jnp.log1p(-jnp.exp2(-jnp.linspace(5.0, 12.0, RET_HEADS, dtype=F32)))
    ci = jnp.arange(chunk, dtype=F32)
    diff = ci[:, None] - ci[None, :]
    causal = diff >= 0
    decay = jnp.where(causal, jnp.exp(jnp.where(causal, diff, 0.0) * lg[:, None, None]), 0.0)
    q_scale = jnp.exp((ci + 1.0) * lg[:, None])
    k_scale = jnp.exp((chunk - 1.0 - ci) * lg[:, None])
    s_scale = jnp.exp(chunk * lg)
    widen = lambda t: jnp.repeat(t.T, RET_DK, axis=1)
    s_scale = jnp.broadcast_to(s_scale[:, None, None], (RET_HEADS, 1, RET_DV))
    return cos, sin, decay, widen(q_scale), widen(k_scale), s_scale


def _rwkv_stages(p_ref, s0_ref, sh0_ref, mu_ref, w0_ref, w2_ref, a0_ref, a2_ref, kk_ref, ka_ref, rk_ref,
                 lnw_ref, lnb_ref, s_ref, shout_ref, geo):
    chunk = geo.chunk
    pair_w = 2 * RWKV_N
    n_pairs = RWKV_HEADS // 2

    @pl.when(pl.program_id(1) == 0)
    def _():
        s_ref[...] = s0_ref[...]
        shout_ref[...] = sh0_ref[...]

    ri = lax.broadcasted_iota(jnp.int32, (chunk, 2 * chunk), 0)
    ci_ = lax.broadcasted_iota(jnp.int32, (chunk, 2 * chunk), 1) & (chunk - 1)
    tril = ri >= ci_
    strict = ri > ci_
    sq_r, sq_c = _square_masks(chunk)
    tril_one = sq_r >= sq_c
    first_row = lax.broadcasted_iota(jnp.int32, (chunk, SHIFT_W), 0) == 0
    head0 = lax.broadcasted_iota(jnp.int32, (chunk, pair_w), 1) < RWKV_N
    pr, pc = _square_masks(pair_w)
    pair_eye = pr == pc
    same_head = (pr < RWKV_N) == (pc < RWKV_N)
    pairs = range(n_pairs)
    lanes = [slice(p * pair_w, (p + 1) * pair_w) for p in pairs]
    per_pair = lambda t: [t[:, s] for s in lanes]
    blocks = lambda x: _block_rows(x, RWKV_N)

    def head_sums(x):
        first = jnp.sum(jnp.where(head0, x, 0.0), -1, keepdims=True)
        second = jnp.sum(jnp.where(head0, 0.0, x), -1, keepdims=True)
        return jnp.where(head0, first, second)

    def prepare(group, ci):
        r_p, k_p, v_p, kk_p, b_p, cum_p, ld_p, bonus_p, post = [], [], [], [], [], [], [], [], []
        for reg in group:
            rows = geo.rows(reg, ci)
            p = p_ref[rows]
            sh = p[:, :SHIFT_W]
            before = shout_ref[reg] if ci == 0 else p_ref[reg, rows[1].start - 1:rows[1].start, :SHIFT_W]
            prev = jnp.where(first_row, before, pltpu.roll(sh, 1, 0))
            if ci == geo.chunks - 1:
                shout_ref[reg] = sh[chunk - 1:chunk, :]
            xs = sh + (prev - sh) * mu_ref[...]
            r = xs[:, :RWKV_W]
            k_in = xs[:, RWKV_W:2 * RWKV_W]
            v = xs[:, 2 * RWKV_W:3 * RWKV_W]
            wd = xs[:, 3 * RWKV_W:3 * RWKV_W + RWKV_LORA]
            ad = xs[:, 3 * RWKV_W + RWKV_LORA:]
            w = -_softplus(-(w0_ref[...] + _mm3(jnp.tanh(wd), w2_ref[...]))) - 0.5
            log_decay = -jnp.exp(w)
            a = jax.nn.sigmoid(a0_ref[...] + _mm3(ad, a2_ref[...]))
            k = k_in * (1.0 + (a - 1.0) * ka_ref[...])
            kk = [x * lax.rsqrt(head_sums(x * x) + L2_EPS) for x in per_pair(k_in * kk_ref[...])]
            r_p += per_pair(r)
            k_p += per_pair(k)
            v_p += per_pair(v)
            kk_p += kk
            b_p += _each(jnp.multiply, kk, per_pair(a))
            cum_p += per_pair(_chunk_cumsum(tril_one, log_decay))
            ld_p += per_pair(log_decay)
            bonus_p += [head_sums(x) for x in per_pair(r * k * rk_ref[...])]
            post += [(rows, lanes[q], p[:, SHIFT_W + q * pair_w:SHIFT_W + (q + 1) * pair_w]) for q in pairs]
        yield
        cum_last = [c[chunk - 1:chunk, :] for c in cum_p]
        grow = [jnp.exp(-c) for c in cum_p]
        tail = _each(lambda cl, c: jnp.exp(cl - c), cum_last, cum_p)
        lhs = _each(lambda kk, r_, c, ld: jnp.concatenate([kk * jnp.exp(c - ld), r_ * jnp.exp(c)], axis=0),
                    kk_p, r_p, cum_p, ld_p)
        g_b = _each(lambda x, b, g: _mm_nt(x, blocks(b * g)), lhs, b_p, grow)
        yield
        inverting = _unit_lower_inverses([jnp.where(strict, g[:chunk], 0.0) for g in g_b], chunk, width=2)
        g_k = _each(lambda x, k_, g: _mm_nt(x, blocks(k_ * g)), lhs, k_p, grow)
        yield
        v_blocks = [blocks(v_) for v_ in v_p]
        from_v = _each(lambda g, vb: _mm(jnp.where(strict, g[:chunk], 0.0), vb), g_k, v_blocks)
        yield
        y_v = _each(lambda g, vb: _mm(jnp.where(tril, g[chunk:], 0.0), vb), g_k, v_blocks)
        yield
        m_rb = [jnp.where(tril, g[chunk:], 0.0) for g in g_b]
        k_tail_t = _each(lambda k_, b, t: jnp.concatenate([k_ * t, b * t], axis=0).T, k_p, b_p, tail)
        state_decay = [jnp.sum(jnp.where(pair_eye, jnp.exp(cl), 0.0), axis=1, keepdims=True) for cl in cum_last]
        inverses = yield from inverting
        return lhs, inverses, from_v, y_v, m_rb, k_tail_t, v_p, state_decay, bonus_p, post

    states = {}

    def load_state(reg, q):
        zero = jnp.zeros((RWKV_N, RWKV_N), F32)
        return jnp.concatenate([jnp.concatenate([s_ref[reg, 2 * q].T, zero], axis=1),
                                jnp.concatenate([zero, s_ref[reg, 2 * q + 1].T], axis=1)], axis=0)

    def finish(group, ci, prepared):
        lhs, inverses, from_v, y_v, m_rb, k_tail_t, v_p, state_decay, bonus_p, post = prepared
        key = group[0]
        if ci == 0:
            states[key] = [load_state(reg, q) for reg in group for q in pairs]
        from_state = _each(_mm, lhs, states[key])
        yield
        u = _each(lambda inv, fs, fv: _mm(inv, blocks(fs[:chunk] + fv)), inverses, from_state, from_v)
        yield
        outer = _each(lambda kt, v_, u_: _mm(kt, jnp.concatenate([v_, -u_], axis=0)), k_tail_t, v_p, u)
        yield
        y_u = _each(lambda m, u_: _mm(m, blocks(u_)), m_rb, u)
        yield
        states[key] = _each(lambda s, d, o: s * d + jnp.where(same_head, o, 0.0), states[key], state_decay, outer)
        y_l = _each(lambda fs, yv, yu: fs[chunk:] + yv - yu, from_state, y_v, y_u)
        mean_l = [head_sums(y) * (1.0 / RWKV_N) for y in y_l]
        cen_l = _each(jnp.subtract, y_l, mean_l)
        var_l = [head_sums(jnp.square(c)) * (1.0 / RWKV_N) for c in cen_l]
        gated = []
        for (rows, ls, gate), cen, var, bonus, v_ in zip(post, cen_l, var_l, bonus_p, v_p):
            y = cen * lax.rsqrt(var + RWKV_GN_EPS) * lnw_ref[:, ls] + lnb_ref[:, ls]
            gated.append((y + bonus * v_) * gate)
        if ci == geo.chunks - 1:
            for i, reg in enumerate(group):
                for q in pairs:
                    state = states[key][i * n_pairs + q]
                    s_ref[reg, 2 * q] = state[:RWKV_N, :RWKV_N].T
                    s_ref[reg, 2 * q + 1] = state[RWKV_N:, RWKV_N:].T
        return gated

    return prepare, finish


def _alternate(*generators):
    values = [None] * len(generators)
    live = list(range(len(generators)))
    while live:
        for n in list(live):
            try:
                next(generators[n])
            except StopIteration as stop:
                values[n] = stop.value
                live.remove(n)
        yield
    return values


N_RET_CONSTS = 5
N_RWKV_CONSTS = 10


def _even_mixers_kernel(p_ret_ref, p_rwkv_ref, h_ref, cos_ref, sin_ref, s0_ret_ref, s0_rwkv_ref, sh0_ref, *rest, geo):
    ret_consts, rest = rest[:N_RET_CONSTS], rest[N_RET_CONSTS:]
    rwkv_consts, rest = rest[:N_RWKV_CONSTS], rest[N_RWKV_CONSTS:]
    wout_ref, h1_ref, s_ret_ref, s_rwkv_ref, shout_ref = rest
    ret_prepare, ret_finish = _ret_stages(p_ret_ref, cos_ref, sin_ref, s0_ret_ref, *ret_consts, s_ret_ref, geo)
    rwkv_prepare, rwkv_finish = _rwkv_stages(p_rwkv_ref, s0_rwkv_ref, sh0_ref, *rwkv_consts, s_rwkv_ref, shout_ref,
                                             geo)
    n_pairs = RWKV_HEADS // 2

    def prepare(group, ci):
        return _alternate(rwkv_prepare(group, ci), ret_prepare(group, ci))

    def finish(group, ci, prepared):
        rwkv_out, ret_out = yield from _alternate(rwkv_finish(group, ci, prepared[0]),
                                                  ret_finish(group, ci, prepared[1]))
        mixed = jnp.concatenate([jnp.concatenate(ret_out[i * RET_HEADS:(i + 1) * RET_HEADS]
                                                 + rwkv_out[i * n_pairs:(i + 1) * n_pairs], axis=1)
                                 for i in range(len(group))], axis=0)
        out = jnp.dot(mixed.astype(BF16), wout_ref[...], preferred_element_type=F32)
        for i, reg in enumerate(group):
            rows = geo.rows(reg, ci)
            h1_ref[rows] = h_ref[rows] + out[i * geo.chunk:(i + 1) * geo.chunk]

    _run_step(geo, prepare, finish)


def _even_mixers(p_ret, p_rwkv, h0, pos, s_ret, s_rwkv, shift0, ret_norm_w, rwkv_params, w_out, batch, seq, geo):
    cos, sin, decay, q_scale, k_scale, s_scale = _ret_tables(pos, geo)
    ret_consts = [decay, q_scale, k_scale, s_scale, ret_norm_w]
    assert len(ret_consts) == N_RET_CONSTS and len(rwkv_params) == N_RWKV_CONSTS
    h1, ret_new, rwkv_new, shift = _recurrent_call(
        _even_mixers_kernel, geo, batch, seq, [p_ret, p_rwkv, h0], [cos, sin],
        [s_ret, s_rwkv, shift0.reshape(1, batch, 1, SHIFT_W)], ret_consts + list(rwkv_params) + [w_out],
        [D_MODEL], [], "even_mixers")
    return h1, ret_new, rwkv_new, shift.reshape(1, batch, SHIFT_W)


def _gdn_kernel(qkv_ref, z_ref, ba_ref, s0_ref, c0_ref, cw_ref, alog_ref, dtb_ref, gnw_ref,
                o_ref, s_ref, cout_ref, xbuf_ref, *, geo):
    chunk = geo.chunk
    taps = GDN_CONV - 1
    region_rows = geo.region_rows
    stride = region_rows + SUBLANES

    @pl.when(pl.program_id(1) == 0)
    def _():
        s_ref[...] = s0_ref[...]
        cout_ref[...] = c0_ref[...]

    ri = lax.broadcasted_iota(jnp.int32, (chunk, 2 * chunk), 0)
    lane = lax.broadcasted_iota(jnp.int32, (chunk, 2 * chunk), 1)
    ci_ = lane & (chunk - 1)
    first_half = lane < chunk
    tril = ri >= ci_
    strict = ri > ci_
    eye = ri == ci_
    sq_r, sq_c = _square_masks(chunk)
    tril_one = sq_r >= sq_c
    heads = range(GDN_HEADS)
    head_slice = lambda base, h: slice(base + h * GDN_DK, base + (h + 1) * GDN_DK)
    l2 = lambda x: x * lax.rsqrt(jnp.sum(x * x, -1, keepdims=True) + L2_EPS)

    for reg in range(geo.regions):
        base = reg * stride + SUBLANES
        xbuf_ref[base - taps:base, :] = cout_ref[reg]
        xbuf_ref[base:base + region_rows, :] = qkv_ref[reg]
        cout_ref[reg] = qkv_ref[reg, region_rows - taps:region_rows, :]

    def prepare(group, ci):
        q_h, k_h, v_h, beta_h, gc, post = [], [], [], [], [], []
        for reg in group:
            rows = geo.rows(reg, ci)
            at = reg * stride + SUBLANES + ci * chunk

            def act_cols(cols, at=at):
                conv = xbuf_ref[at:at + chunk, cols] * cw_ref[taps:taps + 1, cols]
                for j in range(taps):
                    conv = conv + xbuf_ref[at - taps + j:at - taps + j + chunk, cols] * cw_ref[j:j + 1, cols]
                return _silu(conv)

            ba = ba_ref[rows]
            beta = jax.nn.sigmoid(ba[:, :GDN_HEADS])
            g = -jnp.exp(alog_ref[...]) * _softplus(ba[:, GDN_HEADS:] + dtb_ref[...])
            gcum = _chunk_cumsum(tril_one, g)
            q_h += [l2(act_cols(head_slice(0, h))) * (GDN_DK ** -0.5) for h in heads]
            k_h += [l2(act_cols(head_slice(GDN_QK, h))) for h in heads]
            v_h += [act_cols(head_slice(2 * GDN_QK, h)) for h in heads]
            beta_h += [beta[:, h:h + 1] for h in heads]
            gc += [gcum[:, h:h + 1] for h in heads]
            post += [(rows, head_slice(0, h)) for h in heads]
        yield
        pair = lambda t: [jnp.concatenate(t[i:i + 2], axis=1) for i in range(0, len(t), 2)]
        kb = _each(jnp.multiply, k_h, beta_h)
        exp_gc = [jnp.exp(g_) for g_ in gc]
        g_last = [g_[chunk - 1:chunk, :] for g_ in gc]
        gc_col = [jnp.where(first_half, gc[i], gc[i + 1]) for i in range(0, len(gc), 2)]
        gc_row = [jnp.sum(jnp.where(eye, g_, 0.0), axis=0, keepdims=True) for g_ in gc_col]
        decay = _each(lambda c, r: jnp.where(tril, jnp.exp(jnp.where(tril, c - r, 0.0)), 0.0), gc_col, gc_row)
        k_blocks = [_block_rows(k_, GDN_DK) for k_ in pair(k_h)]
        lower = _each(lambda kb_, kbl, d: jnp.where(strict, _mm_nt(kb_, kbl) * d, 0.0), pair(kb), k_blocks, decay)
        yield
        attn = _each(lambda q_, kbl, d: _mm_nt(q_, kbl) * d, pair(q_h), k_blocks, decay)
        yield
        q_state = _each(jnp.multiply, q_h, exp_gc)
        k_tail_t = _each(lambda k_, gl, g_: (k_ * jnp.exp(gl - g_)).T, k_h, g_last, gc)
        inverses = yield from _unit_lower_inverses(lower, chunk, width=2)
        rhs = _each(lambda v_, b, kb_, e: jnp.concatenate([v_ * b, kb_ * e], axis=1), v_h, beta_h, kb, exp_gc)
        sol = _each(lambda inv, r_: _mm(inv, _block_rows(r_, GDN_DV + GDN_DK)), inverses, pair(rhs))
        yield
        return q_state, sol, attn, k_tail_t, [jnp.exp(gl) for gl in g_last], post

    states = {}

    def finish(group, ci, prepared):
        q_state, sol, attn, k_tail_t, state_decay, post = prepared
        key = group[0]
        if ci == 0:
            states[key] = [s_ref[reg, h] for reg in group for h in heads]
        width = GDN_DV + GDN_DK
        u_w = [s_[:, i * width:(i + 1) * width] for s_ in sol for i in range(2)]
        v_new = _each(lambda s_, state: s_[:, :GDN_DV] - _mm(s_[:, GDN_DV:], state), u_w, states[key])
        yield
        o_state = _each(_mm, q_state, states[key])
        yield
        outer = _each(_mm, k_tail_t, v_new)
        yield
        v_pairs = [jnp.concatenate(v_new[i:i + 2], axis=1) for i in range(0, len(v_new), 2)]
        o_pairs = _each(lambda a_, v_: _mm(a_, _block_rows(v_, GDN_DV)), attn, v_pairs)
        o_intra = [o_[:, i * GDN_DV:(i + 1) * GDN_DV] for o_ in o_pairs for i in range(2)]
        yield
        states[key] = _each(lambda s, d, o: s * d + o, states[key], state_decay, outer)
        o_l = _each(jnp.add, o_state, o_intra)
        ms_l = [jnp.mean(o * o, -1, keepdims=True) for o in o_l]
        for (rows, os_), o, ms in zip(post, o_l, ms_l):
            o_ref[rows[0], rows[1], os_] = (o * lax.rsqrt(ms + NORM_EPS) * gnw_ref[...]
                                            * z_ref[rows[0], rows[1], os_])
        if ci == geo.chunks - 1:
            for i, reg in enumerate(group):
                for h in heads:
                    s_ref[reg, h] = states[key][i * GDN_HEADS + h]

    _run_step(geo, prepare, finish)


def _gdn(qkv, z, ba, s0, conv0, params, batch, seq, geo):
    xbuf = pltpu.VMEM((geo.regions * (geo.region_rows + SUBLANES), GDN_QKV), F32)
    return _recurrent_call(_gdn_kernel, geo, batch, seq, [qkv, z, ba], [], [s0, conv0], list(params),
                           [GDN_VW], [xbuf], "gated_delta")


def _run_group(x, pos, s_ret, s_rwkv, s_shift, s_gdn, s_conv, geo, w):
    batch, seq, _ = x.shape
    n_tokens = batch * seq
    h0 = x.reshape(n_tokens, D_MODEL)
    p_ret, p_rwkv = _token_call(
        _even_in_kernel, n_tokens, {"x": h0}, {"nw": w["norm_e"], "w": w["w_in_e"]},
        [RET_W, RWKV_IN], ["x", "nw", "w"], "even_in", EVEN_IN_TILE)
    h1, ret_new, rwkv_new, shift_new = _even_mixers(
        p_ret, p_rwkv, h0, pos, s_ret, s_rwkv, s_shift, w["ret_norm_w"], w["rwkv_params"], w["w_out_e"],
        batch, seq, geo)
    qkv, z, ba = _token_call(
        _odd_in_kernel, n_tokens, {"h": h1}, {"nw": w["norm_o"], "win": w["w_in_o"]},
        [GDN_QKV, GDN_VW, 2 * GDN_HEADS], ["h", "nw", "win"], "odd_in", ODD_IN_TILE)
    o_gdn, gdn_new, conv_new = _gdn(qkv, z, ba, s_gdn, s_conv, w["gdn_params"], batch, seq, geo)
    (y,) = _token_call(
        _odd_out_final_kernel, n_tokens, {"h": h1, "o": o_gdn}, {"wout": w["w_out_o"], "nw": w["final_norm"]},
        [D_MODEL], ["h", "o", "wout", "nw"], "odd_out_final", ODD_OUT_TILE)
    return y.reshape(batch, seq, D_MODEL), ret_new, rwkv_new, shift_new, gdn_new, conv_new


def kernel(x_prompt, x_sample, state_ret, state_rwkv, state_shift, state_gdn, state_conv, norm_e, w_in_e, rwkv_mu, rwkv_w0, rwkv_w2, rwkv_a0, rwkv_a2, rwkv_kk, rwkv_ka, rwkv_rk, rwkv_ln_w, rwkv_ln_b, ret_norm_w, w_out_e, norm_o, w_in_o, gdn_conv_w, gdn_a_log, gdn_dt_bias, gdn_norm_w, w_out_o, final_norm):
    assert state_ret.shape[0] == 1 and state_gdn.shape[0] == 1, "one even and one odd layer"
    row = lambda a: a.reshape(1, -1)
    w = {
        "norm_e": row(norm_e[0]), "w_in_e": w_in_e[0].astype(BF16), "ret_norm_w": row(ret_norm_w[0]),
        "rwkv_params": (row(rwkv_mu[0]), row(rwkv_w0[0]), rwkv_w2[0], row(rwkv_a0[0]), rwkv_a2[0], row(rwkv_kk[0]),
                        row(rwkv_ka[0]), row(rwkv_rk[0]), row(rwkv_ln_w[0]), row(rwkv_ln_b[0])),
        "w_out_e": w_out_e[0].astype(BF16), "norm_o": row(norm_o[0]),
        "w_in_o": w_in_o[0].astype(BF16),
        "gdn_params": (gdn_conv_w[0], row(gdn_a_log[0]), row(gdn_dt_bias[0]), row(gdn_norm_w[0])),
        "w_out_o": w_out_o[0].astype(BF16), "final_norm": row(final_norm),
    }
    batch, seq, _ = x_prompt.shape
    dec_batch, dec_seq, _ = x_sample.shape
    zeros = lambda s: jnp.zeros((1, batch) + s.shape[2:], F32)
    prompt = _run_group(x_prompt, jnp.arange(seq), zeros(state_ret), zeros(state_rwkv), zeros(state_shift),
                        zeros(state_gdn), zeros(state_conv),
                        _Geometry(math.gcd(seq, PROMPT_CHUNK), PROMPT_SEQS_PER_STEP, PROMPT_CHUNKS_PER_STEP,
                                  PROMPT_SEQS_PER_STEP), w)
    sample = _run_group(x_sample, PAST_LEN + jnp.arange(dec_seq), state_ret, state_rwkv, state_shift,
                        state_gdn, state_conv, _Geometry(dec_seq, SAMPLE_SEQS_PER_STEP, 1, SAMPLE_SEQS_JOINT), w)
    return (prompt[0], sample[0]) + prompt[1:] + sample[1:]
```
